```python
import math
import jax, jax.numpy as jnp
from jax import lax
import numpy as np

D_MODEL = 1024
BATCH = 4
SEQ = 4096
DEPTH = 1

CHUNK = 64
EPS = 1e-6

SSM_EXPAND = 2
SSM_INNER = SSM_EXPAND * D_MODEL
SSM_HEAD_DIM = 64
SSM_HEADS = SSM_INNER // SSM_HEAD_DIM
SSM_STATE = 128
SSM_GROUPS = 4
CONV_WIDTH = 4
SSM_XBC = SSM_INNER + 2 * SSM_GROUPS * SSM_STATE

RWKV_DIM = D_MODEL
RWKV_HEAD_DIM = 64
RWKV_HEADS = RWKV_DIM // RWKV_HEAD_DIM
DECAY_LORA = 64
ICLR_LORA = 64
GN_EPS = RWKV_HEAD_DIM * 1e-5
RWKV_COLS = 4 * RWKV_DIM + DECAY_LORA + ICLR_LORA

N_BRANCH = 2
IN_COLS = SSM_INNER + SSM_XBC + SSM_HEADS + RWKV_COLS + N_BRANCH * D_MODEL

kernel_name = "hybrid_ssd_rwkv7_gated_merge"


def rms_norm(x, gain, eps=EPS):
    xf = x.astype(jnp.float32)
    y = xf * lax.rsqrt(jnp.mean(xf * xf, axis=-1, keepdims=True) + eps)
    return (y * gain.astype(jnp.float32)).astype(x.dtype)


def group_rms_norm(y, gain, groups, eps=EPS):
    b, s, c = y.shape
    yg = y.reshape(b, s, groups, c // groups)
    yg = yg * lax.rsqrt(jnp.mean(yg * yg, axis=-1, keepdims=True) + eps)
    return yg.reshape(b, s, c) * gain.astype(jnp.float32)


def causal_depthwise_conv(u, w, bias):
    c = u.shape[-1]
    out = lax.conv_general_dilated(
        u, w[:, None, :].astype(u.dtype), window_strides=(1,),
        padding=[(CONV_WIDTH - 1, 0)], dimension_numbers=("NWC", "WIO", "NWC"),
        feature_group_count=c)
    return out + bias.astype(u.dtype)


def ssd_chunked(xdt, a, bm, cm):
    b, s, h, p = xdt.shape
    g, n = bm.shape[2], bm.shape[3]
    j = h // g
    c = s // CHUNK
    l = CHUNK
    X = xdt.reshape(b, c, l, g, j, p)
    Bq = bm.reshape(b, c, l, g, n)
    Cq = cm.reshape(b, c, l, g, n)
    a_cum = jnp.cumsum(a.reshape(b, c, l, g, j), axis=2)

    causal = jnp.tril(jnp.ones((l, l), dtype=bool))
    seg = a_cum[:, :, :, None] - a_cum[:, :, None, :]
    decay = jnp.exp(jnp.where(causal[None, None, :, :, None, None], seg, -jnp.inf))
    scores = jnp.einsum("bclgn,bcsgn->bclsg", Cq, Bq)
    mix = scores[..., None] * decay
    y_diag = jnp.einsum("bclsgj,bcsgjp->bclgjp", mix, X)

    decay_states = jnp.exp(a_cum[:, :, -1:] - a_cum)
    states = jnp.einsum("bclgn,bclgjp->bcgjpn", Bq, X * decay_states[..., None])
    chunk_decay = jnp.exp(a_cum[:, :, -1])

    def chunk_step(carry, inp):
        st, dec = inp
        return carry * dec[..., None, None] + st, carry

    init = jnp.zeros((b, g, j, p, n), jnp.float32)
    _, prev = lax.scan(chunk_step, init,
                       (jnp.moveaxis(states, 1, 0), jnp.moveaxis(chunk_decay, 1, 0)))
    prev = jnp.moveaxis(prev, 0, 1)

    y_off = jnp.einsum("bclgn,bcgjpn->bclgjp", Cq, prev) * jnp.exp(a_cum)[..., None]
    return (y_diag + y_off).reshape(b, s, h, p)


def mamba2_branch(z, xbc, dt_raw, conv_w, conv_b, dt_bias, a_log, d_skip, norm_gain):
    b, s, _ = z.shape
    xbc = jax.nn.silu(causal_depthwise_conv(xbc, conv_w, conv_b)).astype(jnp.float32)
    xs, bm, cm = jnp.split(xbc, [SSM_INNER, SSM_INNER + SSM_GROUPS * SSM_STATE], axis=-1)
    xs = xs.reshape(b, s, SSM_HEADS, SSM_HEAD_DIM)
    bm = bm.reshape(b, s, SSM_GROUPS, SSM_STATE)
    cm = cm.reshape(b, s, SSM_GROUPS, SSM_STATE)
    dt = jax.nn.softplus(dt_raw.astype(jnp.float32) + dt_bias.astype(jnp.float32))
    A = -jnp.exp(a_log.astype(jnp.float32))
    y = ssd_chunked(xs * dt[..., None], dt * A, bm, cm)
    y = y + d_skip.astype(jnp.float32)[:, None] * xs
    y = y.reshape(b, s, SSM_INNER) * jax.nn.silu(z.astype(jnp.float32))
    return group_rms_norm(y, norm_gain, SSM_GROUPS)


def wkv7_scan(r, w, k, v, kk, a):
    b, s, h, d = r.shape

    def step(S, inp):
        r_t, w_t, k_t, v_t, kk_t, a_t = inp
        s_kk = jnp.einsum("bhvk,bhk->bhv", S, kk_t)
        S = (S * w_t[:, :, None, :]
             - s_kk[..., None] * (kk_t * a_t)[:, :, None, :]
             + v_t[..., None] * k_t[:, :, None, :])
        return S, jnp.einsum("bhvk,bhk->bhv", S, r_t)

    xs = tuple(jnp.moveaxis(t, 1, 0) for t in (r, w, k, v, kk, a))
    S0 = jnp.zeros((b, h, d, d), jnp.float32)
    _, y = lax.scan(step, S0, xs)
    return jnp.moveaxis(y, 0, 1)


def rwkv7_branch(rw, mu, w0, w2, a0, a2, k_k, k_a, r_k, gn_gain, gn_bias):
    b, s, _ = rw.shape
    rw = rw.astype(jnp.float32)
    prev = jnp.pad(rw, ((0, 0), (1, 0), (0, 0)))[:, :-1]
    rw = rw + mu.astype(jnp.float32) * (prev - rw)
    r, k, v, g, wd, ad = jnp.split(
        rw, [RWKV_DIM, 2 * RWKV_DIM, 3 * RWKV_DIM, 4 * RWKV_DIM, 4 * RWKV_DIM + DECAY_LORA], axis=-1)
    wlog = -jax.nn.softplus(-(w0.astype(jnp.float32) + jnp.tanh(wd) @ w2.astype(jnp.float32))) - 0.5
    decay = jnp.exp(-jnp.exp(wlog))
    a = jax.nn.sigmoid(a0.astype(jnp.float32) + ad @ a2.astype(jnp.float32))
    hs = (b, s, RWKV_HEADS, RWKV_HEAD_DIM)
    kk = (k * k_k.astype(jnp.float32)).reshape(hs)
    kk = kk / jnp.maximum(jnp.linalg.norm(kk, axis=-1, keepdims=True), 1e-12)
    k = k * (1.0 + (a - 1.0) * k_a.astype(jnp.float32))
    r, k, v, decay, a = (t.reshape(hs) for t in (r, k, v, decay, a))
    y = wkv7_scan(r, decay, k, v, kk, a)
    mean = jnp.mean(y, axis=-1, keepdims=True)
    var = jnp.mean(jnp.square(y - mean), axis=-1, keepdims=True)
    y = (y - mean) * lax.rsqrt(var + GN_EPS)
    y = y.reshape(b, s, RWKV_DIM) * gn_gain.astype(jnp.float32) + gn_bias.astype(jnp.float32)
    bonus = jnp.sum(r * k * r_k.astype(jnp.float32), axis=-1, keepdims=True) * v
    y = y + bonus.reshape(b, s, RWKV_DIM)
    return y * jax.nn.silu(g)


def hybrid_layer(x, pre_gain, w_in, b_gate, conv_w, conv_b, dt_bias, a_log, d_skip,
                 ssm_norm_gain, rwkv_mu, decay_w0, decay_w2, iclr_a0, iclr_a2, k_k, k_a,
                 r_k, gn_gain, gn_bias, w_branch_ssm, w_branch_rwkv, w_out, post_gain):
    h = rms_norm(x, pre_gain)
    p = jnp.einsum("bsd,de->bse", h, w_in)
    o1 = SSM_INNER
    o2 = o1 + SSM_XBC
    o3 = o2 + SSM_HEADS
    o4 = o3 + RWKV_COLS
    z, xbc, dt_raw, rw, gates = jnp.split(p, [o1, o2, o3, o4], axis=-1)

    y_ssm = mamba2_branch(z, xbc, dt_raw, conv_w, conv_b, dt_bias, a_log, d_skip, ssm_norm_gain)
    y_rwkv = rwkv7_branch(rw, rwkv_mu, decay_w0, decay_w2, iclr_a0, iclr_a2, k_k, k_a,
                          r_k, gn_gain, gn_bias)

    gates = jax.nn.sigmoid((gates + b_gate).astype(jnp.float32))
    g_ssm, g_rwkv = jnp.split(gates, 2, axis=-1)
    merged = (g_ssm * jnp.einsum("bsc,cd->bsd", y_ssm.astype(h.dtype), w_branch_ssm)
              + g_rwkv * jnp.einsum("bsc,cd->bsd", y_rwkv.astype(h.dtype), w_branch_rwkv))
    out = jnp.einsum("bsd,de->bse", merged.astype(h.dtype), w_out)
    return x + rms_norm(out, post_gain).astype(x.dtype)


def setup_inputs(seed: int = 0) -> dict:
    key = jax.random.key(seed)
    ks = jax.random.split(key, 25)
    L = DEPTH
    f32 = jnp.float32

    def nrm(k, shape, scale):
        return jax.random.normal(k, shape, f32) * scale

    def unif(k, shape, lo, hi):
        return jax.random.uniform(k, shape, f32, lo, hi)

    x = nrm(ks[0], (BATCH, SEQ, D_MODEL), 1.0)
    pre_gain = 1.0 + nrm(ks[1], (L, D_MODEL), 0.02)
    w_in = nrm(ks[2], (L, D_MODEL, IN_COLS), D_MODEL ** -0.5)
    b_gate = nrm(ks[3], (L, N_BRANCH * D_MODEL), 0.02)
    conv_w = nrm(ks[4], (L, CONV_WIDTH, SSM_XBC), CONV_WIDTH ** -0.5)
    conv_b = nrm(ks[5], (L, SSM_XBC), 0.02)
    dt0 = jnp.exp(unif(ks[6], (L, SSM_HEADS), math.log(1e-3), math.log(1e-1)))
    dt_bias = dt0 + jnp.log(-jnp.expm1(-dt0))
    a_log = jnp.log(unif(ks[7], (L, SSM_HEADS), 1.0, 16.0))
    d_skip = 1.0 + nrm(ks[8], (L, SSM_HEADS), 0.02)
    ssm_norm_gain = 1.0 + nrm(ks[9], (L, SSM_INNER), 0.02)
    rwkv_mu = unif(ks[10], (L, RWKV_COLS), 0.0, 1.0)
    decay_w0 = unif(ks[11], (L, RWKV_DIM), -6.0, 0.0)
    decay_w2 = nrm(ks[12], (L, DECAY_LORA, RWKV_DIM), 0.1 * DECAY_LORA ** -0.5)
    iclr_a0 = nrm(ks[13], (L, RWKV_DIM), 0.1)
    iclr_a2 = nrm(ks[14], (L, ICLR_LORA, RWKV_DIM), 0.5 * ICLR_LORA ** -0.5)
    k_k = 0.85 + nrm(ks[15], (L, RWKV_DIM), 0.02)
    k_a = 1.0 + nrm(ks[16], (L, RWKV_DIM), 0.02)
    r_k = nrm(ks[17], (L, RWKV_HEADS, RWKV_HEAD_DIM), 0.1)
    gn_gain = 1.0 + nrm(ks[18], (L, RWKV_DIM), 0.02)
    gn_bias = nrm(ks[19], (L, RWKV_DIM), 0.02)
    w_branch_ssm = nrm(ks[20], (L, SSM_INNER, D_MODEL), SSM_INNER ** -0.5)
    w_branch_rwkv = nrm(ks[21], (L, RWKV_DIM, D_MODEL), RWKV_DIM ** -0.5)
    w_out = nrm(ks[22], (L, D_MODEL, D_MODEL), D_MODEL ** -0.5)
    post_gain = 1.0 + nrm(ks[23], (L, D_MODEL), 0.02)
    return {"x": x, "pre_gain": pre_gain, "w_in": w_in, "b_gate": b_gate,
            "conv_w": conv_w, "conv_b": conv_b, "dt_bias": dt_bias, "a_log": a_log,
            "d_skip": d_skip, "ssm_norm_gain": ssm_norm_gain, "rwkv_mu": rwkv_mu,
            "decay_w0": decay_w0, "decay_w2": decay_w2, "iclr_a0": iclr_a0,
            "iclr_a2": iclr_a2, "k_k": k_k, "k_a": k_a, "r_k": r_k,
            "gn_gain": gn_gain, "gn_bias": gn_bias, "w_branch_ssm": w_branch_ssm,
            "w_branch_rwkv": w_branch_rwkv, "w_out": w_out, "post_gain": post_gain}


def reference(x, pre_gain, w_in, b_gate, conv_w, conv_b, dt_bias, a_log, d_skip,
              ssm_norm_gain, rwkv_mu, decay_w0, decay_w2, iclr_a0, iclr_a2, k_k, k_a,
              r_k, gn_gain, gn_bias, w_branch_ssm, w_branch_rwkv, w_out, post_gain):
    for layer in range(DEPTH):
        x = hybrid_layer(
            x, pre_gain[layer], w_in[layer], b_gate[layer], conv_w[layer], conv_b[layer],
            dt_bias[layer], a_log[layer], d_skip[layer], ssm_norm_gain[layer],
            rwkv_mu[layer], decay_w0[layer], decay_w2[layer], iclr_a0[layer],
            iclr_a2[layer], k_k[layer], k_a[layer], r_k[layer], gn_gain[layer],
            gn_bias[layer], w_branch_ssm[layer], w_branch_rwkv[layer], w_out[layer],
            post_gain[layer])
    return x
```

```python
import functools

import jax
import jax.numpy as jnp
from jax import lax
from jax.experimental import pallas as pl
from jax.experimental.pallas import tpu as pltpu

D_MODEL = 1024
EPS = 1e-6

SSM_INNER = 2048
SSM_HEAD_DIM = 64
SSM_HEADS = 32
SSM_STATE = 128
SSM_GROUPS = 4
SSM_GROUP_WIDTH = SSM_INNER // SSM_GROUPS
CONV_WIDTH = 4
SSM_XBC = 3072

RWKV_DIM = 1024
RWKV_HEAD_DIM = 64
RWKV_HEADS = 16
LORA = 64
GN_EPS = RWKV_HEAD_DIM * 1e-5

LANES = 128
SUBLANES = 8

COL_XBC = 0
COL_Z = 3072
COL_R = 5120
COL_GATE = 9216
COL_SMALL = 11264
SMALL_WIDTH = 256
PROJ_COLS = COL_SMALL + SMALL_WIDTH

PROJ_TM = 1024
PROJ_TN = 1280
SSD_CHUNK = 128
RWKV_CHUNK = 64
MERGE_TM = 256
VMEM_LIMIT = 48 * 1024 * 1024

BF16 = jnp.bfloat16
F32 = jnp.float32


def _dot(a, b):
    return jnp.dot(a, b, preferred_element_type=F32)


def _dot_nt(a, b):
    return lax.dot_general(a, b, (((1,), (1,)), ((), ())), preferred_element_type=F32)


def _dot_tn(a, b):
    return lax.dot_general(a, b, (((0,), (0,)), ((), ())), preferred_element_type=F32)


def _split_terms(x, n):
    terms = []
    rem = x
    for _ in range(n):
        t = rem.astype(BF16)
        terms.append(t)
        rem = rem - t.astype(F32)
    return terms


def _dot_left_exact(m_bf16, x, n=3):
    acc = None
    for t in _split_terms(x, n):
        p = _dot(m_bf16, t)
        acc = p if acc is None else acc + p
    return acc


def _dot_right_exact(x, m_bf16, n=2):
    acc = None
    for t in _split_terms(x, n):
        p = _dot(t, m_bf16)
        acc = p if acc is None else acc + p
    return acc


def _softplus(x):
    return jnp.maximum(x, 0.0) + jnp.log(1.0 + jnp.exp(-jnp.abs(x)))


def _sigmoid(x):
    return 1.0 / (1.0 + jnp.exp(-x))


def _lower_tri(n, strict):
    row = lax.broadcasted_iota(jnp.int32, (n, n), 0)
    col = lax.broadcasted_iota(jnp.int32, (n, n), 1)
    return (col < row) if strict else (col <= row)


def _lower_tri_ones(n):
    row = lax.broadcasted_iota(jnp.int32, (n, n), 0)
    col = lax.broadcasted_iota(jnp.int32, (n, n), 1)
    return jnp.clip(row - col + 1, 0, 1).astype(F32).astype(BF16)


def _proj_kernel(x_ref, gain_ref, w_ref, o_ref, h_ref):
    @pl.when(pl.program_id(1) == 0)
    def _():
        x = x_ref[...]
        ms = jnp.mean(x * x, axis=-1, keepdims=True)
        h_ref[...] = (x * lax.rsqrt(ms + EPS) * gain_ref[...]).astype(BF16)

    o_ref[...] = _dot(h_ref[...], w_ref[...])


def _project(x2d, pre_gain, w_proj):
    t = x2d.shape[0]
    return pl.pallas_call(
        _proj_kernel,
        grid=(t // PROJ_TM, PROJ_COLS // PROJ_TN),
        in_specs=[
            pl.BlockSpec((PROJ_TM, D_MODEL), lambda i, j: (i, 0)),
            pl.BlockSpec((1, D_MODEL), lambda i, j: (0, 0)),
            pl.BlockSpec((D_MODEL, PROJ_TN), lambda i, j: (0, j)),
        ],
        out_specs=pl.BlockSpec((PROJ_TM, PROJ_TN), lambda i, j: (i, j)),
        out_shape=jax.ShapeDtypeStruct((t, PROJ_COLS), F32),
        scratch_shapes=[pltpu.VMEM((PROJ_TM, D_MODEL), BF16)],
        compiler_params=pltpu.CompilerParams(
            dimension_semantics=("arbitrary", "arbitrary"), vmem_limit_bytes=VMEM_LIMIT),
        name="in_proj",
    )(x2d, pre_gain, w_proj)


def _ssd_kernel(xbc_ref, z0_ref, z1_ref, sm_ref, convw_ref, convb_ref, dtb_ref, alog_ref,
                dskip_ref, gain_ref, expand_ref, y_ref, xbuf_ref, st_ref, yd_ref):
    L = SSD_CHUNK
    HALO = SUBLANES

    @pl.when(pl.program_id(1) == 0)
    def _():
        xbuf_ref[0:HALO, :] = jnp.zeros((HALO, SSM_XBC), F32)
        st_ref[...] = jnp.zeros_like(st_ref)

    xbuf_ref[HALO:HALO + L, :] = xbc_ref[...]

    def conv_silu(c0, width):
        acc = convb_ref[:, c0:c0 + width]
        for j in range(CONV_WIDTH):
            r0 = HALO - (CONV_WIDTH - 1) + j
            acc = acc + convw_ref[j:j + 1, c0:c0 + width] * xbuf_ref[r0:r0 + L, c0:c0 + width]
        return acc * _sigmoid(acc)

    dt = _softplus(sm_ref[:, LANES:2 * LANES] + dtb_ref[...])
    a = dt * (-jnp.exp(alog_ref[...]))
    tri = _lower_tri(L, strict=False)
    acum = _dot_left_exact(_lower_tri_ones(L), a)
    acum_t = acum.T
    a_last = acum[L - 1:L, :]
    dec_out = jnp.exp(acum)
    dec_state = jnp.exp(a_last - acum)

    for g in range(SSM_GROUPS):
        c0 = g * SSM_GROUP_WIDTH
        ex = expand_ref[:, c0:c0 + SSM_GROUP_WIDTH]
        xs = conv_silu(c0, SSM_GROUP_WIDTH)
        bm = conv_silu(SSM_INNER + g * SSM_STATE, SSM_STATE).astype(BF16)
        cm = conv_silu(SSM_INNER + SSM_GROUPS * SSM_STATE + g * SSM_STATE, SSM_STATE).astype(BF16)
        dt_x = _dot_right_exact(dt, ex)
        dec_out_x = _dot_right_exact(dec_out, ex)
        dec_state_x = _dot_right_exact(dec_state, ex)
        xdt = xs * dt_x
        xdt_b = xdt.astype(BF16)

        scores = _dot_nt(cm, bm)
        state = st_ref[g]
        y_off = _dot(cm, state.astype(BF16)) * dec_out_x
        st_ref[g] = state * dec_out_x[L - 1:L, :] + _dot_tn(bm, (xdt * dec_state_x).astype(BF16))

        for j in range(SSM_GROUP_WIDTH // SSM_HEAD_DIM):
            h = g * (SSM_GROUP_WIDTH // SSM_HEAD_DIM) + j
            seg = acum[:, h:h + 1] - acum_t[h:h + 1, :]
            mix = (scores * jnp.exp(jnp.where(tri, seg, -jnp.inf))).astype(BF16)
            yd_ref[:, j * SSM_HEAD_DIM:(j + 1) * SSM_HEAD_DIM] = _dot(
                mix, xdt_b[:, j * SSM_HEAD_DIM:(j + 1) * SSM_HEAD_DIM])

        y = yd_ref[...] + y_off + dskip_ref[:, c0:c0 + SSM_GROUP_WIDTH] * xs
        z_ref = z0_ref if g < 2 else z1_ref
        zc = (g % 2) * SSM_GROUP_WIDTH
        z = z_ref[:, zc:zc + SSM_GROUP_WIDTH]
        y = y * (z * _sigmoid(z))
        ms = jnp.mean(y * y, axis=-1, keepdims=True)
        y_ref[:, c0:c0 + SSM_GROUP_WIDTH] = y * lax.rsqrt(ms + EPS) * gain_ref[:, c0:c0 + SSM_GROUP_WIDTH]

    xbuf_ref[0:HALO, :] = xbuf_ref[L:L + HALO, :]


def _ssd_branch(p, batch, seq, conv_w, conv_b, dt_bias, a_log, dskip_x, norm_gain, expand):
    L = SSD_CHUNK
    nc = seq // L
    row = lambda b, c: b * nc + c
    const = lambda shape: pl.BlockSpec(shape, lambda b, c: (0, 0))
    return pl.pallas_call(
        _ssd_kernel,
        grid=(batch, nc),
        in_specs=[
            pl.BlockSpec((L, SSM_XBC), lambda b, c: (row(b, c), COL_XBC // SSM_XBC)),
            pl.BlockSpec((L, 1024), lambda b, c: (row(b, c), COL_Z // 1024)),
            pl.BlockSpec((L, 1024), lambda b, c: (row(b, c), COL_Z // 1024 + 1)),
            pl.BlockSpec((L, SMALL_WIDTH), lambda b, c: (row(b, c), COL_SMALL // SMALL_WIDTH)),
            const((CONV_WIDTH, SSM_XBC)),
            const((1, SSM_XBC)),
            const((1, LANES)),
            const((1, LANES)),
            const((1, SSM_INNER)),
            const((1, SSM_INNER)),
            const((LANES, SSM_INNER)),
        ],
        out_specs=pl.BlockSpec((L, SSM_INNER), lambda b, c: (row(b, c), 0)),
        out_shape=jax.ShapeDtypeStruct((batch * seq, SSM_INNER), F32),
        scratch_shapes=[
            pltpu.VMEM((L + SUBLANES, SSM_XBC), F32),
            pltpu.VMEM((SSM_GROUPS, SSM_STATE, SSM_GROUP_WIDTH), F32),
            pltpu.VMEM((L, SSM_GROUP_WIDTH), F32),
        ],
        compiler_params=pltpu.CompilerParams(
            dimension_semantics=("arbitrary", "arbitrary"), vmem_limit_bytes=VMEM_LIMIT),
        name="ssd_branch",
    )(p, p, p, p, conv_w, conv_b, dt_bias, a_log, dskip_x, norm_gain, expand)


def _rwkv_kernel(r_ref, k_ref, v_ref, g_ref, sm_ref, mur_ref, muk_ref, muv_ref, mug_ref, musm_ref,
                 w0_ref, w2_ref, a0_ref, a2_ref, kks_ref, kas_ref, rk_ref, gng_ref, gnb_ref,
                 y_ref, cr_ref, ck_ref, cv_ref, cg_ref, csm_ref, s_ref):
    C = RWKV_CHUNK
    D = RWKV_HEAD_DIM

    @pl.when(pl.program_id(1) == 0)
    def _():
        for c_ref in (cr_ref, ck_ref, cv_ref, cg_ref, csm_ref):
            c_ref[...] = jnp.zeros_like(c_ref)
        s_ref[...] = jnp.zeros_like(s_ref)

    def shifted(x_ref, carry_ref, mu_ref):
        x = x_ref[...]
        first = lax.broadcasted_iota(jnp.int32, x.shape, 0) == 0
        prev = jnp.where(first, carry_ref[...], pltpu.roll(x, 1, 0))
        carry_ref[...] = x[C - 1:C, :]
        return x + mu_ref[...] * (prev - x)

    r = shifted(r_ref, cr_ref, mur_ref)
    k = shifted(k_ref, ck_ref, muk_ref)
    v = shifted(v_ref, cv_ref, muv_ref)
    gate = shifted(g_ref, cg_ref, mug_ref)
    sm = shifted(sm_ref, csm_ref, musm_ref)
    wd = sm[:, 0:LORA]
    ad = sm[:, LORA:2 * LORA]

    lw = _dot(jnp.tanh(wd).astype(BF16), w2_ref[...])
    logw = -jnp.exp(-_softplus(-(w0_ref[...] + lw)) - 0.5)
    a = _sigmoid(a0_ref[...] + _dot(ad.astype(BF16), a2_ref[...]))
    kkp = k * kks_ref[...]
    k2 = k * (1.0 + (a - 1.0) * kas_ref[...])

    cum = _dot_left_exact(_lower_tri_ones(C), logw)
    cum_last = cum[C - 1:C, :]
    e_pos = jnp.exp(cum)
    e_neg = jnp.exp(-cum)
    e_end = jnp.exp(cum_last - cum)
    kc_p = kkp * jnp.exp(cum - logw)
    rt = r * e_pos
    kt = k2 * e_neg
    bt_p = kkp * a * e_neg
    ktg = k2 * e_end
    btg_p = kkp * a * e_end
    gamma_end = jnp.exp(cum_last)
    rkk = r * k2 * rk_ref[...]
    gsilu = gate * _sigmoid(gate)

    row2 = lax.broadcasted_iota(jnp.int32, (2 * C, 2 * C), 0)
    col2 = lax.broadcasted_iota(jnp.int32, (2 * C, 2 * C), 1)
    colm = jnp.where(col2 >= C, col2 - C, col2)
    gmask = colm < jnp.where(row2 >= C, row2 - C + 1, row2)
    rowh = lax.broadcasted_iota(jnp.int32, (C, 2 * C), 0)
    colh = lax.broadcasted_iota(jnp.int32, (C, 2 * C), 1)
    left = colh < C
    eye_right = jnp.where(colh == rowh + C, 1.0, 0.0)
    zeros_cd = jnp.zeros((C, D), F32)

    for h in range(RWKV_HEADS):
        sl = slice(h * D, (h + 1) * D)
        kkp_h = kkp[:, sl]
        ss = jnp.sum(kkp_h * kkp_h, axis=-1, keepdims=True)
        rinv = 1.0 / jnp.maximum(jnp.sqrt(ss), 1e-12)
        kc = kc_p[:, sl] * rinv
        bt = bt_p[:, sl] * rinv
        btg = btg_p[:, sl] * rinv
        v_h = v[:, sl]

        xq = jnp.concatenate([kc, rt[:, sl]], axis=0).astype(BF16)
        wk = jnp.concatenate([bt, kt[:, sl]], axis=0).astype(BF16)
        gm = jnp.where(gmask, _dot_nt(xq, wk), 0.0)
        g_top = gm[:C, :]
        a_r = jnp.where(left, -gm[C:, :], gm[C:, :])

        pt = jnp.where(left, -g_top, eye_right)
        for _ in range(6):
            pt = _dot(pt[:, :C].astype(BF16), pt.astype(BF16)) + jnp.where(left, 0.0, pt)

        w1 = _dot(g_top.astype(BF16), jnp.concatenate([zeros_cd, v_h], axis=0).astype(BF16))
        s_h = s_ref[h]
        z = _dot_nt(xq, s_h.astype(BF16))
        u = _dot(pt.astype(BF16), jnp.concatenate([zeros_cd, z[:C] + w1], axis=0).astype(BF16))
        y = z[C:] + _dot(a_r.astype(BF16), jnp.concatenate([u, v_h], axis=0).astype(BF16))
        s_ref[h] = s_h * gamma_end[:, sl] + _dot_tn(
            jnp.concatenate([v_h, -u], axis=0).astype(BF16),
            jnp.concatenate([ktg[:, sl], btg], axis=0).astype(BF16))

        mean = jnp.mean(y, axis=-1, keepdims=True)
        yc = y - mean
        var = jnp.mean(yc * yc, axis=-1, keepdims=True)
        yn = yc * lax.rsqrt(var + GN_EPS) * gng_ref[:, sl] + gnb_ref[:, sl]
        bonus = jnp.sum(rkk[:, sl], axis=-1, keepdims=True) * v_h
        y_ref[:, sl] = (yn + bonus) * gsilu[:, sl]


def _rwkv_branch(p, batch, seq, mu_r, mu_k, mu_v, mu_g, mu_sm, w0, w2, a0, a2, k_k, k_a, r_k,
                 gn_gain, gn_bias):
    C = RWKV_CHUNK
    nc = seq // C
    row = lambda b, c: b * nc + c
    const = lambda shape: pl.BlockSpec(shape, lambda b, c: (0, 0))
    col = COL_R // RWKV_DIM
    return pl.pallas_call(
        _rwkv_kernel,
        grid=(batch, nc),
        in_specs=[
            pl.BlockSpec((C, RWKV_DIM), lambda b, c: (row(b, c), col)),
            pl.BlockSpec((C, RWKV_DIM), lambda b, c: (row(b, c), col + 1)),
            pl.BlockSpec((C, RWKV_DIM), lambda b, c: (row(b, c), col + 2)),
            pl.BlockSpec((C, RWKV_DIM), lambda b, c: (row(b, c), col + 3)),
            pl.BlockSpec((C, SMALL_WIDTH), lambda b, c: (row(b, c), COL_SMALL // SMALL_WIDTH)),
            const((1, RWKV_DIM)), const((1, RWKV_DIM)), const((1, RWKV_DIM)), const((1, RWKV_DIM)),
            const((1, SMALL_WIDTH)),
            const((1, RWKV_DIM)), const((LORA, RWKV_DIM)), const((1, RWKV_DIM)), const((LORA, RWKV_DIM)),
            const((1, RWKV_DIM)), const((1, RWKV_DIM)), const((1, RWKV_DIM)),
            const((1, RWKV_DIM)), const((1, RWKV_DIM)),
        ],
        out_specs=pl.BlockSpec((C, RWKV_DIM), lambda b, c: (row(b, c), 0)),
        out_shape=jax.ShapeDtypeStruct((batch * seq, RWKV_DIM), F32),
        scratch_shapes=[
            pltpu.VMEM((1, RWKV_DIM), F32), pltpu.VMEM((1, RWKV_DIM), F32),
            pltpu.VMEM((1, RWKV_DIM), F32), pltpu.VMEM((1, RWKV_DIM), F32),
            pltpu.VMEM((1, SMALL_WIDTH), F32),
            pltpu.VMEM((RWKV_HEADS, RWKV_HEAD_DIM, RWKV_HEAD_DIM), F32),
        ],
        compiler_params=pltpu.CompilerParams(
            dimension_semantics=("arbitrary", "arbitrary"), vmem_limit_bytes=VMEM_LIMIT),
        name="rwkv_branch",
    )(p, p, p, p, p, mu_r, mu_k, mu_v, mu_g, mu_sm, w0, w2, a0, a2, k_k, k_a, r_k, gn_gain, gn_bias)


def _merge_kernel(x_ref, ys_ref, yr_ref, g0_ref, g1_ref, bg_ref, ws_ref, wr_ref, wo_ref, gain_ref, o_ref):
    g_ssm = _sigmoid(g0_ref[...] + bg_ref[:, 0:D_MODEL])
    g_rwkv = _sigmoid(g1_ref[...] + bg_ref[:, D_MODEL:2 * D_MODEL])
    merged = (g_ssm * _dot(ys_ref[...].astype(BF16), ws_ref[...])
              + g_rwkv * _dot(yr_ref[...].astype(BF16), wr_ref[...]))
    out = _dot(merged.astype(BF16), wo_ref[...])
    ms = jnp.mean(out * out, axis=-1, keepdims=True)
    o_ref[...] = x_ref[...] + out * lax.rsqrt(ms + EPS) * gain_ref[...]


def _merge(x2d, y_ssm, y_rwkv, p, b_gate, w_ssm, w_rwkv, w_out, post_gain):
    t = x2d.shape[0]
    tm = MERGE_TM
    const = lambda shape: pl.BlockSpec(shape, lambda i: (0, 0))
    return pl.pallas_call(
        _merge_kernel,
        grid=(t // tm,),
        in_specs=[
            pl.BlockSpec((tm, D_MODEL), lambda i: (i, 0)),
            pl.BlockSpec((tm, SSM_INNER), lambda i: (i, 0)),
            pl.BlockSpec((tm, RWKV_DIM), lambda i: (i, 0)),
            pl.BlockSpec((tm, D_MODEL), lambda i: (i, COL_GATE // D_MODEL)),
            pl.BlockSpec((tm, D_MODEL), lambda i: (i, COL_GATE // D_MODEL + 1)),
            const((1, 2 * D_MODEL)),
            const((SSM_INNER, D_MODEL)), const((RWKV_DIM, D_MODEL)), const((D_MODEL, D_MODEL)),
            const((1, D_MODEL)),
        ],
        out_specs=pl.BlockSpec((tm, D_MODEL), lambda i: (i, 0)),
        out_shape=jax.ShapeDtypeStruct((t, D_MODEL), F32),
        compiler_params=pltpu.CompilerParams(
            dimension_semantics=("arbitrary",), vmem_limit_bytes=VMEM_LIMIT),
        name="gated_merge",
    )(x2d, y_ssm, y_rwkv, p, p, b_gate, w_ssm, w_rwkv, w_out, post_gain)


def _pad_lanes(v, width):
    return jnp.pad(v, ((0, 0), (0, width - v.shape[-1])))


def _layer(x, pre_gain, w_in, b_gate, conv_w, conv_b, dt_bias, a_log, d_skip, ssm_norm_gain,
           rwkv_mu, decay_w0, decay_w2, iclr_a0, iclr_a2, k_k, k_a, r_k, gn_gain, gn_bias,
           w_branch_ssm, w_branch_rwkv, w_out, post_gain):
    batch, seq, _ = x.shape
    x2d = x.reshape(batch * seq, D_MODEL)
    row = lambda v: v.reshape(1, -1)

    o_xbc, o_dt, o_rw = SSM_INNER, SSM_INNER + SSM_XBC, SSM_INNER + SSM_XBC + SSM_HEADS
    o_lora, o_gate = o_rw + 4 * RWKV_DIM, o_rw + 4 * RWKV_DIM + 2 * LORA
    w_proj = jnp.concatenate([
        w_in[:, o_xbc:o_dt], w_in[:, :o_xbc], w_in[:, o_rw:o_lora], w_in[:, o_gate:],
        w_in[:, o_lora:o_gate], w_in[:, o_dt:o_rw],
        jnp.zeros((D_MODEL, SMALL_WIDTH - 2 * LORA - SSM_HEADS), w_in.dtype)], axis=1).astype(BF16)

    p = _project(x2d, row(pre_gain), w_proj)

    head_of_col = jnp.arange(SSM_INNER) // SSM_HEAD_DIM
    expand = (jnp.arange(LANES)[:, None] == head_of_col[None, :]).astype(BF16)
    y_ssm = _ssd_branch(
        p, batch, seq, conv_w, row(conv_b), _pad_lanes(row(dt_bias), LANES), _pad_lanes(row(a_log), LANES),
        row(jnp.repeat(d_skip, SSM_HEAD_DIM)), row(ssm_norm_gain), expand)

    mu = row(rwkv_mu)
    mu_sm = _pad_lanes(mu[:, 4 * RWKV_DIM:], SMALL_WIDTH)
    y_rwkv = _rwkv_branch(
        p, batch, seq, mu[:, 0:RWKV_DIM], mu[:, RWKV_DIM:2 * RWKV_DIM], mu[:, 2 * RWKV_DIM:3 * RWKV_DIM],
        mu[:, 3 * RWKV_DIM:4 * RWKV_DIM], mu_sm, row(decay_w0), decay_w2.astype(BF16), row(iclr_a0),
        iclr_a2.astype(BF16), row(k_k), row(k_a), row(r_k), row(gn_gain), row(gn_bias))

    out = _merge(x2d, y_ssm, y_rwkv, p, row(b_gate), w_branch_ssm.astype(BF16),
                 w_branch_rwkv.astype(BF16), w_out.astype(BF16), row(post_gain))
    return out.reshape(batch, seq, D_MODEL)


def kernel(x, pre_gain, w_in, b_gate, conv_w, conv_b, dt_bias, a_log, d_skip, ssm_norm_gain, rwkv_mu,
           decay_w0, decay_w2, iclr_a0, iclr_a2, k_k, k_a, r_k, gn_gain, gn_bias, w_branch_ssm,
           w_branch_rwkv, w_out, post_gain):
    for layer in range(pre_gain.shape[0]):
        x = _layer(
            x, pre_gain[layer], w_in[layer], b_gate[layer], conv_w[layer], conv_b[layer], dt_bias[layer],
            a_log[layer], d_skip[layer], ssm_norm_gain[layer], rwkv_mu[layer], decay_w0[layer],
            decay_w2[layer], iclr_a0[layer], iclr_a2[layer], k_k[layer], k_a[layer], r_k[layer],
            gn_gain[layer], gn_bias[layer], w_branch_ssm[layer], w_branch_rwkv[layer], w_out[layer],
            post_gain[layer])
    return x
```

```python
import functools

import jax
import jax.numpy as jnp
from jax import lax
from jax.experimental import pallas as pl
from jax.experimental.pallas import tpu as pltpu

D_MODEL = 1024
EPS = 1e-6

SSM_INNER = 2048
SSM_HEAD_DIM = 64
SSM_HEADS = 32
SSM_STATE = 128
SSM_GROUPS = 4
SSM_GROUP_WIDTH = SSM_INNER // SSM_GROUPS
CONV_WIDTH = 4
SSM_XBC = 3072

RWKV_DIM = 1024
RWKV_HEAD_DIM = 64
RWKV_HEADS = 16
LORA = 64
GN_EPS = RWKV_HEAD_DIM * 1e-5

LANES = 128
SUBLANES = 8

COL_XBC = 0
COL_Z = 3072
COL_R = 5120
COL_GATE = 9216
COL_SMALL = 11264
SMALL_WIDTH = 256
PROJ_COLS = COL_SMALL + SMALL_WIDTH

PROJ_TM = 1024
PROJ_TN = 1280
SSD_CHUNK = 128
RWKV_CHUNK = 64
MERGE_TM = 256
VMEM_LIMIT = 48 * 1024 * 1024

BF16 = jnp.bfloat16
F32 = jnp.float32


def _dot(a, b):
    return jnp.dot(a, b, preferred_element_type=F32)


def _dot_nt(a, b):
    return lax.dot_general(a, b, (((1,), (1,)), ((), ())), preferred_element_type=F32)


def _dot_tn(a, b):
    return lax.dot_general(a, b, (((0,), (0,)), ((), ())), preferred_element_type=F32)


def _split_terms(x, n):
    terms = []
    rem = x
    for _ in range(n):
        t = rem.astype(BF16)
        terms.append(t)
        rem = rem - t.astype(F32)
    return terms


def _dot_left_exact(m_bf16, x, n=3):
    acc = None
    for t in _split_terms(x, n):
        p = _dot(m_bf16, t)
        acc = p if acc is None else acc + p
    return acc


def _dot_right_exact(x, m_bf16, n=2):
    acc = None
    for t in _split_terms(x, n):
        p = _dot(t, m_bf16)
        acc = p if acc is None else acc + p
    return acc


def _softplus(x):
    return jnp.maximum(x, 0.0) + jnp.log(1.0 + jnp.exp(-jnp.abs(x)))


def _sigmoid(x):
    return 1.0 / (1.0 + jnp.exp(-x))


def _lower_tri(n, strict):
    row = lax.broadcasted_iota(jnp.int32, (n, n), 0)
    col = lax.broadcasted_iota(jnp.int32, (n, n), 1)
    return (col < row) if strict else (col <= row)


def _lower_tri_ones(n):
    row = lax.broadcasted_iota(jnp.int32, (n, n), 0)
    col = lax.broadcasted_iota(jnp.int32, (n, n), 1)
    return jnp.clip(row - col + 1, 0, 1).astype(F32).astype(BF16)


def _proj_kernel(x_ref, gain_ref, w_ref, o_ref, h_ref):
    @pl.when(pl.program_id(1) == 0)
    def _():
        x = x_ref[...]
        ms = jnp.mean(x * x, axis=-1, keepdims=True)
        h_ref[...] = (x * lax.rsqrt(ms + EPS) * gain_ref[...]).astype(BF16)

    o_ref[...] = _dot(h_ref[...], w_ref[...])


def _project(x2d, pre_gain, w_proj):
    t = x2d.shape[0]
    return pl.pallas_call(
        _proj_kernel,
        grid=(t // PROJ_TM, PROJ_COLS // PROJ_TN),
        in_specs=[
            pl.BlockSpec((PROJ_TM, D_MODEL), lambda i, j: (i, 0)),
            pl.BlockSpec((1, D_MODEL), lambda i, j: (0, 0)),
            pl.BlockSpec((D_MODEL, PROJ_TN), lambda i, j: (0, j)),
        ],
        out_specs=pl.BlockSpec((PROJ_TM, PROJ_TN), lambda i, j: (i, j)),
        out_shape=jax.ShapeDtypeStruct((t, PROJ_COLS), F32),
        scratch_shapes=[pltpu.VMEM((PROJ_TM, D_MODEL), BF16)],
        compiler_params=pltpu.CompilerParams(
            dimension_semantics=("arbitrary", "arbitrary"), vmem_limit_bytes=VMEM_LIMIT),
        name="in_proj",
    )(x2d, pre_gain, w_proj)


def _ssd_kernel(xbc_ref, z0_ref, z1_ref, sm_ref, convw_ref, convb_ref, dtb_ref, alog_ref,
                dskip_ref, gain_ref, expand_ref, y_ref, xbuf_ref, st_ref, yd_ref):
    L = SSD_CHUNK
    HALO = SUBLANES

    @pl.when(pl.program_id(1) == 0)
    def _():
        xbuf_ref[0:HALO, :] = jnp.zeros((HALO, SSM_XBC), F32)
        st_ref[...] = jnp.zeros_like(st_ref)

    xbuf_ref[HALO:HALO + L, :] = xbc_ref[...]

    def conv_silu(c0, width):
        acc = convb_ref[:, c0:c0 + width]
        for j in range(CONV_WIDTH):
            r0 = HALO - (CONV_WIDTH - 1) + j
            acc = acc + convw_ref[j:j + 1, c0:c0 + width] * xbuf_ref[r0:r0 + L, c0:c0 + width]
        return acc * _sigmoid(acc)

    dt = _softplus(sm_ref[:, LANES:2 * LANES] + dtb_ref[...])
    a = dt * (-jnp.exp(alog_ref[...]))
    tri = _lower_tri(L, strict=False)
    acum = _dot_left_exact(_lower_tri_ones(L), a)
    acum_t = acum.T
    a_last = acum[L - 1:L, :]
    dec_out = jnp.exp(acum)
    dec_state = jnp.exp(a_last - acum)

    for g in range(SSM_GROUPS):
        c0 = g * SSM_GROUP_WIDTH
        ex = expand_ref[:, c0:c0 + SSM_GROUP_WIDTH]
        xs = conv_silu(c0, SSM_GROUP_WIDTH)
        bm = conv_silu(SSM_INNER + g * SSM_STATE, SSM_STATE).astype(BF16)
        cm = conv_silu(SSM_INNER + SSM_GROUPS * SSM_STATE + g * SSM_STATE, SSM_STATE).astype(BF16)
        dt_x = _dot_right_exact(dt, ex)
        dec_out_x = _dot_right_exact(dec_out, ex)
        dec_state_x = _dot_right_exact(dec_state, ex)
        xdt = xs * dt_x
        xdt_b = xdt.astype(BF16)

        scores = _dot_nt(cm, bm)
        state = st_ref[g]
        y_off = _dot(cm, state.astype(BF16)) * dec_out_x
        st_ref[g] = state * dec_out_x[L - 1:L, :] + _dot_tn(bm, (xdt * dec_state_x).astype(BF16))

        for j in range(SSM_GROUP_WIDTH // SSM_HEAD_DIM):
            h = g * (SSM_GROUP_WIDTH // SSM_HEAD_DIM) + j
            seg = acum[:, h:h + 1] - acum_t[h:h + 1, :]
            mix = (scores * jnp.exp(jnp.where(tri, seg, -jnp.inf))).astype(BF16)
            yd_ref[:, j * SSM_HEAD_DIM:(j + 1) * SSM_HEAD_DIM] = _dot(
                mix, xdt_b[:, j * SSM_HEAD_DIM:(j + 1) * SSM_HEAD_DIM])

        y = yd_ref[...] + y_off + dskip_ref[:, c0:c0 + SSM_GROUP_WIDTH] * xs
        z_ref = z0_ref if g < 2 else z1_ref
        zc = (g % 2) * SSM_GROUP_WIDTH
        z = z_ref[:, zc:zc + SSM_GROUP_WIDTH]
        y = y * (z * _sigmoid(z))
        ms = jnp.mean(y * y, axis=-1, keepdims=True)
        y_ref[:, c0:c0 + SSM_GROUP_WIDTH] = y * lax.rsqrt(ms + EPS) * gain_ref[:, c0:c0 + SSM_GROUP_WIDTH]

    xbuf_ref[0:HALO, :] = xbuf_ref[L:L + HALO, :]


def _ssd_branch(p, batch, seq, conv_w, conv_b, dt_bias, a_log, dskip_x, norm_gain, expand):
    L = SSD_CHUNK
    nc = seq // L
    row = lambda b, c: b * nc + c
    const = lambda shape: pl.BlockSpec(shape, lambda b, c: (0, 0))
    return pl.pallas_call(
        _ssd_kernel,
        grid=(batch, nc),
        in_specs=[
            pl.BlockSpec((L, SSM_XBC), lambda b, c: (row(b, c), COL_XBC // SSM_XBC)),
            pl.BlockSpec((L, 1024), lambda b, c: (row(b, c), COL_Z // 1024)),
            pl.BlockSpec((L, 1024), lambda b, c: (row(b, c), COL_Z // 1024 + 1)),
            pl.BlockSpec((L, SMALL_WIDTH), lambda b, c: (row(b, c), COL_SMALL // SMALL_WIDTH)),
            const((CONV_WIDTH, SSM_XBC)),
            const((1, SSM_XBC)),
            const((1, LANES)),
            const((1, LANES)),
            const((1, SSM_INNER)),
            const((1, SSM_INNER)),
            const((LANES, SSM_INNER)),
        ],
        out_specs=pl.BlockSpec((L, SSM_INNER), lambda b, c: (row(b, c), 0)),
        out_shape=jax.ShapeDtypeStruct((batch * seq, SSM_INNER), F32),
        scratch_shapes=[
            pltpu.VMEM((L + SUBLANES, SSM_XBC), F32),
            pltpu.VMEM((SSM_GROUPS, SSM_STATE, SSM_GROUP_WIDTH), F32),
            pltpu.VMEM((L, SSM_GROUP_WIDTH), F32),
        ],
        compiler_params=pltpu.CompilerParams(
            dimension_semantics=("arbitrary", "arbitrary"), vmem_limit_bytes=VMEM_LIMIT),
        name="ssd_branch",
    )(p, p, p, p, conv_w, conv_b, dt_bias, a_log, dskip_x, norm_gain, expand)


def _rwkv_kernel(r_ref, k_ref, v_ref, g_ref, sm_ref, mur_ref, muk_ref, muv_ref, mug_ref, musm_ref,
                 w0_ref, w2_ref, a0_ref, a2_ref, kks_ref, kas_ref, rk_ref, gng_ref, gnb_ref,
                 y_ref, cr_ref, ck_ref, cv_ref, cg_ref, csm_ref, s_ref):
    C = RWKV_CHUNK
    D = RWKV_HEAD_DIM

    @pl.when(pl.program_id(1) == 0)
    def _():
        for c_ref in (cr_ref, ck_ref, cv_ref, cg_ref, csm_ref):
            c_ref[...] = jnp.zeros_like(c_ref)
        s_ref[...] = jnp.zeros_like(s_ref)

    def shifted(x_ref, carry_ref, mu_ref):
        x = x_ref[...]
        first = lax.broadcasted_iota(jnp.int32, x.shape, 0) == 0
        prev = jnp.where(first, carry_ref[...], pltpu.roll(x, 1, 0))
        carry_ref[...] = x[C - 1:C, :]
        return x + mu_ref[...] * (prev - x)

    r = shifted(r_ref, cr_ref, mur_ref)
    k = shifted(k_ref, ck_ref, muk_ref)
    v = shifted(v_ref, cv_ref, muv_ref)
    gate = shifted(g_ref, cg_ref, mug_ref)
    sm = shifted(sm_ref, csm_ref, musm_ref)
    wd = sm[:, 0:LORA]
    ad = sm[:, LORA:2 * LORA]

    lw = _dot(jnp.tanh(wd).astype(BF16), w2_ref[...])
    logw = -jnp.exp(-_softplus(-(w0_ref[...] + lw)) - 0.5)
    a = _sigmoid(a0_ref[...] + _dot(ad.astype(BF16), a2_ref[...]))
    kkp = k * kks_ref[...]
    k2 = k * (1.0 + (a - 1.0) * kas_ref[...])

    cum = _dot_left_exact(_lower_tri_ones(C), logw)
    cum_last = cum[C - 1:C, :]
    e_pos = jnp.exp(cum)
    e_neg = jnp.exp(-cum)
    e_end = jnp.exp(cum_last - cum)
    kc_p = kkp * jnp.exp(cum - logw)
    rt = r * e_pos
    kt = k2 * e_neg
    bt_p = kkp * a * e_neg
    ktg = k2 * e_end
    btg_p = kkp * a * e_end
    gamma_end = jnp.exp(cum_last)
    rkk = r * k2 * rk_ref[...]
    gsilu = gate * _sigmoid(gate)

    P = 2 * D
    lane_a = lax.broadcasted_iota(jnp.int32, (C, P), 1) < D
    lane_a2 = lax.broadcasted_iota(jnp.int32, (2 * C, P), 1) < D
    row2 = lax.broadcasted_iota(jnp.int32, (2 * C, 2 * C), 0)
    col2 = lax.broadcasted_iota(jnp.int32, (2 * C, 2 * C), 1)
    colm = jnp.where(col2 >= C, col2 - C, col2)
    gmask = colm < jnp.where(row2 >= C, row2 - C + 1, row2)
    diag_blk = jnp.where(row2 >= C, 1, 0) == jnp.where(col2 >= C, 1, 0)
    anti_eye = jnp.where(col2 == jnp.where(row2 >= C, row2 - C, row2 + C), 1.0, 0.0)

    def seg_sum(x):
        sa = jnp.sum(jnp.where(lane_a, x, 0.0), axis=-1, keepdims=True)
        sb = jnp.sum(jnp.where(lane_a, 0.0, x), axis=-1, keepdims=True)
        return jnp.where(lane_a, sa, sb)

    def cross(lo, hi):
        return jnp.concatenate([lo, hi], axis=0).astype(BF16)

    pairs = range(RWKV_HEADS // 2)
    sls = [slice(j * P, (j + 1) * P) for j in pairs]
    xb, g_top, g_bot, btg, v_p = [], [], [], [], []
    for sl in sls:
        kkp_p = kkp[:, sl]
        rinv = 1.0 / jnp.maximum(jnp.sqrt(seg_sum(kkp_p * kkp_p)), 1e-12)
        bt = bt_p[:, sl] * rinv
        xp = jnp.concatenate([kc_p[:, sl] * rinv, rt[:, sl]], axis=0)
        w_a = jnp.concatenate([bt, kt[:, sl]], axis=0).astype(BF16)
        w_b = jnp.concatenate([kt[:, sl], bt], axis=0).astype(BF16)
        g_a = jnp.where(gmask, _dot_nt(jnp.where(lane_a2, xp, 0.0).astype(BF16), w_a), 0.0)
        g_b = jnp.where(gmask, _dot_nt(jnp.where(lane_a2, 0.0, xp).astype(BF16), w_b), 0.0)
        xb.append(xp.astype(BF16))
        g_top.append(jnp.concatenate([g_a[:C], g_b[:C]], axis=0))
        g_bot.append(jnp.concatenate([g_a[C:], g_b[C:]], axis=0))
        btg.append(btg_p[:, sl] * rinv)
        v_p.append(v[:, sl])

    z = [_dot_nt(xb[j], s_ref[j].astype(BF16)) for j in pairs]
    w1 = []
    for j in pairs:
        rw = _dot(g_top[j].astype(BF16),
                  cross(jnp.where(lane_a, 0.0, v_p[j]), jnp.where(lane_a, v_p[j], 0.0)))
        w1.append(jnp.where(lane_a, rw[:C], rw[C:]))

    pt = [jnp.where(diag_blk, -g_top[j], anti_eye) for j in pairs]
    for _ in range(6):
        pt = [_dot(jnp.where(diag_blk, pt[j], 0.0).astype(BF16), pt[j].astype(BF16))
              + jnp.where(diag_blk, 0.0, pt[j]) for j in pairs]

    u = []
    for j in pairs:
        rhs = z[j][:C] + w1[j]
        ru = _dot(pt[j].astype(BF16), cross(jnp.where(lane_a, 0.0, rhs), jnp.where(lane_a, rhs, 0.0)))
        u.append(ru[:C] + ru[C:])
    for j in pairs:
        sl = sls[j]
        a_r = jnp.where(diag_blk, -g_bot[j], g_bot[j])
        ry = _dot(a_r.astype(BF16),
                  cross(jnp.where(lane_a, u[j], v_p[j]), jnp.where(lane_a, v_p[j], u[j])))
        y = z[j][C:] + jnp.where(lane_a, ry[:C], ry[C:])
        upd = _dot_tn(jnp.concatenate([v_p[j], -u[j]], axis=0).astype(BF16),
                      jnp.concatenate([ktg[:, sl], btg[j]], axis=0).astype(BF16))
        s_ref[j] = s_ref[j] * gamma_end[:, sl] + jnp.where(diag_blk, upd, 0.0)

        yc = y - seg_sum(y) * (1.0 / D)
        var = seg_sum(yc * yc) * (1.0 / D)
        yn = yc * lax.rsqrt(var + GN_EPS) * gng_ref[:, sl] + gnb_ref[:, sl]
        bonus = seg_sum(rkk[:, sl]) * v_p[j]
        y_ref[:, sl] = (yn + bonus) * gsilu[:, sl]


def _rwkv_branch(p, batch, seq, mu_r, mu_k, mu_v, mu_g, mu_sm, w0, w2, a0, a2, k_k, k_a, r_k,
                 gn_gain, gn_bias):
    C = RWKV_CHUNK
    nc = seq // C
    row = lambda b, c: b * nc + c
    const = lambda shape: pl.BlockSpec(shape, lambda b, c: (0, 0))
    col = COL_R // RWKV_DIM
    return pl.pallas_call(
        _rwkv_kernel,
        grid=(batch, nc),
        in_specs=[
            pl.BlockSpec((C, RWKV_DIM), lambda b, c: (row(b, c), col)),
            pl.BlockSpec((C, RWKV_DIM), lambda b, c: (row(b, c), col + 1)),
            pl.BlockSpec((C, RWKV_DIM), lambda b, c: (row(b, c), col + 2)),
            pl.BlockSpec((C, RWKV_DIM), lambda b, c: (row(b, c), col + 3)),
            pl.BlockSpec((C, SMALL_WIDTH), lambda b, c: (row(b, c), COL_SMALL // SMALL_WIDTH)),
            const((1, RWKV_DIM)), const((1, RWKV_DIM)), const((1, RWKV_DIM)), const((1, RWKV_DIM)),
            const((1, SMALL_WIDTH)),
            const((1, RWKV_DIM)), const((LORA, RWKV_DIM)), const((1, RWKV_DIM)), const((LORA, RWKV_DIM)),
            const((1, RWKV_DIM)), const((1, RWKV_DIM)), const((1, RWKV_DIM)),
            const((1, RWKV_DIM)), const((1, RWKV_DIM)),
        ],
        out_specs=pl.BlockSpec((C, RWKV_DIM), lambda b, c: (row(b, c), 0)),
        out_shape=jax.ShapeDtypeStruct((batch * seq, RWKV_DIM), F32),
        scratch_shapes=[
            pltpu.VMEM((1, RWKV_DIM), F32), pltpu.VMEM((1, RWKV_DIM), F32),
            pltpu.VMEM((1, RWKV_DIM), F32), pltpu.VMEM((1, RWKV_DIM), F32),
            pltpu.VMEM((1, SMALL_WIDTH), F32),
            pltpu.VMEM((RWKV_HEADS // 2, 2 * RWKV_HEAD_DIM, 2 * RWKV_HEAD_DIM), F32),
        ],
        compiler_params=pltpu.CompilerParams(
            dimension_semantics=("arbitrary", "arbitrary"), vmem_limit_bytes=VMEM_LIMIT),
        name="rwkv_branch",
    )(p, p, p, p, p, mu_r, mu_k, mu_v, mu_g, mu_sm, w0, w2, a0, a2, k_k, k_a, r_k, gn_gain, gn_bias)


def _merge_kernel(x_ref, ys_ref, yr_ref, g0_ref, g1_ref, bg_ref, ws_ref, wr_ref, wo_ref, gain_ref, o_ref):
    g_ssm = _sigmoid(g0_ref[...] + bg_ref[:, 0:D_MODEL])
    g_rwkv = _sigmoid(g1_ref[...] + bg_ref[:, D_MODEL:2 * D_MODEL])
    merged = (g_ssm * _dot(ys_ref[...].astype(BF16), ws_ref[...])
              + g_rwkv * _dot(yr_ref[...].astype(BF16), wr_ref[...]))
    out = _dot(merged.astype(BF16), wo_ref[...])
    ms = jnp.mean(out * out, axis=-1, keepdims=True)
    o_ref[...] = x_ref[...] + out * lax.rsqrt(ms + EPS) * gain_ref[...]


def _merge(x2d, y_ssm, y_rwkv, p, b_gate, w_ssm, w_rwkv, w_out, post_gain):
    t = x2d.shape[0]
    tm = MERGE_TM
    const = lambda shape: pl.BlockSpec(shape, lambda i: (0, 0))
    return pl.pallas_call(
        _merge_kernel,
        grid=(t // tm,),
        in_specs=[
            pl.BlockSpec((tm, D_MODEL), lambda i: (i, 0)),
            pl.BlockSpec((tm, SSM_INNER), lambda i: (i, 0)),
            pl.BlockSpec((tm, RWKV_DIM), lambda i: (i, 0)),
            pl.BlockSpec((tm, D_MODEL), lambda i: (i, COL_GATE // D_MODEL)),
            pl.BlockSpec((tm, D_MODEL), lambda i: (i, COL_GATE // D_MODEL + 1)),
            const((1, 2 * D_MODEL)),
            const((SSM_INNER, D_MODEL)), const((RWKV_DIM, D_MODEL)), const((D_MODEL, D_MODEL)),
            const((1, D_MODEL)),
        ],
        out_specs=pl.BlockSpec((tm, D_MODEL), lambda i: (i, 0)),
        out_shape=jax.ShapeDtypeStruct((t, D_MODEL), F32),
        compiler_params=pltpu.CompilerParams(
            dimension_semantics=("arbitrary",), vmem_limit_bytes=VMEM_LIMIT),
        name="gated_merge",
    )(x2d, y_ssm, y_rwkv, p, p, b_gate, w_ssm, w_rwkv, w_out, post_gain)


def _pad_lanes(v, width):
    return jnp.pad(v, ((0, 0), (0, width - v.shape[-1])))


def _layer(x, pre_gain, w_in, b_gate, conv_w, conv_b, dt_bias, a_log, d_skip, ssm_norm_gain,
           rwkv_mu, decay_w0, decay_w2, iclr_a0, iclr_a2, k_k, k_a, r_k, gn_gain, gn_bias,
           w_branch_ssm, w_branch_rwkv, w_out, post_gain):
    batch, seq, _ = x.shape
    x2d = x.reshape(batch * seq, D_MODEL)
    row = lambda v: v.reshape(1, -1)

    o_xbc, o_dt, o_rw = SSM_INNER, SSM_INNER + SSM_XBC, SSM_INNER + SSM_XBC + SSM_HEADS
    o_lora, o_gate = o_rw + 4 * RWKV_DIM, o_rw + 4 * RWKV_DIM + 2 * LORA
    w_proj = jnp.concatenate([
        w_in[:, o_xbc:o_dt], w_in[:, :o_xbc], w_in[:, o_rw:o_lora], w_in[:, o_gate:],
        w_in[:, o_lora:o_gate], w_in[:, o_dt:o_rw],
        jnp.zeros((D_MODEL, SMALL_WIDTH - 2 * LORA - SSM_HEADS), w_in.dtype)], axis=1).astype(BF16)

    p = _project(x2d, row(pre_gain), w_proj)

    head_of_col = jnp.arange(SSM_INNER) // SSM_HEAD_DIM
    expand = (jnp.arange(LANES)[:, None] == head_of_col[None, :]).astype(BF16)
    y_ssm = _ssd_branch(
        p, batch, seq, conv_w, row(conv_b), _pad_lanes(row(dt_bias), LANES), _pad_lanes(row(a_log), LANES),
        row(jnp.repeat(d_skip, SSM_HEAD_DIM)), row(ssm_norm_gain), expand)

    mu = row(rwkv_mu)
    mu_sm = _pad_lanes(mu[:, 4 * RWKV_DIM:], SMALL_WIDTH)
    y_rwkv = _rwkv_branch(
        p, batch, seq, mu[:, 0:RWKV_DIM], mu[:, RWKV_DIM:2 * RWKV_DIM], mu[:, 2 * RWKV_DIM:3 * RWKV_DIM],
        mu[:, 3 * RWKV_DIM:4 * RWKV_DIM], mu_sm, row(decay_w0), decay_w2.astype(BF16), row(iclr_a0),
        iclr_a2.astype(BF16), row(k_k), row(k_a), row(r_k), row(gn_gain), row(gn_bias))

    out = _merge(x2d, y_ssm, y_rwkv, p, row(b_gate), w_branch_ssm.astype(BF16),
                 w_branch_rwkv.astype(BF16), w_out.astype(BF16), row(post_gain))
    return out.reshape(batch, seq, D_MODEL)


def kernel(x, pre_gain, w_in, b_gate, conv_w, conv_b, dt_bias, a_log, d_skip, ssm_norm_gain, rwkv_mu,
           decay_w0, decay_w2, iclr_a0, iclr_a2, k_k, k_a, r_k, gn_gain, gn_bias, w_branch_ssm,
           w_branch_rwkv, w_out, post_gain):
    for layer in range(pre_gain.shape[0]):
        x = _layer(
            x, pre_gain[layer], w_in[layer], b_gate[layer], conv_w[layer], conv_b[layer], dt_bias[layer],
            a_log[layer], d_skip[layer], ssm_norm_gain[layer], rwkv_mu[layer], decay_w0[layer],
            decay_w2[layer], iclr_a0[layer], iclr_a2[layer], k_k[layer], k_a[layer], r_k[layer],
            gn_gain[layer], gn_bias[layer], w_branch_ssm[layer], w_branch_rwkv[layer], w_out[layer],
            post_gain[layer])
    return x
```

```python
import functools

import jax
import jax.numpy as jnp
from jax import lax
from jax.experimental import pallas as pl
from jax.experimental.pallas import tpu as pltpu

D_MODEL = 1024
EPS = 1e-6

SSM_INNER = 2048
SSM_HEAD_DIM = 64
SSM_HEADS = 32
SSM_STATE = 128
SSM_GROUPS = 4
SSM_GROUP_WIDTH = SSM_INNER // SSM_GROUPS
CONV_WIDTH = 4
SSM_XBC = 3072

RWKV_DIM = 1024
RWKV_HEAD_DIM = 64
RWKV_HEADS = 16
LORA = 64
GN_EPS = RWKV_HEAD_DIM * 1e-5
DECAY_SCALE = 0.6065306597126334

LANES = 128
SUBLANES = 8

SMALL_WIDTH = 256
RW_PAD = 256
RW_COLS = 4 * RWKV_DIM + SMALL_WIDTH + RW_PAD
ZG_COLS = SSM_INNER + 2 * D_MODEL

PROJ_TM = 1024
PROJ_SUBTILE = 256
PROJ_TN = {SSM_XBC: 1024, RW_COLS: RW_COLS // 3, ZG_COLS: 1024}
SSD_CHUNK = 128
RWKV_CHUNK = 64
RWKV_BATCH_ROWS = 2
MERGE_TM = 256
VMEM_LIMIT = 48 * 1024 * 1024

BF16 = jnp.bfloat16
F32 = jnp.float32


def _dot(a, b):
    return jnp.dot(a, b, preferred_element_type=F32)


def _dot_nt(a, b):
    return lax.dot_general(a, b, (((1,), (1,)), ((), ())), preferred_element_type=F32)


def _dot_tn(a, b):
    return lax.dot_general(a, b, (((0,), (0,)), ((), ())), preferred_element_type=F32)


def _split_terms(x, n):
    terms = []
    rem = x
    for _ in range(n):
        t = rem.astype(BF16)
        terms.append(t)
        rem = rem - t.astype(F32)
    return terms


def _dot_left_exact(m_bf16, x, n=3):
    acc = None
    for t in _split_terms(x, n):
        p = _dot(m_bf16, t)
        acc = p if acc is None else acc + p
    return acc


def _dot_terms(terms, m_bf16):
    acc = None
    for t in terms:
        p = _dot(t, m_bf16)
        acc = p if acc is None else acc + p
    return acc


def _softplus(x):
    return jnp.maximum(x, 0.0) + jnp.log(1.0 + jnp.exp(-jnp.abs(x)))


def _sigmoid(x):
    return 1.0 / (1.0 + jnp.exp(-x))


def _lower_tri(n, strict):
    row = lax.broadcasted_iota(jnp.int32, (n, n), 0)
    col = lax.broadcasted_iota(jnp.int32, (n, n), 1)
    return (col < row) if strict else (col <= row)


def _lower_tri_ones(n):
    row = lax.broadcasted_iota(jnp.int32, (n, n), 0)
    col = lax.broadcasted_iota(jnp.int32, (n, n), 1)
    return jnp.clip(row - col + 1, 0, 1).astype(F32).astype(BF16)


def _normed_input(x_ref, gain_ref, h_ref):
    @pl.when(pl.program_id(1) == 0)
    def _():
        x = x_ref[...]
        ms = jnp.mean(x * x, axis=-1, keepdims=True)
        h_ref[...] = (x * lax.rsqrt(ms + EPS) * gain_ref[...]).astype(BF16)


def _delayed(o, halo, k):
    head = jnp.concatenate([halo, o[:SUBLANES]], axis=0)[SUBLANES - k:2 * SUBLANES - k]
    if o.shape[0] == SUBLANES:
        return head
    return jnp.concatenate([head, pltpu.roll(o, k, 0)[SUBLANES:]], axis=0)


def _zero_carry_at_sequence_start(carry_ref, rows_per_seq):
    i, j = pl.program_id(0), pl.program_id(1)

    @pl.when(i % (rows_per_seq // PROJ_TM) == 0)
    def _():
        carry_ref[j] = jnp.zeros(carry_ref.shape[1:], F32)


def _proj_plain_kernel(x_ref, gain_ref, w_ref, o_ref, h_ref):
    _normed_input(x_ref, gain_ref, h_ref)
    o_ref[...] = _dot(h_ref[...], w_ref[...])


def _proj_conv_kernel(rows_per_seq, x_ref, gain_ref, w_ref, cw_ref, cb_ref, o_ref, h_ref, carry_ref):
    _normed_input(x_ref, gain_ref, h_ref)
    _zero_carry_at_sequence_start(carry_ref, rows_per_seq)
    j = pl.program_id(1)
    for c0 in range(0, o_ref.shape[1], PROJ_SUBTILE):
        cs = slice(c0, c0 + PROJ_SUBTILE)
        o = _dot(h_ref[...], w_ref[:, cs])
        halo = carry_ref[j, :, cs]
        carry_ref[j, :, cs] = o[PROJ_TM - SUBLANES:, :]
        e = _delayed(o, halo, 1)
        e_halo = _delayed(halo, jnp.zeros_like(halo), 1)
        inner = cw_ref[1:2, cs] * o + cw_ref[0:1, cs] * e
        inner_halo = cw_ref[1:2, cs] * halo + cw_ref[0:1, cs] * e_halo
        acc = cb_ref[:, cs] + cw_ref[3:4, cs] * o + cw_ref[2:3, cs] * e + _delayed(inner, inner_halo, 2)
        o_ref[:, cs] = acc * _sigmoid(acc)


def _proj_shift_kernel(rows_per_seq, x_ref, gain_ref, w_ref, mu_ref, o_ref, h_ref, carry_ref):
    _normed_input(x_ref, gain_ref, h_ref)
    _zero_carry_at_sequence_start(carry_ref, rows_per_seq)
    j = pl.program_id(1)
    for c0 in range(0, o_ref.shape[1], PROJ_SUBTILE):
        cs = slice(c0, c0 + PROJ_SUBTILE)
        o = _dot(h_ref[...], w_ref[:, cs])
        halo = carry_ref[j, :, cs]
        carry_ref[j, :, cs] = o[PROJ_TM - SUBLANES:, :]
        o_ref[:, cs] = o + mu_ref[:, cs] * (_delayed(o, halo, 1) - o)


def _project(kernel_fn, x2d, pre_gain, w, col_params, carry):
    t, n = x2d.shape[0], w.shape[1]
    tn = PROJ_TN[n]
    scratch = [pltpu.VMEM((PROJ_TM, D_MODEL), BF16)]
    if carry:
        scratch.append(pltpu.VMEM((n // tn, SUBLANES, tn), F32))
    return pl.pallas_call(
        kernel_fn,
        grid=(t // PROJ_TM, n // tn),
        in_specs=[
            pl.BlockSpec((PROJ_TM, D_MODEL), lambda i, j: (i, 0)),
            pl.BlockSpec((1, D_MODEL), lambda i, j: (0, 0)),
            pl.BlockSpec((D_MODEL, tn), lambda i, j: (0, j)),
        ] + [pl.BlockSpec((c.shape[0], tn), lambda i, j: (0, j)) for c in col_params],
        out_specs=pl.BlockSpec((PROJ_TM, tn), lambda i, j: (i, j)),
        out_shape=jax.ShapeDtypeStruct((t, n), F32),
        scratch_shapes=scratch,
        compiler_params=pltpu.CompilerParams(
            dimension_semantics=("arbitrary", "arbitrary"), vmem_limit_bytes=VMEM_LIMIT),
        name="in_proj_%d" % n,
    )(x2d, pre_gain, w, *col_params)


def _ssd_kernel(u_ref, z_ref, sm_ref, dtb_ref, alog_ref, dskip_ref, gain_ref, expand_ref,
                y_ref, st_ref, yd_ref):
    L = SSD_CHUNK

    @pl.when(pl.program_id(1) == 0)
    def _():
        st_ref[...] = jnp.zeros_like(st_ref)

    dt = _softplus(sm_ref[:, LANES:2 * LANES] + dtb_ref[...])
    a = dt * (-jnp.exp(alog_ref[...]))
    tri = _lower_tri(L, strict=False)
    acum = _dot_left_exact(_lower_tri_ones(L), a)
    acum_t = acum.T
    a_last = acum[L - 1:L, :]
    dt_terms = _split_terms(dt, 2)
    dec_out_terms = _split_terms(jnp.exp(acum), 2)
    dec_state_terms = _split_terms(jnp.exp(a_last - acum), 2)

    for g in range(SSM_GROUPS):
        c0 = g * SSM_GROUP_WIDTH
        ex = expand_ref[:, c0:c0 + SSM_GROUP_WIDTH]
        xs = u_ref[:, c0:c0 + SSM_GROUP_WIDTH]
        b0 = SSM_INNER + g * SSM_STATE
        bm = u_ref[:, b0:b0 + SSM_STATE].astype(BF16)
        cm = u_ref[:, b0 + SSM_GROUPS * SSM_STATE:b0 + (SSM_GROUPS + 1) * SSM_STATE].astype(BF16)
        dt_x = _dot_terms(dt_terms, ex)
        dec_out_x = _dot_terms(dec_out_terms, ex)
        dec_state_x = _dot_terms(dec_state_terms, ex)
        xdt = xs * dt_x
        xdt_b = xdt.astype(BF16)

        scores = _dot_nt(cm, bm)
        state = st_ref[g]
        y_off = _dot(cm, state.astype(BF16)) * dec_out_x
        st_ref[g] = state * dec_out_x[L - 1:L, :] + _dot_tn(bm, (xdt * dec_state_x).astype(BF16))

        for j in range(SSM_GROUP_WIDTH // SSM_HEAD_DIM):
            h = g * (SSM_GROUP_WIDTH // SSM_HEAD_DIM) + j
            seg = acum[:, h:h + 1] - acum_t[h:h + 1, :]
            mix = (scores * jnp.exp(jnp.where(tri, seg, -jnp.inf))).astype(BF16)
            yd_ref[:, j * SSM_HEAD_DIM:(j + 1) * SSM_HEAD_DIM] = _dot(
                mix, xdt_b[:, j * SSM_HEAD_DIM:(j + 1) * SSM_HEAD_DIM])

        y = yd_ref[...] + y_off + dskip_ref[:, c0:c0 + SSM_GROUP_WIDTH] * xs
        z = z_ref[:, c0:c0 + SSM_GROUP_WIDTH]
        y = y * (z * _sigmoid(z))
        ms = jnp.mean(y * y, axis=-1, keepdims=True)
        y_ref[:, c0:c0 + SSM_GROUP_WIDTH] = y * lax.rsqrt(ms + EPS) * gain_ref[:, c0:c0 + SSM_GROUP_WIDTH]


def _ssd_branch(u, zg, rw, batch, seq, dt_bias, a_log, dskip_x, norm_gain, expand):
    L = SSD_CHUNK
    nc = seq // L
    row = lambda b, c: b * nc + c
    const = lambda shape: pl.BlockSpec(shape, lambda b, c: (0, 0))
    return pl.pallas_call(
        _ssd_kernel,
        grid=(batch, nc),
        in_specs=[
            pl.BlockSpec((L, SSM_XBC), lambda b, c: (row(b, c), 0)),
            pl.BlockSpec((L, SSM_INNER), lambda b, c: (row(b, c), 0)),
            pl.BlockSpec((L, SMALL_WIDTH), lambda b, c: (row(b, c), 4 * RWKV_DIM // SMALL_WIDTH)),
            const((1, LANES)),
            const((1, LANES)),
            const((1, SSM_INNER)),
            const((1, SSM_INNER)),
            const((LANES, SSM_INNER)),
        ],
        out_specs=pl.BlockSpec((L, SSM_INNER), lambda b, c: (row(b, c), 0)),
        out_shape=jax.ShapeDtypeStruct((batch * seq, SSM_INNER), F32),
        scratch_shapes=[
            pltpu.VMEM((SSM_GROUPS, SSM_STATE, SSM_GROUP_WIDTH), F32),
            pltpu.VMEM((L, SSM_GROUP_WIDTH), F32),
        ],
        compiler_params=pltpu.CompilerParams(
            dimension_semantics=("arbitrary", "arbitrary"), vmem_limit_bytes=VMEM_LIMIT),
        name="ssd_branch",
    )(u, zg, rw, dt_bias, a_log, dskip_x, norm_gain, expand)


def _rwkv_kernel(r_ref, k_ref, v_ref, g_ref, sm_ref, w0_ref, w2_ref, a0_ref, a2_ref, kks_ref, kas_ref,
                 rk_ref, gng_ref, gnb_ref, y_ref, s_ref):
    C = RWKV_CHUNK
    D = RWKV_HEAD_DIM
    NB = RWKV_BATCH_ROWS
    R = NB * C

    @pl.when(pl.program_id(1) == 0)
    def _():
        s_ref[...] = jnp.zeros_like(s_ref)

    rows = lambda x_ref: x_ref[...].reshape(R, x_ref.shape[-1])
    r, k, v, gate, sm = rows(r_ref), rows(k_ref), rows(v_ref), rows(g_ref), rows(sm_ref)
    wd = sm[:, 0:LORA]
    ad = sm[:, LORA:2 * LORA]

    lw = _dot(jnp.tanh(wd).astype(BF16), w2_ref[...])
    logw = (-DECAY_SCALE) * _sigmoid(w0_ref[...] + lw)
    neg_a = 1.0 / (-1.0 - jnp.exp(-(a0_ref[...] + _dot(ad.astype(BF16), a2_ref[...]))))
    kkp = k * kks_ref[...]
    k2 = k * (1.0 - (neg_a + 1.0) * kas_ref[...])

    rr = lax.broadcasted_iota(jnp.int32, (R, R), 0)
    cc = lax.broadcasted_iota(jnp.int32, (R, R), 1)
    same_seq = 1 - jnp.clip(jnp.abs((rr // C) - (cc // C)), 0, 1)
    tri = (jnp.clip(rr - cc + 1, 0, 1) * same_seq).astype(F32).astype(BF16)
    cum = _dot_left_exact(tri, logw)
    cum_last = [cum[(b + 1) * C - 1:(b + 1) * C, :] for b in range(NB)]
    e_pos = jnp.exp(cum)
    e_neg = 1.0 / e_pos
    kc_p = kkp * jnp.exp(cum - logw)
    rt = r * e_pos
    kt = k2 * e_neg
    nbt_p = kkp * neg_a * e_neg
    gamma_end = [jnp.exp(cum_last[b]) for b in range(NB)]
    rkk = r * k2 * rk_ref[...]
    gsilu = gate * _sigmoid(gate)

    P = 2 * D
    lane_a = lax.broadcasted_iota(jnp.int32, (C, P), 1) < D
    lane_a2 = lax.broadcasted_iota(jnp.int32, (2 * C, P), 1) < D
    row2 = lax.broadcasted_iota(jnp.int32, (2 * C, 2 * C), 0)
    col2 = lax.broadcasted_iota(jnp.int32, (2 * C, 2 * C), 1)
    colm = jnp.where(col2 >= C, col2 - C, col2)
    gmask = colm < jnp.where(row2 >= C, row2 - C + 1, row2)
    diag_blk = jnp.where(row2 >= C, 1, 0) == jnp.where(col2 >= C, 1, 0)
    anti_eye = jnp.where(col2 == jnp.where(row2 >= C, row2 - C, row2 + C), 1.0, 0.0)

    def seg_sum(x):
        sa = jnp.sum(jnp.where(lane_a, x, 0.0), axis=-1, keepdims=True)
        sb = jnp.sum(jnp.where(lane_a, 0.0, x), axis=-1, keepdims=True)
        return jnp.where(lane_a, sa, sb)

    def cross(lo, hi):
        return jnp.concatenate([lo, hi], axis=0).astype(BF16)

    units = [(b, j) for b in range(NB) for j in range(RWKV_HEADS // 2)]
    idx = [(slice(b * C, (b + 1) * C), slice(j * P, (j + 1) * P)) for b, j in units]
    n_units = range(len(units))
    xb, wa, g_top, g_bot, v_p = [], [], [], [], []
    for rs, sl in idx:
        kkp_p = kkp[rs, sl]
        rinv = 1.0 / jnp.maximum(jnp.sqrt(seg_sum(kkp_p * kkp_p)), 1e-12)
        nbt = nbt_p[rs, sl] * rinv
        xp = jnp.concatenate([kc_p[rs, sl] * rinv, rt[rs, sl]], axis=0)
        w_a = jnp.concatenate([nbt, kt[rs, sl]], axis=0).astype(BF16)
        w_b = jnp.concatenate([kt[rs, sl], nbt], axis=0).astype(BF16)
        g_a = jnp.where(gmask, _dot_nt(jnp.where(lane_a2, xp, 0.0).astype(BF16), w_a), 0.0)
        g_b = jnp.where(gmask, _dot_nt(jnp.where(lane_a2, 0.0, xp).astype(BF16), w_b), 0.0)
        xb.append(xp.astype(BF16))
        wa.append(w_a)
        g_top.append(jnp.concatenate([g_a[:C], g_b[:C]], axis=0))
        g_bot.append(jnp.concatenate([g_a[C:], g_b[C:]], axis=0))
        v_p.append(v[rs, sl])

    z = [_dot_nt(xb[i], s_ref[units[i][0], units[i][1]].astype(BF16)) for i in n_units]
    w1 = []
    for i in n_units:
        rw = _dot(g_top[i].astype(BF16),
                  cross(jnp.where(lane_a, 0.0, v_p[i]), jnp.where(lane_a, v_p[i], 0.0)))
        w1.append(jnp.where(lane_a, rw[:C], rw[C:]))

    pt = [jnp.where(diag_blk, g_top[i], anti_eye) for i in n_units]
    for _ in range(6):
        pt = [_dot(jnp.where(diag_blk, pt[i], 0.0).astype(BF16), pt[i].astype(BF16))
              + jnp.where(diag_blk, 0.0, pt[i]) for i in n_units]

    u = []
    for i in n_units:
        rhs = z[i][:C] + w1[i]
        ru = _dot(pt[i].astype(BF16), cross(jnp.where(lane_a, 0.0, rhs), jnp.where(lane_a, rhs, 0.0)))
        u.append(ru[:C] + ru[C:])
    for i in n_units:
        (b, j), (rs, sl) = units[i], idx[i]
        ry = _dot(g_bot[i].astype(BF16),
                  cross(jnp.where(lane_a, u[i], v_p[i]), jnp.where(lane_a, v_p[i], u[i])))
        y = z[i][C:] + jnp.where(lane_a, ry[:C], ry[C:])
        upd = _dot_tn(jnp.concatenate([u[i], v_p[i]], axis=0).astype(BF16), wa[i])
        s_ref[b, j] = (s_ref[b, j] + jnp.where(diag_blk, upd, 0.0)) * gamma_end[b][:, sl]

        yc = y - seg_sum(y) * (1.0 / D)
        var = seg_sum(yc * yc) * (1.0 / D)
        yn = yc * lax.rsqrt(var + GN_EPS) * gng_ref[:, sl] + gnb_ref[:, sl]
        bonus = seg_sum(rkk[rs, sl]) * v_p[i]
        y_ref[b, :, sl] = (yn + bonus) * gsilu[rs, sl]


def _rwkv_branch(rw, batch, seq, w0, w2, a0, a2, k_k, k_a, r_k, gn_gain, gn_bias):
    C = RWKV_CHUNK
    NB = RWKV_BATCH_ROWS
    rw3 = rw.reshape(batch, seq, RW_COLS)
    const = lambda shape: pl.BlockSpec(shape, lambda b, c: (0, 0))
    col_block = lambda j: pl.BlockSpec((NB, C, RWKV_DIM), lambda b, c: (b, c, j))
    return pl.pallas_call(
        _rwkv_kernel,
        grid=(batch // NB, seq // C),
        in_specs=[
            col_block(0), col_block(1), col_block(2), col_block(3),
            pl.BlockSpec((NB, C, SMALL_WIDTH), lambda b, c: (b, c, 4 * RWKV_DIM // SMALL_WIDTH)),
            const((1, RWKV_DIM)), const((LORA, RWKV_DIM)), const((1, RWKV_DIM)), const((LORA, RWKV_DIM)),
            const((1, RWKV_DIM)), const((1, RWKV_DIM)), const((1, RWKV_DIM)),
            const((1, RWKV_DIM)), const((1, RWKV_DIM)),
        ],
        out_specs=pl.BlockSpec((NB, C, RWKV_DIM), lambda b, c: (b, c, 0)),
        out_shape=jax.ShapeDtypeStruct((batch, seq, RWKV_DIM), F32),
        scratch_shapes=[
            pltpu.VMEM((NB, RWKV_HEADS // 2, 2 * RWKV_HEAD_DIM, 2 * RWKV_HEAD_DIM), F32),
        ],
        compiler_params=pltpu.CompilerParams(
            dimension_semantics=("arbitrary", "arbitrary"), vmem_limit_bytes=VMEM_LIMIT),
        name="rwkv_branch",
    )(rw3, rw3, rw3, rw3, rw3, w0, w2, a0, a2, k_k, k_a, r_k, gn_gain, gn_bias
      ).reshape(batch * seq, RWKV_DIM)


def _merge_kernel(x_ref, ys_ref, yr_ref, g0_ref, g1_ref, bg_ref, ws_ref, wr_ref, wo_ref, gain_ref, o_ref):
    g_ssm = _sigmoid(g0_ref[...] + bg_ref[:, 0:D_MODEL])
    g_rwkv = _sigmoid(g1_ref[...] + bg_ref[:, D_MODEL:2 * D_MODEL])
    merged = (g_ssm * _dot(ys_ref[...].astype(BF16), ws_ref[...])
              + g_rwkv * _dot(yr_ref[...].astype(BF16), wr_ref[...]))
    out = _dot(merged.astype(BF16), wo_ref[...])
    ms = jnp.mean(out * out, axis=-1, keepdims=True)
    o_ref[...] = x_ref[...] + out * lax.rsqrt(ms + EPS) * gain_ref[...]


def _merge(x2d, y_ssm, y_rwkv, zg, b_gate, w_ssm, w_rwkv, w_out, post_gain):
    t = x2d.shape[0]
    tm = MERGE_TM
    const = lambda shape: pl.BlockSpec(shape, lambda i: (0, 0))
    return pl.pallas_call(
        _merge_kernel,
        grid=(t // tm,),
        in_specs=[
            pl.BlockSpec((tm, D_MODEL), lambda i: (i, 0)),
            pl.BlockSpec((tm, SSM_INNER), lambda i: (i, 0)),
            pl.BlockSpec((tm, RWKV_DIM), lambda i: (i, 0)),
            pl.BlockSpec((tm, D_MODEL), lambda i: (i, SSM_INNER // D_MODEL)),
            pl.BlockSpec((tm, D_MODEL), lambda i: (i, SSM_INNER // D_MODEL + 1)),
            const((1, 2 * D_MODEL)),
            const((SSM_INNER, D_MODEL)), const((RWKV_DIM, D_MODEL)), const((D_MODEL, D_MODEL)),
            const((1, D_MODEL)),
        ],
        out_specs=pl.BlockSpec((tm, D_MODEL), lambda i: (i, 0)),
        out_shape=jax.ShapeDtypeStruct((t, D_MODEL), F32),
        compiler_params=pltpu.CompilerParams(
            dimension_semantics=("arbitrary",), vmem_limit_bytes=VMEM_LIMIT),
        name="gated_merge",
    )(x2d, y_ssm, y_rwkv, zg, zg, b_gate, w_ssm, w_rwkv, w_out, post_gain)


def _pad_lanes(v, width):
    return jnp.pad(v, ((0, 0), (0, width - v.shape[-1])))


def _layer(x, pre_gain, w_in, b_gate, conv_w, conv_b, dt_bias, a_log, d_skip, ssm_norm_gain,
           rwkv_mu, decay_w0, decay_w2, iclr_a0, iclr_a2, k_k, k_a, r_k, gn_gain, gn_bias,
           w_branch_ssm, w_branch_rwkv, w_out, post_gain):
    batch, seq, _ = x.shape
    x2d = x.reshape(batch * seq, D_MODEL)
    row = lambda v: v.reshape(1, -1)

    o_xbc, o_dt, o_rw = SSM_INNER, SSM_INNER + SSM_XBC, SSM_INNER + SSM_XBC + SSM_HEADS
    o_lora, o_gate = o_rw + 4 * RWKV_DIM, o_rw + 4 * RWKV_DIM + 2 * LORA
    pad = SMALL_WIDTH - 2 * LORA - SSM_HEADS
    w_xbc = w_in[:, o_xbc:o_dt].astype(BF16)
    w_rw = jnp.concatenate([w_in[:, o_rw:o_gate], w_in[:, o_dt:o_rw], jnp.zeros((D_MODEL, pad + RW_PAD), w_in.dtype)],
                           axis=1).astype(BF16)
    w_zg = jnp.concatenate([w_in[:, :o_xbc], w_in[:, o_gate:]], axis=1).astype(BF16)
    mu = _pad_lanes(row(rwkv_mu), RW_COLS)

    gain = row(pre_gain)
    u = _project(functools.partial(_proj_conv_kernel, seq), x2d, gain, w_xbc, [conv_w, row(conv_b)], True)
    rw = _project(functools.partial(_proj_shift_kernel, seq), x2d, gain, w_rw, [mu], True)
    zg = _project(_proj_plain_kernel, x2d, gain, w_zg, [], False)

    head_of_col = jnp.arange(SSM_INNER) // SSM_HEAD_DIM
    expand = (jnp.arange(LANES)[:, None] == head_of_col[None, :]).astype(BF16)
    y_ssm = _ssd_branch(
        u, zg, rw, batch, seq, _pad_lanes(row(dt_bias), LANES), _pad_lanes(row(a_log), LANES),
        row(jnp.repeat(d_skip, SSM_HEAD_DIM)), row(ssm_norm_gain), expand)

    y_rwkv = _rwkv_branch(
        rw, batch, seq, row(decay_w0), decay_w2.astype(BF16), row(iclr_a0), iclr_a2.astype(BF16),
        row(k_k), row(k_a), row(r_k), row(gn_gain), row(gn_bias))

    out = _merge(x2d, y_ssm, y_rwkv, zg, row(b_gate), w_branch_ssm.astype(BF16),
                 w_branch_rwkv.astype(BF16), w_out.astype(BF16), row(post_gain))
    return out.reshape(batch, seq, D_MODEL)


def kernel(x, pre_gain, w_in, b_gate, conv_w, conv_b, dt_bias, a_log, d_skip, ssm_norm_gain, rwkv_mu,
           decay_w0, decay_w2, iclr_a0, iclr_a2, k_k, k_a, r_k, gn_gain, gn_bias, w_branch_ssm,
           w_branch_rwkv, w_out, post_gain):
    for layer in range(pre_gain.shape[0]):
        x = _layer(
            x, pre_gain[layer], w_in[layer], b_gate[layer], conv_w[layer], conv_b[layer], dt_bias[layer],
            a_log[layer], d_skip[layer], ssm_norm_gain[layer], rwkv_mu[layer], decay_w0[layer],
            decay_w2[layer], iclr_a0[layer], iclr_a2[layer], k_k[layer], k_a[layer], r_k[layer],
            gn_gain[layer], gn_bias[layer], w_branch_ssm[layer], w_branch_rwkv[layer], w_out[layer],
            post_gain[layer])
    return x
```

```python
import functools

import jax
import jax.numpy as jnp
from jax import lax
from jax.experimental import pallas as pl
from jax.experimental.pallas import tpu as pltpu

D_MODEL = 1024
EPS = 1e-6

SSM_INNER = 2048
SSM_HEAD_DIM = 64
SSM_HEADS = 32
SSM_STATE = 128
SSM_GROUPS = 4
SSM_GROUP_WIDTH = SSM_INNER // SSM_GROUPS
CONV_WIDTH = 4
SSM_XBC = 3072

RWKV_DIM = 1024
RWKV_HEAD_DIM = 64
RWKV_HEADS = 16
LORA = 64
GN_EPS = RWKV_HEAD_DIM * 1e-5
DECAY_SCALE = 0.6065306597126334

LANES = 128
SUBLANES = 8

SMALL_WIDTH = 256
RW_PAD = 256
RW_COLS = 4 * RWKV_DIM + SMALL_WIDTH + RW_PAD
ZG_COLS = SSM_INNER + 2 * D_MODEL

PROJ_TM = 1024
PROJ_SUBTILE = 256
PROJ_TN = {SSM_XBC: 1024, RW_COLS: RW_COLS // 3, ZG_COLS: 2048}
SSD_CHUNK = 128
SSD_CHUNKS_PER_STEP = 2
RWKV_CHUNK = 64
RWKV_BATCH_ROWS = 2
RWKV_CHUNKS_PER_STEP = 2
MERGE_TM = 512
VMEM_LIMIT = 48 * 1024 * 1024

BF16 = jnp.bfloat16
F32 = jnp.float32


def _dot(a, b):
    return jnp.dot(a, b, preferred_element_type=F32)


def _dot_nt(a, b):
    return lax.dot_general(a, b, (((1,), (1,)), ((), ())), preferred_element_type=F32)


def _dot_tn(a, b):
    return lax.dot_general(a, b, (((0,), (0,)), ((), ())), preferred_element_type=F32)


def _split_terms(x, n):
    terms = []
    rem = x
    for _ in range(n):
        t = rem.astype(BF16)
        terms.append(t)
        rem = rem - t.astype(F32)
    return terms


def _dot_left_exact(m_bf16, x, n=3):
    acc = None
    for t in _split_terms(x, n):
        p = _dot(m_bf16, t)
        acc = p if acc is None else acc + p
    return acc


def _dot_terms(terms, m_bf16):
    acc = None
    for t in terms:
        p = _dot(t, m_bf16)
        acc = p if acc is None else acc + p
    return acc


def _softplus(x):
    return jnp.maximum(x, 0.0) + jnp.log(1.0 + jnp.exp(-jnp.abs(x)))


def _sigmoid(x):
    return 1.0 / (1.0 + jnp.exp(-x))


def _lower_tri(n, strict):
    row = lax.broadcasted_iota(jnp.int32, (n, n), 0)
    col = lax.broadcasted_iota(jnp.int32, (n, n), 1)
    return (col < row) if strict else (col <= row)


def _lower_tri_ones(n):
    row = lax.broadcasted_iota(jnp.int32, (n, n), 0)
    col = lax.broadcasted_iota(jnp.int32, (n, n), 1)
    return jnp.clip(row - col + 1, 0, 1).astype(F32).astype(BF16)


def _normed_input(x_ref, gain_ref, h_ref):
    @pl.when(pl.program_id(1) == 0)
    def _():
        x = x_ref[...]
        ms = jnp.mean(x * x, axis=-1, keepdims=True)
        h_ref[...] = (x * lax.rsqrt(ms + EPS) * gain_ref[...]).astype(BF16)


def _delayed(o, halo, k):
    head = jnp.concatenate([halo, o[:SUBLANES]], axis=0)[SUBLANES - k:2 * SUBLANES - k]
    if o.shape[0] == SUBLANES:
        return head
    return jnp.concatenate([head, pltpu.roll(o, k, 0)[SUBLANES:]], axis=0)


def _zero_carry_at_sequence_start(carry_ref, rows_per_seq):
    i, j = pl.program_id(0), pl.program_id(1)

    @pl.when(i % (rows_per_seq // PROJ_TM) == 0)
    def _():
        carry_ref[j] = jnp.zeros(carry_ref.shape[1:], F32)


def _proj_plain_kernel(h_ref, w_ref, o_ref):
    o_ref[...] = _dot(h_ref[...], w_ref[...])


def _proj_conv_kernel(rows_per_seq, x_ref, gain_ref, w_ref, cw_ref, cb_ref, o_ref, h_ref, carry_ref):
    _normed_input(x_ref, gain_ref, h_ref)
    _zero_carry_at_sequence_start(carry_ref, rows_per_seq)
    j = pl.program_id(1)
    for c0 in range(0, o_ref.shape[1], PROJ_SUBTILE):
        cs = slice(c0, c0 + PROJ_SUBTILE)
        o = _dot(h_ref[...], w_ref[:, cs])
        halo = carry_ref[j, :, cs]
        carry_ref[j, :, cs] = o[PROJ_TM - SUBLANES:, :]
        e = _delayed(o, halo, 1)
        e_halo = _delayed(halo, jnp.zeros_like(halo), 1)
        inner = cw_ref[1:2, cs] * o + cw_ref[0:1, cs] * e
        inner_halo = cw_ref[1:2, cs] * halo + cw_ref[0:1, cs] * e_halo
        acc = cb_ref[:, cs] + cw_ref[3:4, cs] * o + cw_ref[2:3, cs] * e + _delayed(inner, inner_halo, 2)
        o_ref[:, cs] = acc * _sigmoid(acc)


def _proj_shift_kernel(rows_per_seq, h_ref, w_ref, mu_ref, o_ref, carry_ref):
    _zero_carry_at_sequence_start(carry_ref, rows_per_seq)
    j = pl.program_id(1)
    for c0 in range(0, o_ref.shape[1], PROJ_SUBTILE):
        cs = slice(c0, c0 + PROJ_SUBTILE)
        o = _dot(h_ref[...], w_ref[:, cs])
        halo = carry_ref[j, :, cs]
        carry_ref[j, :, cs] = o[PROJ_TM - SUBLANES:, :]
        o_ref[:, cs] = o + mu_ref[:, cs] * (_delayed(o, halo, 1) - o)


def _project_normed(kernel_fn, x2d, pre_gain, w, col_params):
    t, n = x2d.shape[0], w.shape[1]
    tn = PROJ_TN[n]
    return pl.pallas_call(
        kernel_fn,
        grid=(t // PROJ_TM, n // tn),
        in_specs=[
            pl.BlockSpec((PROJ_TM, D_MODEL), lambda i, j: (i, 0)),
            pl.BlockSpec((1, D_MODEL), lambda i, j: (0, 0)),
            pl.BlockSpec((D_MODEL, tn), lambda i, j: (0, j)),
        ] + [pl.BlockSpec((c.shape[0], tn), lambda i, j: (0, j)) for c in col_params],
        out_specs=[pl.BlockSpec((PROJ_TM, tn), lambda i, j: (i, j)),
                   pl.BlockSpec((PROJ_TM, D_MODEL), lambda i, j: (i, 0))],
        out_shape=[jax.ShapeDtypeStruct((t, n), F32), jax.ShapeDtypeStruct((t, D_MODEL), BF16)],
        scratch_shapes=[pltpu.VMEM((n // tn, SUBLANES, tn), F32)],
        compiler_params=pltpu.CompilerParams(
            dimension_semantics=("arbitrary", "arbitrary"), vmem_limit_bytes=VMEM_LIMIT),
        name="in_proj_%d" % n,
    )(x2d, pre_gain, w, *col_params)


def _project(kernel_fn, h, w, col_params, carry):
    t, n = h.shape[0], w.shape[1]
    tn = PROJ_TN[n]
    return pl.pallas_call(
        kernel_fn,
        grid=(t // PROJ_TM, n // tn),
        in_specs=[
            pl.BlockSpec((PROJ_TM, D_MODEL), lambda i, j: (i, 0)),
            pl.BlockSpec((D_MODEL, tn), lambda i, j: (0, j)),
        ] + [pl.BlockSpec((c.shape[0], tn), lambda i, j: (0, j)) for c in col_params],
        out_specs=pl.BlockSpec((PROJ_TM, tn), lambda i, j: (i, j)),
        out_shape=jax.ShapeDtypeStruct((t, n), F32),
        scratch_shapes=[pltpu.VMEM((n // tn, SUBLANES, tn), F32)] if carry else [],
        compiler_params=pltpu.CompilerParams(
            dimension_semantics=("arbitrary", "arbitrary"), vmem_limit_bytes=VMEM_LIMIT),
        name="in_proj_%d" % n,
    )(h, w, *col_params)


def _ssd_kernel(u_ref, z_ref, sm_ref, dtb_ref, alog_ref, dskip_ref, gain_ref, expand_ref,
                y_ref, st_ref, yd_ref):
    L = SSD_CHUNK

    @pl.when(pl.program_id(1) == 0)
    def _():
        st_ref[...] = jnp.zeros_like(st_ref)

    tri = _lower_tri(L, strict=False)
    tri_ones = _lower_tri_ones(L)
    for ci in range(SSD_CHUNKS_PER_STEP):
        rows = slice(ci * L, (ci + 1) * L)
        dt = _softplus(sm_ref[rows, LANES:2 * LANES] + dtb_ref[...])
        a = dt * (-jnp.exp(alog_ref[...]))
        acum = _dot_left_exact(tri_ones, a)
        acum_t = acum.T
        a_last = acum[L - 1:L, :]
        dt_terms = _split_terms(dt, 2)
        dec_out_terms = _split_terms(jnp.exp(acum), 2)
        dec_state_terms = _split_terms(jnp.exp(a_last - acum), 2)

        for g in range(SSM_GROUPS):
            c0 = g * SSM_GROUP_WIDTH
            ex = expand_ref[:, c0:c0 + SSM_GROUP_WIDTH]
            xs = u_ref[rows, c0:c0 + SSM_GROUP_WIDTH]
            b0 = SSM_INNER + g * SSM_STATE
            bm = u_ref[rows, b0:b0 + SSM_STATE].astype(BF16)
            cm = u_ref[rows, b0 + SSM_GROUPS * SSM_STATE:b0 + (SSM_GROUPS + 1) * SSM_STATE].astype(BF16)
            dt_x = _dot_terms(dt_terms, ex)
            dec_out_x = _dot_terms(dec_out_terms, ex)
            dec_state_x = _dot_terms(dec_state_terms, ex)
            xdt = xs * dt_x
            xdt_b = xdt.astype(BF16)

            scores = _dot_nt(cm, bm)
            state = st_ref[g]
            y_off = _dot(cm, state.astype(BF16)) * dec_out_x
            st_ref[g] = state * dec_out_x[L - 1:L, :] + _dot_tn(bm, (xdt * dec_state_x).astype(BF16))

            for j in range(SSM_GROUP_WIDTH // SSM_HEAD_DIM):
                h = g * (SSM_GROUP_WIDTH // SSM_HEAD_DIM) + j
                seg = acum[:, h:h + 1] - acum_t[h:h + 1, :]
                mix = (scores * jnp.exp(jnp.where(tri, seg, -jnp.inf))).astype(BF16)
                yd_ref[ci, :, j * SSM_HEAD_DIM:(j + 1) * SSM_HEAD_DIM] = _dot(
                    mix, xdt_b[:, j * SSM_HEAD_DIM:(j + 1) * SSM_HEAD_DIM])

            y = yd_ref[ci] + y_off + dskip_ref[:, c0:c0 + SSM_GROUP_WIDTH] * xs
            z = z_ref[rows, c0:c0 + SSM_GROUP_WIDTH]
            y = y * (z * _sigmoid(z))
            ms = jnp.mean(y * y, axis=-1, keepdims=True)
            y_ref[rows, c0:c0 + SSM_GROUP_WIDTH] = (
                y * lax.rsqrt(ms + EPS) * gain_ref[:, c0:c0 + SSM_GROUP_WIDTH]).astype(BF16)


def _ssd_branch(u, zg, rw, batch, seq, dt_bias, a_log, dskip_x, norm_gain, expand):
    L = SSD_CHUNK * SSD_CHUNKS_PER_STEP
    nc = seq // L
    row = lambda b, c: b * nc + c
    const = lambda shape: pl.BlockSpec(shape, lambda b, c: (0, 0))
    return pl.pallas_call(
        _ssd_kernel,
        grid=(batch, nc),
        in_specs=[
            pl.BlockSpec((L, SSM_XBC), lambda b, c: (row(b, c), 0)),
            pl.BlockSpec((L, SSM_INNER), lambda b, c: (row(b, c), 0)),
            pl.BlockSpec((L, SMALL_WIDTH), lambda b, c: (row(b, c), 4 * RWKV_DIM // SMALL_WIDTH)),
            const((1, LANES)),
            const((1, LANES)),
            const((1, SSM_INNER)),
            const((1, SSM_INNER)),
            const((LANES, SSM_INNER)),
        ],
        out_specs=pl.BlockSpec((L, SSM_INNER), lambda b, c: (row(b, c), 0)),
        out_shape=jax.ShapeDtypeStruct((batch * seq, SSM_INNER), BF16),
        scratch_shapes=[
            pltpu.VMEM((SSM_GROUPS, SSM_STATE, SSM_GROUP_WIDTH), F32),
            pltpu.VMEM((SSD_CHUNKS_PER_STEP, SSD_CHUNK, SSM_GROUP_WIDTH), F32),
        ],
        compiler_params=pltpu.CompilerParams(
            dimension_semantics=("arbitrary", "arbitrary"), vmem_limit_bytes=VMEM_LIMIT),
        name="ssd_branch",
    )(u, zg, rw, dt_bias, a_log, dskip_x, norm_gain, expand)


def _rwkv_kernel(r_ref, k_ref, v_ref, g_ref, sm_ref, w0_ref, w2_ref, a0_ref, a2_ref, kks_ref, kas_ref,
                 rk_ref, gng_ref, gnb_ref, y_ref, s_ref):
    C = RWKV_CHUNK
    D = RWKV_HEAD_DIM
    NB = RWKV_BATCH_ROWS
    CP = RWKV_CHUNKS_PER_STEP
    R = NB * CP * C

    @pl.when(pl.program_id(1) == 0)
    def _():
        s_ref[...] = jnp.zeros_like(s_ref)

    rows = lambda x_ref: x_ref[...].reshape(R, x_ref.shape[-1])
    r, k, v, gate, sm = rows(r_ref), rows(k_ref), rows(v_ref), rows(g_ref), rows(sm_ref)
    wd = sm[:, 0:LORA]
    ad = sm[:, LORA:2 * LORA]

    lw = _dot(jnp.tanh(wd).astype(BF16), w2_ref[...])
    logw = (-DECAY_SCALE) * _sigmoid(w0_ref[...] + lw)
    neg_a = 1.0 / (-1.0 - jnp.exp(-(a0_ref[...] + _dot(ad.astype(BF16), a2_ref[...]))))
    kkp = k * kks_ref[...]
    k2 = k * (1.0 - (neg_a + 1.0) * kas_ref[...])

    rr = lax.broadcasted_iota(jnp.int32, (R, R), 0)
    cc = lax.broadcasted_iota(jnp.int32, (R, R), 1)
    same_seq = 1 - jnp.clip(jnp.abs((rr // C) - (cc // C)), 0, 1)
    tri = (jnp.clip(rr - cc + 1, 0, 1) * same_seq).astype(F32).astype(BF16)
    cum = _dot_left_exact(tri, logw)
    e_pos = jnp.exp(cum)
    e_neg = 1.0 / e_pos
    kc_p = kkp * jnp.exp(cum - logw)
    rt = r * e_pos
    kt = k2 * e_neg
    nbt_p = kkp * neg_a * e_neg
    gamma_end = [jnp.exp(cum[(q + 1) * C - 1:(q + 1) * C, :]) for q in range(NB * CP)]
    rkk = r * k2 * rk_ref[...]
    gsilu = gate * _sigmoid(gate)

    P = 2 * D
    lane_a = lax.broadcasted_iota(jnp.int32, (C, P), 1) < D
    lane_a2 = lax.broadcasted_iota(jnp.int32, (2 * C, P), 1) < D
    row2 = lax.broadcasted_iota(jnp.int32, (2 * C, 2 * C), 0)
    col2 = lax.broadcasted_iota(jnp.int32, (2 * C, 2 * C), 1)
    colm = jnp.where(col2 >= C, col2 - C, col2)
    gmask = colm < jnp.where(row2 >= C, row2 - C + 1, row2)
    diag_blk = jnp.where(row2 >= C, 1, 0) == jnp.where(col2 >= C, 1, 0)
    anti_eye = jnp.where(col2 == jnp.where(row2 >= C, row2 - C, row2 + C), 1.0, 0.0)

    def seg_sum(x):
        sa = jnp.sum(jnp.where(lane_a, x, 0.0), axis=-1, keepdims=True)
        sb = jnp.sum(jnp.where(lane_a, 0.0, x), axis=-1, keepdims=True)
        return jnp.where(lane_a, sa, sb)

    def cross(lo, hi):
        return jnp.concatenate([lo, hi], axis=0).astype(BF16)

    units = [(b, j) for b in range(NB) for j in range(RWKV_HEADS // 2)]
    n_units = range(len(units))

    def chunk_chain(ci):
        idx = [(slice((b * CP + ci) * C, (b * CP + ci + 1) * C), slice(j * P, (j + 1) * P)) for b, j in units]
        xb, wa, g_top, g_bot, v_p = [], [], [], [], []
        for rs, sl in idx:
            kkp_p = kkp[rs, sl]
            rinv = 1.0 / jnp.maximum(jnp.sqrt(seg_sum(kkp_p * kkp_p)), 1e-12)
            nbt = nbt_p[rs, sl] * rinv
            xp = jnp.concatenate([kc_p[rs, sl] * rinv, rt[rs, sl]], axis=0)
            w_a = jnp.concatenate([nbt, kt[rs, sl]], axis=0).astype(BF16)
            w_b = jnp.concatenate([kt[rs, sl], nbt], axis=0).astype(BF16)
            g_a = jnp.where(gmask, _dot_nt(jnp.where(lane_a2, xp, 0.0).astype(BF16), w_a), 0.0)
            g_b = jnp.where(gmask, _dot_nt(jnp.where(lane_a2, 0.0, xp).astype(BF16), w_b), 0.0)
            xb.append(xp.astype(BF16))
            wa.append(w_a)
            g_top.append(jnp.concatenate([g_a[:C], g_b[:C]], axis=0))
            g_bot.append(jnp.concatenate([g_a[C:], g_b[C:]], axis=0))
            v_p.append(v[rs, sl])

        z = [_dot_nt(xb[i], s_ref[units[i][0], units[i][1]].astype(BF16)) for i in n_units]
        w1 = []
        for i in n_units:
            rw = _dot(g_top[i].astype(BF16),
                      cross(jnp.where(lane_a, 0.0, v_p[i]), jnp.where(lane_a, v_p[i], 0.0)))
            w1.append(jnp.where(lane_a, rw[:C], rw[C:]))

        pt = [jnp.where(diag_blk, g_top[i], anti_eye) for i in n_units]
        for _ in range(6):
            pt = [_dot(jnp.where(diag_blk, pt[i], 0.0).astype(BF16), pt[i].astype(BF16))
                  + jnp.where(diag_blk, 0.0, pt[i]) for i in n_units]

        u = []
        for i in n_units:
            rhs = z[i][:C] + w1[i]
            ru = _dot(pt[i].astype(BF16), cross(jnp.where(lane_a, 0.0, rhs), jnp.where(lane_a, rhs, 0.0)))
            u.append(ru[:C] + ru[C:])
        for i in n_units:
            (b, j), (rs, sl) = units[i], idx[i]
            ry = _dot(g_bot[i].astype(BF16),
                      cross(jnp.where(lane_a, u[i], v_p[i]), jnp.where(lane_a, v_p[i], u[i])))
            y = z[i][C:] + jnp.where(lane_a, ry[:C], ry[C:])
            upd = _dot_tn(jnp.concatenate([u[i], v_p[i]], axis=0).astype(BF16), wa[i])
            s_ref[b, j] = (s_ref[b, j] + jnp.where(diag_blk, upd, 0.0)) * gamma_end[b * CP + ci][:, sl]

            yc = y - seg_sum(y) * (1.0 / D)
            var = seg_sum(yc * yc) * (1.0 / D)
            yn = yc * lax.rsqrt(var + GN_EPS) * gng_ref[:, sl] + gnb_ref[:, sl]
            bonus = seg_sum(rkk[rs, sl]) * v_p[i]
            y_ref[b, ci * C:(ci + 1) * C, sl] = ((yn + bonus) * gsilu[rs, sl]).astype(BF16)

    for ci in range(CP):
        chunk_chain(ci)


def _rwkv_branch(rw, batch, seq, w0, w2, a0, a2, k_k, k_a, r_k, gn_gain, gn_bias):
    C = RWKV_CHUNK
    NB = RWKV_BATCH_ROWS
    CP = RWKV_CHUNKS_PER_STEP
    rw3 = rw.reshape(batch, seq, RW_COLS)
    const = lambda shape: pl.BlockSpec(shape, lambda b, c: (0, 0))
    col_block = lambda j: pl.BlockSpec((NB, CP * C, RWKV_DIM), lambda b, c: (b, c, j))
    return pl.pallas_call(
        _rwkv_kernel,
        grid=(batch // NB, seq // (CP * C)),
        in_specs=[
            col_block(0), col_block(1), col_block(2), col_block(3),
            pl.BlockSpec((NB, CP * C, SMALL_WIDTH), lambda b, c: (b, c, 4 * RWKV_DIM // SMALL_WIDTH)),
            const((1, RWKV_DIM)), const((LORA, RWKV_DIM)), const((1, RWKV_DIM)), const((LORA, RWKV_DIM)),
            const((1, RWKV_DIM)), const((1, RWKV_DIM)), const((1, RWKV_DIM)),
            const((1, RWKV_DIM)), const((1, RWKV_DIM)),
        ],
        out_specs=pl.BlockSpec((NB, CP * C, RWKV_DIM), lambda b, c: (b, c, 0)),
        out_shape=jax.ShapeDtypeStruct((batch, seq, RWKV_DIM), BF16),
        scratch_shapes=[
            pltpu.VMEM((NB, RWKV_HEADS // 2, 2 * RWKV_HEAD_DIM, 2 * RWKV_HEAD_DIM), F32),
        ],
        compiler_params=pltpu.CompilerParams(
            dimension_semantics=("arbitrary", "arbitrary"), vmem_limit_bytes=VMEM_LIMIT),
        name="rwkv_branch",
    )(rw3, rw3, rw3, rw3, rw3, w0, w2, a0, a2, k_k, k_a, r_k, gn_gain, gn_bias
      ).reshape(batch * seq, RWKV_DIM)


def _merge_kernel(x_ref, ys_ref, yr_ref, g0_ref, g1_ref, bg_ref, ws_ref, wr_ref, wo_ref, gain_ref, o_ref):
    g_ssm = _sigmoid(g0_ref[...] + bg_ref[:, 0:D_MODEL])
    g_rwkv = _sigmoid(g1_ref[...] + bg_ref[:, D_MODEL:2 * D_MODEL])
    merged = g_ssm * _dot(ys_ref[...], ws_ref[...]) + g_rwkv * _dot(yr_ref[...], wr_ref[...])
    out = _dot(merged.astype(BF16), wo_ref[...])
    ms = jnp.mean(out * out, axis=-1, keepdims=True)
    o_ref[...] = x_ref[...] + out * lax.rsqrt(ms + EPS) * gain_ref[...]


def _merge(x2d, y_ssm, y_rwkv, zg, b_gate, w_ssm, w_rwkv, w_out, post_gain):
    t = x2d.shape[0]
    tm = MERGE_TM
    const = lambda shape: pl.BlockSpec(shape, lambda i: (0, 0), pipeline_mode=pl.Buffered(1))
    return pl.pallas_call(
        _merge_kernel,
        grid=(t // tm,),
        in_specs=[
            pl.BlockSpec((tm, D_MODEL), lambda i: (i, 0)),
            pl.BlockSpec((tm, SSM_INNER), lambda i: (i, 0)),
            pl.BlockSpec((tm, RWKV_DIM), lambda i: (i, 0)),
            pl.BlockSpec((tm, D_MODEL), lambda i: (i, SSM_INNER // D_MODEL)),
            pl.BlockSpec((tm, D_MODEL), lambda i: (i, SSM_INNER // D_MODEL + 1)),
            const((1, 2 * D_MODEL)),
            const((SSM_INNER, D_MODEL)), const((RWKV_DIM, D_MODEL)), const((D_MODEL, D_MODEL)),
            const((1, D_MODEL)),
        ],
        out_specs=pl.BlockSpec((tm, D_MODEL), lambda i: (i, 0)),
        out_shape=jax.ShapeDtypeStruct((t, D_MODEL), F32),
        compiler_params=pltpu.CompilerParams(
            dimension_semantics=("arbitrary",), vmem_limit_bytes=VMEM_LIMIT),
        name="gated_merge",
    )(x2d, y_ssm, y_rwkv, zg, zg, b_gate, w_ssm, w_rwkv, w_out, post_gain)


def _pad_lanes(v, width):
    return jnp.pad(v, ((0, 0), (0, width - v.shape[-1])))


def _layer(x, pre_gain, w_in, b_gate, conv_w, conv_b, dt_bias, a_log, d_skip, ssm_norm_gain,
           rwkv_mu, decay_w0, decay_w2, iclr_a0, iclr_a2, k_k, k_a, r_k, gn_gain, gn_bias,
           w_branch_ssm, w_branch_rwkv, w_out, post_gain):
    batch, seq, _ = x.shape
    x2d = x.reshape(batch * seq, D_MODEL)
    row = lambda v: v.reshape(1, -1)

    o_xbc, o_dt, o_rw = SSM_INNER, SSM_INNER + SSM_XBC, SSM_INNER + SSM_XBC + SSM_HEADS
    o_lora, o_gate = o_rw + 4 * RWKV_DIM, o_rw + 4 * RWKV_DIM + 2 * LORA
    pad = SMALL_WIDTH - 2 * LORA - SSM_HEADS
    w_xbc = w_in[:, o_xbc:o_dt].astype(BF16)
    w_rw = jnp.concatenate([w_in[:, o_rw:o_gate], w_in[:, o_dt:o_rw], jnp.zeros((D_MODEL, pad + RW_PAD), w_in.dtype)],
                           axis=1).astype(BF16)
    w_zg = jnp.concatenate([w_in[:, :o_xbc], w_in[:, o_gate:]], axis=1).astype(BF16)
    mu = _pad_lanes(row(rwkv_mu), RW_COLS)

    gain = row(pre_gain)
    u, h = _project_normed(functools.partial(_proj_conv_kernel, seq), x2d, gain, w_xbc, [conv_w, row(conv_b)])
    rw = _project(functools.partial(_proj_shift_kernel, seq), h, w_rw, [mu], True)
    zg = _project(_proj_plain_kernel, h, w_zg, [], False)

    head_of_col = jnp.arange(SSM_INNER) // SSM_HEAD_DIM
    expand = (jnp.arange(LANES)[:, None] == head_of_col[None, :]).astype(BF16)
    y_ssm = _ssd_branch(
        u, zg, rw, batch, seq, _pad_lanes(row(dt_bias), LANES), _pad_lanes(row(a_log), LANES),
        row(jnp.repeat(d_skip, SSM_HEAD_DIM)), row(ssm_norm_gain), expand)

    y_rwkv = _rwkv_branch(
        rw, batch, seq, row(decay_w0), decay_w2.astype(BF16), row(iclr_a0), iclr_a2.astype(BF16),
        row(k_k), row(k_a), row(r_k), row(gn_gain), row(gn_bias))

    out = _merge(x2d, y_ssm, y_rwkv, zg, row(b_gate), w_branch_ssm.astype(BF16),
                 w_branch_rwkv.astype(BF16), w_out.astype(BF16), row(post_gain))
    return out.reshape(batch, seq, D_MODEL)


def kernel(x, pre_gain, w_in, b_gate, conv_w, conv_b, dt_bias, a_log, d_skip, ssm_norm_gain, rwkv_mu,
           decay_w0, decay_w2, iclr_a0, iclr_a2, k_k, k_a, r_k, gn_gain, gn_bias, w_branch_ssm,
           w_branch_rwkv, w_out, post_gain):
    for layer in range(pre_gain.shape[0]):
        x = _layer(
            x, pre_gain[layer], w_in[layer], b_gate[layer], conv_w[layer], conv_b[layer], dt_bias[layer],
            a_log[layer], d_skip[layer], ssm_norm_gain[layer], rwkv_mu[layer], decay_w0[layer],
            decay_w2[layer], iclr_a0[layer], iclr_a2[layer], k_k[layer], k_a[layer], r_k[layer],
            gn_gain[layer], gn_bias[layer], w_branch_ssm[layer], w_branch_rwkv[layer], w_out[layer],
            post_gain[layer])
    return x
```

```python
import functools

import jax
import jax.numpy as jnp
from jax import lax
from jax.experimental import pallas as pl
from jax.experimental.pallas import tpu as pltpu

D_MODEL = 1024
EPS = 1e-6

SSM_INNER = 2048
SSM_HEAD_DIM = 64
SSM_HEADS = 32
SSM_STATE = 128
SSM_GROUPS = 4
SSM_GROUP_WIDTH = SSM_INNER // SSM_GROUPS
CONV_WIDTH = 4
SSM_XBC = 3072

RWKV_DIM = 1024
RWKV_HEAD_DIM = 64
RWKV_HEADS = 16
LORA = 64
GN_EPS = RWKV_HEAD_DIM * 1e-5
DECAY_SCALE = 0.6065306597126334

LANES = 128
SUBLANES = 8

SMALL_WIDTH = 256
RW_PAD = 256
RW_COLS = 4 * RWKV_DIM + SMALL_WIDTH + RW_PAD
ZG_COLS = SSM_INNER + 2 * D_MODEL

PROJ_TM = {SSM_XBC: 1024, RW_COLS: 2048, ZG_COLS: 2048}
PROJ_SUBTILE = 256
PROJ_TN = {SSM_XBC: 1024, RW_COLS: RW_COLS // 6, ZG_COLS: 1024}
SSD_CHUNK = 128
SSD_CHUNKS_PER_STEP = 2
RWKV_CHUNK = 64
RWKV_BATCH_ROWS = 2
RWKV_CHUNKS_PER_STEP = 2
MERGE_TM = 512
VMEM_LIMIT = 48 * 1024 * 1024

BF16 = jnp.bfloat16
F32 = jnp.float32


def _dot(a, b):
    return jnp.dot(a, b, preferred_element_type=F32)


def _dot_nt(a, b):
    return lax.dot_general(a, b, (((1,), (1,)), ((), ())), preferred_element_type=F32)


def _dot_tn(a, b):
    return lax.dot_general(a, b, (((0,), (0,)), ((), ())), preferred_element_type=F32)


def _split_terms(x, n):
    terms = []
    rem = x
    for _ in range(n):
        t = rem.astype(BF16)
        terms.append(t)
        rem = rem - t.astype(F32)
    return terms


def _dot_left_exact(m_bf16, x, n=3):
    acc = None
    for t in _split_terms(x, n):
        p = _dot(m_bf16, t)
        acc = p if acc is None else acc + p
    return acc


def _two_terms_on_lanes(x):
    lane = lax.broadcasted_iota(jnp.int32, x.shape, 1)
    xm = jnp.where(lane < SSM_HEADS, x, 0.0)
    hi = xm.astype(BF16).astype(F32)
    return (hi + pltpu.roll(xm - hi, SSM_HEADS, 1)).astype(BF16)


def _softplus(x):
    return jnp.maximum(x, 0.0) + jnp.log(1.0 + jnp.exp(-jnp.abs(x)))


def _sigmoid(x):
    return 1.0 / (1.0 + jnp.exp(-x))


def _lower_tri(n, strict):
    row = lax.broadcasted_iota(jnp.int32, (n, n), 0)
    col = lax.broadcasted_iota(jnp.int32, (n, n), 1)
    return (col < row) if strict else (col <= row)


def _lower_tri_ones(n):
    row = lax.broadcasted_iota(jnp.int32, (n, n), 0)
    col = lax.broadcasted_iota(jnp.int32, (n, n), 1)
    return jnp.clip(row - col + 1, 0, 1).astype(F32).astype(BF16)


def _normed_input(x_ref, gain_ref, h_ref):
    @pl.when(pl.program_id(1) == 0)
    def _():
        x = x_ref[...]
        ms = jnp.mean(x * x, axis=-1, keepdims=True)
        h_ref[...] = (x * lax.rsqrt(ms + EPS) * gain_ref[...]).astype(BF16)


def _delayed(o, halo, k):
    head = jnp.concatenate([halo, o[:SUBLANES]], axis=0)[SUBLANES - k:2 * SUBLANES - k]
    if o.shape[0] == SUBLANES:
        return head
    return jnp.concatenate([head, pltpu.roll(o, k, 0)[SUBLANES:]], axis=0)


def _zero_carry_at_sequence_start(carry_ref, row_blocks_per_seq):
    i, j = pl.program_id(0), pl.program_id(1)

    @pl.when(i % row_blocks_per_seq == 0)
    def _():
        carry_ref[j] = jnp.zeros(carry_ref.shape[1:], F32)


def _proj_plain_kernel(h_ref, w_ref, o_ref):
    o_ref[...] = _dot(h_ref[...], w_ref[...])


def _proj_conv_kernel(row_blocks_per_seq, x_ref, gain_ref, w_ref, cw_ref, cb_ref, o_ref, h_ref, carry_ref):
    _normed_input(x_ref, gain_ref, h_ref)
    _zero_carry_at_sequence_start(carry_ref, row_blocks_per_seq)
    j = pl.program_id(1)
    for c0 in range(0, o_ref.shape[1], PROJ_SUBTILE):
        cs = slice(c0, c0 + PROJ_SUBTILE)
        o = _dot(h_ref[...], w_ref[:, cs])
        halo = carry_ref[j, :, cs]
        carry_ref[j, :, cs] = o[o.shape[0] - SUBLANES:, :]
        e = _delayed(o, halo, 1)
        e_halo = _delayed(halo, jnp.zeros_like(halo), 1)
        inner = cw_ref[1:2, cs] * o + cw_ref[0:1, cs] * e
        inner_halo = cw_ref[1:2, cs] * halo + cw_ref[0:1, cs] * e_halo
        acc = cb_ref[:, cs] + cw_ref[3:4, cs] * o + cw_ref[2:3, cs] * e + _delayed(inner, inner_halo, 2)
        o_ref[:, cs] = acc * _sigmoid(acc)


def _proj_shift_kernel(row_blocks_per_seq, h_ref, w_ref, mu_ref, o_ref, carry_ref):
    _zero_carry_at_sequence_start(carry_ref, row_blocks_per_seq)
    j = pl.program_id(1)
    for c0 in range(0, o_ref.shape[1], PROJ_SUBTILE):
        cs = slice(c0, c0 + PROJ_SUBTILE)
        o = _dot(h_ref[...], w_ref[:, cs])
        halo = carry_ref[j, :, cs]
        carry_ref[j, :, cs] = o[o.shape[0] - SUBLANES:, :]
        o_ref[:, cs] = o + mu_ref[:, cs] * (_delayed(o, halo, 1) - o)


def _project_normed(kernel_fn, seq, x2d, pre_gain, w, col_params):
    t, n = x2d.shape[0], w.shape[1]
    tm, tn = PROJ_TM[n], PROJ_TN[n]
    return pl.pallas_call(
        functools.partial(kernel_fn, seq // tm),
        grid=(t // tm, n // tn),
        in_specs=[
            pl.BlockSpec((tm, D_MODEL), lambda i, j: (i, 0)),
            pl.BlockSpec((1, D_MODEL), lambda i, j: (0, 0)),
            pl.BlockSpec((D_MODEL, tn), lambda i, j: (0, j)),
        ] + [pl.BlockSpec((c.shape[0], tn), lambda i, j: (0, j)) for c in col_params],
        out_specs=[pl.BlockSpec((tm, tn), lambda i, j: (i, j)),
                   pl.BlockSpec((tm, D_MODEL), lambda i, j: (i, 0))],
        out_shape=[jax.ShapeDtypeStruct((t, n), F32), jax.ShapeDtypeStruct((t, D_MODEL), BF16)],
        scratch_shapes=[pltpu.VMEM((n // tn, SUBLANES, tn), F32)],
        compiler_params=pltpu.CompilerParams(
            dimension_semantics=("arbitrary", "arbitrary"), vmem_limit_bytes=VMEM_LIMIT),
        name="in_proj_%d" % n,
    )(x2d, pre_gain, w, *col_params)


def _project(kernel_fn, seq, h, w, col_params, carry):
    t, n = h.shape[0], w.shape[1]
    tm, tn = PROJ_TM[n], PROJ_TN[n]
    return pl.pallas_call(
        functools.partial(kernel_fn, seq // tm) if carry else kernel_fn,
        grid=(t // tm, n // tn),
        in_specs=[
            pl.BlockSpec((tm, D_MODEL), lambda i, j: (i, 0)),
            pl.BlockSpec((D_MODEL, tn), lambda i, j: (0, j)),
        ] + [pl.BlockSpec((c.shape[0], tn), lambda i, j: (0, j)) for c in col_params],
        out_specs=pl.BlockSpec((tm, tn), lambda i, j: (i, j)),
        out_shape=jax.ShapeDtypeStruct((t, n), F32),
        scratch_shapes=[pltpu.VMEM((n // tn, SUBLANES, tn), F32)] if carry else [],
        compiler_params=pltpu.CompilerParams(
            dimension_semantics=("arbitrary", "arbitrary"), vmem_limit_bytes=VMEM_LIMIT),
        name="in_proj_%d" % n,
    )(h, w, *col_params)


def _ssd_kernel(u_ref, z_ref, sm_ref, dtb_ref, alog_ref, dskip_ref, gain_ref, expand_ref,
                y_ref, st_ref, yd_ref):
    L = SSD_CHUNK

    @pl.when(pl.program_id(1) == 0)
    def _():
        st_ref[...] = jnp.zeros_like(st_ref)

    tri = _lower_tri(L, strict=False)
    tri_ones = _lower_tri_ones(L)
    head_a = lax.broadcasted_iota(jnp.int32, (L, LANES), 1) < SSM_HEAD_DIM
    for ci in range(SSD_CHUNKS_PER_STEP):
        rows = slice(ci * L, (ci + 1) * L)
        dt = _softplus(sm_ref[rows, LANES:2 * LANES] + dtb_ref[...])
        a = dt * (-jnp.exp(alog_ref[...]))
        acum = _dot_left_exact(tri_ones, a)
        acum_t = acum.T
        dt_t = dt.T
        a_last = acum[L - 1:L, :]
        dec_out_p = _two_terms_on_lanes(jnp.exp(acum))
        w_state_p = _two_terms_on_lanes(dt * jnp.exp(a_last - acum))

        for g in range(SSM_GROUPS):
            c0 = g * SSM_GROUP_WIDTH
            ex = expand_ref[:, c0:c0 + SSM_GROUP_WIDTH]
            xs = u_ref[rows, c0:c0 + SSM_GROUP_WIDTH]
            b0 = SSM_INNER + g * SSM_STATE
            bm = u_ref[rows, b0:b0 + SSM_STATE].astype(BF16)
            cm = u_ref[rows, b0 + SSM_GROUPS * SSM_STATE:b0 + (SSM_GROUPS + 1) * SSM_STATE].astype(BF16)
            dec_out_x = _dot(dec_out_p, ex)
            w_state_x = _dot(w_state_p, ex)
            xs_b = xs.astype(BF16)

            scores = _dot_nt(cm, bm)
            state = st_ref[g]
            y_off = _dot(cm, state.astype(BF16)) * dec_out_x
            st_ref[g] = state * dec_out_x[L - 1:L, :] + _dot_tn(bm, (xs * w_state_x).astype(BF16))

            def mix_of(h):
                seg = acum[:, h:h + 1] - acum_t[h:h + 1, :]
                return (scores * jnp.exp(jnp.where(tri, seg, -jnp.inf)) * dt_t[h:h + 1, :]).astype(BF16)

            for jp in range(SSM_GROUP_WIDTH // LANES):
                h = g * (SSM_GROUP_WIDTH // SSM_HEAD_DIM) + 2 * jp
                xp = xs_b[:, jp * LANES:(jp + 1) * LANES]
                x2 = jnp.concatenate([jnp.where(head_a, xp, 0.0), jnp.where(head_a, 0.0, xp)], axis=0)
                yd_ref[ci, :, jp * LANES:(jp + 1) * LANES] = _dot(
                    jnp.concatenate([mix_of(h), mix_of(h + 1)], axis=1), x2)

            y = yd_ref[ci] + y_off + dskip_ref[:, c0:c0 + SSM_GROUP_WIDTH] * xs
            z = z_ref[rows, c0:c0 + SSM_GROUP_WIDTH]
            y = y * (z * _sigmoid(z))
            ms = jnp.mean(y * y, axis=-1, keepdims=True)
            y_ref[rows, c0:c0 + SSM_GROUP_WIDTH] = (
                y * lax.rsqrt(ms + EPS) * gain_ref[:, c0:c0 + SSM_GROUP_WIDTH]).astype(BF16)


def _ssd_branch(u, zg, rw, batch, seq, dt_bias, a_log, dskip_x, norm_gain, expand):
    L = SSD_CHUNK * SSD_CHUNKS_PER_STEP
    nc = seq // L
    row = lambda b, c: b * nc + c
    const = lambda shape: pl.BlockSpec(shape, lambda b, c: (0, 0))
    return pl.pallas_call(
        _ssd_kernel,
        grid=(batch, nc),
        in_specs=[
            pl.BlockSpec((L, SSM_XBC), lambda b, c: (row(b, c), 0)),
            pl.BlockSpec((L, SSM_INNER), lambda b, c: (row(b, c), 0)),
            pl.BlockSpec((L, SMALL_WIDTH), lambda b, c: (row(b, c), 4 * RWKV_DIM // SMALL_WIDTH)),
            const((1, LANES)),
            const((1, LANES)),
            const((1, SSM_INNER)),
            const((1, SSM_INNER)),
            const((LANES, SSM_INNER)),
        ],
        out_specs=pl.BlockSpec((L, SSM_INNER), lambda b, c: (row(b, c), 0)),
        out_shape=jax.ShapeDtypeStruct((batch * seq, SSM_INNER), BF16),
        scratch_shapes=[
            pltpu.VMEM((SSM_GROUPS, SSM_STATE, SSM_GROUP_WIDTH), F32),
            pltpu.VMEM((SSD_CHUNKS_PER_STEP, SSD_CHUNK, SSM_GROUP_WIDTH), F32),
        ],
        compiler_params=pltpu.CompilerParams(
            dimension_semantics=("arbitrary", "arbitrary"), vmem_limit_bytes=VMEM_LIMIT),
        name="ssd_branch",
    )(u, zg, rw, dt_bias, a_log, dskip_x, norm_gain, expand)


def _rwkv_kernel(r_ref, k_ref, v_ref, g_ref, sm_ref, w0_ref, w2_ref, a0_ref, a2_ref, kks_ref, kas_ref,
                 rk_ref, gng_ref, gnb_ref, y_ref, s_ref):
    C = RWKV_CHUNK
    D = RWKV_HEAD_DIM
    NB = RWKV_BATCH_ROWS
    CP = RWKV_CHUNKS_PER_STEP
    R = NB * CP * C

    @pl.when(pl.program_id(1) == 0)
    def _():
        s_ref[...] = jnp.zeros_like(s_ref)

    rows = lambda x_ref: x_ref[...].reshape(R, x_ref.shape[-1])
    r, k, v, gate, sm = rows(r_ref), rows(k_ref), rows(v_ref), rows(g_ref), rows(sm_ref)
    wd = sm[:, 0:LORA]
    ad = sm[:, LORA:2 * LORA]

    lw = _dot(jnp.tanh(wd).astype(BF16), w2_ref[...])
    logw = (-DECAY_SCALE) * _sigmoid(w0_ref[...] + lw)
    neg_a = 1.0 / (-1.0 - jnp.exp(-(a0_ref[...] + _dot(ad.astype(BF16), a2_ref[...]))))
    kkp = k * kks_ref[...]
    k2 = k * (1.0 - (neg_a + 1.0) * kas_ref[...])

    rr = lax.broadcasted_iota(jnp.int32, (R, R), 0)
    cc = lax.broadcasted_iota(jnp.int32, (R, R), 1)
    same_seq = 1 - jnp.clip(jnp.abs((rr // C) - (cc // C)), 0, 1)
    tri = (jnp.clip(rr - cc + 1, 0, 1) * same_seq).astype(F32).astype(BF16)
    cum = _dot_left_exact(tri, logw)
    e_pos = jnp.exp(cum)
    e_neg = 1.0 / e_pos
    kc_p = kkp * jnp.exp(cum - logw)
    rt = r * e_pos
    kt = k2 * e_neg
    nbt_p = kkp * neg_a * e_neg
    gamma_end = [jnp.exp(cum[(q + 1) * C - 1:(q + 1) * C, :]) for q in range(NB * CP)]
    rkk = r * k2 * rk_ref[...]
    gsilu = gate * _sigmoid(gate)

    P = 2 * D
    lane_a = lax.broadcasted_iota(jnp.int32, (C, P), 1) < D
    lane_a2 = lax.broadcasted_iota(jnp.int32, (2 * C, P), 1) < D
    row2 = lax.broadcasted_iota(jnp.int32, (2 * C, 2 * C), 0)
    col2 = lax.broadcasted_iota(jnp.int32, (2 * C, 2 * C), 1)
    colm = jnp.where(col2 >= C, col2 - C, col2)
    gmask = colm < jnp.where(row2 >= C, row2 - C + 1, row2)
    diag_blk = jnp.where(row2 >= C, 1, 0) == jnp.where(col2 >= C, 1, 0)
    anti_eye = jnp.where(col2 == jnp.where(row2 >= C, row2 - C, row2 + C), 1.0, 0.0)

    def seg_sum(x):
        sa = jnp.sum(jnp.where(lane_a, x, 0.0), axis=-1, keepdims=True)
        sb = jnp.sum(jnp.where(lane_a, 0.0, x), axis=-1, keepdims=True)
        return jnp.where(lane_a, sa, sb)

    def cross(lo, hi):
        return jnp.concatenate([lo, hi], axis=0).astype(BF16)

    units = [(b, j) for b in range(NB) for j in range(RWKV_HEADS // 2)]
    n_units = range(len(units))

    def chunk_chain(ci):
        idx = [(slice((b * CP + ci) * C, (b * CP + ci + 1) * C), slice(j * P, (j + 1) * P)) for b, j in units]
        xb, wa, g_top, g_bot, v_p = [], [], [], [], []
        for rs, sl in idx:
            kkp_p = kkp[rs, sl]
            rinv = 1.0 / jnp.maximum(jnp.sqrt(seg_sum(kkp_p * kkp_p)), 1e-12)
            nbt = nbt_p[rs, sl] * rinv
            xp = jnp.concatenate([kc_p[rs, sl] * rinv, rt[rs, sl]], axis=0)
            w_a = jnp.concatenate([nbt, kt[rs, sl]], axis=0).astype(BF16)
            w_b = jnp.concatenate([kt[rs, sl], nbt], axis=0).astype(BF16)
            g_a = jnp.where(gmask, _dot_nt(jnp.where(lane_a2, xp, 0.0).astype(BF16), w_a), 0.0)
            g_b = jnp.where(gmask, _dot_nt(jnp.where(lane_a2, 0.0, xp).astype(BF16), w_b), 0.0)
            xb.append(xp.astype(BF16))
            wa.append(w_a)
            g_top.append(jnp.concatenate([g_a[:C], g_b[:C]], axis=0))
            g_bot.append(jnp.concatenate([g_a[C:], g_b[C:]], axis=0))
            v_p.append(v[rs, sl])

        z = [_dot_nt(xb[i], s_ref[units[i][0], units[i][1]].astype(BF16)) for i in n_units]
        w1 = []
        for i in n_units:
            rw = _dot(g_top[i].astype(BF16),
                      cross(jnp.where(lane_a, 0.0, v_p[i]), jnp.where(lane_a, v_p[i], 0.0)))
            w1.append(jnp.where(lane_a, rw[:C], rw[C:]))

        pt = [jnp.where(diag_blk, g_top[i], anti_eye) for i in n_units]
        for _ in range(6):
            pt = [_dot(jnp.where(diag_blk, pt[i], 0.0).astype(BF16), pt[i].astype(BF16))
                  + jnp.where(diag_blk, 0.0, pt[i]) for i in n_units]

        u = []
        for i in n_units:
            rhs = z[i][:C] + w1[i]
            ru = _dot(pt[i].astype(BF16), cross(jnp.where(lane_a, 0.0, rhs), jnp.where(lane_a, rhs, 0.0)))
            u.append(ru[:C] + ru[C:])
        for i in n_units:
            (b, j), (rs, sl) = units[i], idx[i]
            ry = _dot(g_bot[i].astype(BF16),
                      cross(jnp.where(lane_a, u[i], v_p[i]), jnp.where(lane_a, v_p[i], u[i])))
            y = z[i][C:] + jnp.where(lane_a, ry[:C], ry[C:])
            upd = _dot_tn(jnp.concatenate([u[i], v_p[i]], axis=0).astype(BF16), wa[i])
            s_ref[b, j] = (s_ref[b, j] + jnp.where(diag_blk, upd, 0.0)) * gamma_end[b * CP + ci][:, sl]

            yc = y - seg_sum(y) * (1.0 / D)
            var = seg_sum(yc * yc) * (1.0 / D)
            yn = yc * lax.rsqrt(var + GN_EPS) * gng_ref[:, sl] + gnb_ref[:, sl]
            bonus = seg_sum(rkk[rs, sl]) * v_p[i]
            y_ref[b, ci * C:(ci + 1) * C, sl] = ((yn + bonus) * gsilu[rs, sl]).astype(BF16)

    for ci in range(CP):
        chunk_chain(ci)


def _rwkv_branch(rw, batch, seq, w0, w2, a0, a2, k_k, k_a, r_k, gn_gain, gn_bias):
    C = RWKV_CHUNK
    NB = RWKV_BATCH_ROWS
    CP = RWKV_CHUNKS_PER_STEP
    rw3 = rw.reshape(batch, seq, RW_COLS)
    const = lambda shape: pl.BlockSpec(shape, lambda b, c: (0, 0))
    col_block = lambda j: pl.BlockSpec((NB, CP * C, RWKV_DIM), lambda b, c: (b, c, j))
    return pl.pallas_call(
        _rwkv_kernel,
        grid=(batch // NB, seq // (CP * C)),
        in_specs=[
            col_block(0), col_block(1), col_block(2), col_block(3),
            pl.BlockSpec((NB, CP * C, SMALL_WIDTH), lambda b, c: (b, c, 4 * RWKV_DIM // SMALL_WIDTH)),
            const((1, RWKV_DIM)), const((LORA, RWKV_DIM)), const((1, RWKV_DIM)), const((LORA, RWKV_DIM)),
            const((1, RWKV_DIM)), const((1, RWKV_DIM)), const((1, RWKV_DIM)),
            const((1, RWKV_DIM)), const((1, RWKV_DIM)),
        ],
        out_specs=pl.BlockSpec((NB, CP * C, RWKV_DIM), lambda b, c: (b, c, 0)),
        out_shape=jax.ShapeDtypeStruct((batch, seq, RWKV_DIM), BF16),
        scratch_shapes=[
            pltpu.VMEM((NB, RWKV_HEADS // 2, 2 * RWKV_HEAD_DIM, 2 * RWKV_HEAD_DIM), F32),
        ],
        compiler_params=pltpu.CompilerParams(
            dimension_semantics=("arbitrary", "arbitrary"), vmem_limit_bytes=VMEM_LIMIT),
        name="rwkv_branch",
    )(rw3, rw3, rw3, rw3, rw3, w0, w2, a0, a2, k_k, k_a, r_k, gn_gain, gn_bias
      ).reshape(batch * seq, RWKV_DIM)


def _merge_kernel(x_ref, ys_ref, yr_ref, g0_ref, g1_ref, bg_ref, ws_ref, wr_ref, wo_ref, gain_ref, o_ref):
    g_ssm = _sigmoid(g0_ref[...] + bg_ref[:, 0:D_MODEL])
    g_rwkv = _sigmoid(g1_ref[...] + bg_ref[:, D_MODEL:2 * D_MODEL])
    merged = g_ssm * _dot(ys_ref[...], ws_ref[...]) + g_rwkv * _dot(yr_ref[...], wr_ref[...])
    out = _dot(merged.astype(BF16), wo_ref[...])
    ms = jnp.mean(out * out, axis=-1, keepdims=True)
    o_ref[...] = x_ref[...] + out * lax.rsqrt(ms + EPS) * gain_ref[...]


def _merge(x2d, y_ssm, y_rwkv, zg, b_gate, w_ssm, w_rwkv, w_out, post_gain):
    t = x2d.shape[0]
    tm = MERGE_TM
    const = lambda shape: pl.BlockSpec(shape, lambda i: (0, 0), pipeline_mode=pl.Buffered(1))
    return pl.pallas_call(
        _merge_kernel,
        grid=(t // tm,),
        in_specs=[
            pl.BlockSpec((tm, D_MODEL), lambda i: (i, 0)),
            pl.BlockSpec((tm, SSM_INNER), lambda i: (i, 0)),
            pl.BlockSpec((tm, RWKV_DIM), lambda i: (i, 0)),
            pl.BlockSpec((tm, D_MODEL), lambda i: (i, SSM_INNER // D_MODEL)),
            pl.BlockSpec((tm, D_MODEL), lambda i: (i, SSM_INNER // D_MODEL + 1)),
            const((1, 2 * D_MODEL)),
            const((SSM_INNER, D_MODEL)), const((RWKV_DIM, D_MODEL)), const((D_MODEL, D_MODEL)),
            const((1, D_MODEL)),
        ],
        out_specs=pl.BlockSpec((tm, D_MODEL), lambda i: (i, 0)),
        out_shape=jax.ShapeDtypeStruct((t, D_MODEL), F32),
        compiler_params=pltpu.CompilerParams(
            dimension_semantics=("arbitrary",), vmem_limit_bytes=VMEM_LIMIT),
        name="gated_merge",
    )(x2d, y_ssm, y_rwkv, zg, zg, b_gate, w_ssm, w_rwkv, w_out, post_gain)


def _pad_lanes(v, width):
    return jnp.pad(v, ((0, 0), (0, width - v.shape[-1])))


def _layer(x, pre_gain, w_in, b_gate, conv_w, conv_b, dt_bias, a_log, d_skip, ssm_norm_gain,
           rwkv_mu, decay_w0, decay_w2, iclr_a0, iclr_a2, k_k, k_a, r_k, gn_gain, gn_bias,
           w_branch_ssm, w_branch_rwkv, w_out, post_gain):
    batch, seq, _ = x.shape
    x2d = x.reshape(batch * seq, D_MODEL)
    row = lambda v: v.reshape(1, -1)

    o_xbc, o_dt, o_rw = SSM_INNER, SSM_INNER + SSM_XBC, SSM_INNER + SSM_XBC + SSM_HEADS
    o_lora, o_gate = o_rw + 4 * RWKV_DIM, o_rw + 4 * RWKV_DIM + 2 * LORA
    pad = SMALL_WIDTH - 2 * LORA - SSM_HEADS
    w_xbc = w_in[:, o_xbc:o_dt].astype(BF16)
    w_rw = jnp.concatenate([w_in[:, o_rw:o_gate], w_in[:, o_dt:o_rw], jnp.zeros((D_MODEL, pad + RW_PAD), w_in.dtype)],
                           axis=1).astype(BF16)
    w_zg = jnp.concatenate([w_in[:, :o_xbc], w_in[:, o_gate:]], axis=1).astype(BF16)
    mu = _pad_lanes(row(rwkv_mu), RW_COLS)

    gain = row(pre_gain)
    u, h = _project_normed(_proj_conv_kernel, seq, x2d, gain, w_xbc, [conv_w, row(conv_b)])
    rw = _project(_proj_shift_kernel, seq, h, w_rw, [mu], True)
    zg = _project(_proj_plain_kernel, seq, h, w_zg, [], False)

    head_of_col = jnp.arange(SSM_INNER) // SSM_HEAD_DIM
    expand = ((jnp.arange(LANES)[:, None] % SSM_HEADS == head_of_col[None, :])
              & (jnp.arange(LANES)[:, None] < 2 * SSM_HEADS)).astype(BF16)
    y_ssm = _ssd_branch(
        u, zg, rw, batch, seq, _pad_lanes(row(dt_bias), LANES), _pad_lanes(row(a_log), LANES),
        row(jnp.repeat(d_skip, SSM_HEAD_DIM)), row(ssm_norm_gain), expand)

    y_rwkv = _rwkv_branch(
        rw, batch, seq, row(decay_w0), decay_w2.astype(BF16), row(iclr_a0), iclr_a2.astype(BF16),
        row(k_k), row(k_a), row(r_k), row(gn_gain), row(gn_bias))

    out = _merge(x2d, y_ssm, y_rwkv, zg, row(b_gate), w_branch_ssm.astype(BF16),
                 w_branch_rwkv.astype(BF16), w_out.astype(BF16), row(post_gain))
    return out.reshape(batch, seq, D_MODEL)


def kernel(x, pre_gain, w_in, b_gate, conv_w, conv_b, dt_bias, a_log, d_skip, ssm_norm_gain, rwkv_mu,
           decay_w0, decay_w2, iclr_a0, iclr_a2, k_k, k_a, r_k, gn_gain, gn_bias, w_branch_ssm,
           w_branch_rwkv, w_out, post_gain):
    for layer in range(pre_gain.shape[0]):
        x = _layer(
            x, pre_gain[layer], w_in[layer], b_gate[layer], conv_w[layer], conv_b[layer], dt_bias[layer],
            a_log[layer], d_skip[layer], ssm_norm_gain[layer], rwkv_mu[layer], decay_w0[layer],
            decay_w2[layer], iclr_a0[layer], iclr_a2[layer], k_k[layer], k_a[layer], r_k[layer],
            gn_gain[layer], gn_bias[layer], w_branch_ssm[layer], w_branch_rwkv[layer], w_out[layer],
            post_gain[layer])
    return x
```

```python
import functools

import jax
import jax.numpy as jnp
from jax import lax
from jax.experimental import pallas as pl
from jax.experimental.pallas import tpu as pltpu

D_MODEL = 1024
EPS = 1e-6

SSM_INNER = 2048
SSM_HEAD_DIM = 64
SSM_HEADS = 32
SSM_STATE = 128
SSM_GROUPS = 4
SSM_GROUP_WIDTH = SSM_INNER // SSM_GROUPS
CONV_WIDTH = 4
SSM_XBC = 3072

RWKV_DIM = 1024
RWKV_HEAD_DIM = 64
RWKV_HEADS = 16
LORA = 64
GN_EPS = RWKV_HEAD_DIM * 1e-5
DECAY_SCALE = 0.6065306597126334

LANES = 128
SUBLANES = 8

SMALL_WIDTH = 256
RW_PAD = 256
RW_COLS = 4 * RWKV_DIM + SMALL_WIDTH + RW_PAD
ZG_COLS = SSM_INNER + 2 * D_MODEL

PROJ_TM = {SSM_XBC: 1024, RW_COLS: 2048, ZG_COLS: 2048}
PROJ_SUBTILE = 256
PROJ_TN = {SSM_XBC: 1024, RW_COLS: RW_COLS // 6, ZG_COLS: 1024}
SSD_CHUNK = 128
SSD_CHUNKS_PER_STEP = 2
RWKV_CHUNK = 64
RWKV_BATCH_ROWS = 2
RWKV_CHUNKS_PER_STEP = 2
MERGE_TM = 512
VMEM_LIMIT = 48 * 1024 * 1024

BF16 = jnp.bfloat16
F32 = jnp.float32


def _dot(a, b):
    return jnp.dot(a, b, preferred_element_type=F32)


def _dot_nt(a, b):
    return lax.dot_general(a, b, (((1,), (1,)), ((), ())), preferred_element_type=F32)


def _dot_tn(a, b):
    return lax.dot_general(a, b, (((0,), (0,)), ((), ())), preferred_element_type=F32)


def _split_terms(x, n):
    terms = []
    rem = x
    for _ in range(n):
        t = rem.astype(BF16)
        terms.append(t)
        rem = rem - t.astype(F32)
    return terms


def _dot_left_exact(m_bf16, x, n=3):
    acc = None
    for t in _split_terms(x, n):
        p = _dot(m_bf16, t)
        acc = p if acc is None else acc + p
    return acc


def _two_terms_on_lanes(x):
    lane = lax.broadcasted_iota(jnp.int32, x.shape, 1)
    xm = jnp.where(lane < SSM_HEADS, x, 0.0)
    hi = xm.astype(BF16).astype(F32)
    return (hi + pltpu.roll(xm - hi, SSM_HEADS, 1)).astype(BF16)


def _softplus(x):
    return jnp.maximum(x, 0.0) + jnp.log(1.0 + jnp.exp(-jnp.abs(x)))


def _sigmoid(x):
    return 0.5 * jnp.tanh(0.5 * x) + 0.5


def _silu(x):
    h = 0.5 * x
    return h + h * jnp.tanh(h)


def _lower_tri(n, strict):
    row = lax.broadcasted_iota(jnp.int32, (n, n), 0)
    col = lax.broadcasted_iota(jnp.int32, (n, n), 1)
    return (col < row) if strict else (col <= row)


def _lower_tri_ones(n):
    row = lax.broadcasted_iota(jnp.int32, (n, n), 0)
    col = lax.broadcasted_iota(jnp.int32, (n, n), 1)
    return jnp.clip(row - col + 1, 0, 1).astype(F32).astype(BF16)


def _normed_input(x_ref, gain_ref, h_ref):
    @pl.when(pl.program_id(1) == 0)
    def _():
        x = x_ref[...]
        ms = jnp.mean(x * x, axis=-1, keepdims=True)
        h_ref[...] = (x * lax.rsqrt(ms + EPS) * gain_ref[...]).astype(BF16)


def _delayed(o, halo, k):
    head = jnp.concatenate([halo, o[:SUBLANES]], axis=0)[SUBLANES - k:2 * SUBLANES - k]
    if o.shape[0] == SUBLANES:
        return head
    return jnp.concatenate([head, pltpu.roll(o, k, 0)[SUBLANES:]], axis=0)


def _zero_carry_at_sequence_start(carry_ref, row_blocks_per_seq):
    i, j = pl.program_id(0), pl.program_id(1)

    @pl.when(i % row_blocks_per_seq == 0)
    def _():
        carry_ref[j] = jnp.zeros(carry_ref.shape[1:], F32)


def _proj_plain_kernel(h_ref, w_ref, o_ref):
    o_ref[...] = _dot(h_ref[...], w_ref[...])


def _proj_conv_kernel(row_blocks_per_seq, x_ref, gain_ref, w_ref, cw_ref, cb_ref, o_ref, h_ref, carry_ref):
    _normed_input(x_ref, gain_ref, h_ref)
    _zero_carry_at_sequence_start(carry_ref, row_blocks_per_seq)
    j = pl.program_id(1)
    for c0 in range(0, o_ref.shape[1], PROJ_SUBTILE):
        cs = slice(c0, c0 + PROJ_SUBTILE)
        o = _dot(h_ref[...], w_ref[:, cs])
        halo = carry_ref[j, :, cs]
        carry_ref[j, :, cs] = o[o.shape[0] - SUBLANES:, :]
        e = _delayed(o, halo, 1)
        e_halo = _delayed(halo, jnp.zeros_like(halo), 1)
        inner = cw_ref[1:2, cs] * o + cw_ref[0:1, cs] * e
        inner_halo = cw_ref[1:2, cs] * halo + cw_ref[0:1, cs] * e_halo
        acc = cb_ref[:, cs] + cw_ref[3:4, cs] * o + cw_ref[2:3, cs] * e + _delayed(inner, inner_halo, 2)
        o_ref[:, cs] = _silu(acc)


def _proj_shift_kernel(row_blocks_per_seq, h_ref, w_ref, mu_ref, o_ref, carry_ref):
    _zero_carry_at_sequence_start(carry_ref, row_blocks_per_seq)
    j = pl.program_id(1)
    for c0 in range(0, o_ref.shape[1], PROJ_SUBTILE):
        cs = slice(c0, c0 + PROJ_SUBTILE)
        o = _dot(h_ref[...], w_ref[:, cs])
        halo = carry_ref[j, :, cs]
        carry_ref[j, :, cs] = o[o.shape[0] - SUBLANES:, :]
        o_ref[:, cs] = o + mu_ref[:, cs] * (_delayed(o, halo, 1) - o)


def _project_normed(kernel_fn, seq, x2d, pre_gain, w, col_params):
    t, n = x2d.shape[0], w.shape[1]
    tm, tn = PROJ_TM[n], PROJ_TN[n]
    return pl.pallas_call(
        functools.partial(kernel_fn, seq // tm),
        grid=(t // tm, n // tn),
        in_specs=[
            pl.BlockSpec((tm, D_MODEL), lambda i, j: (i, 0)),
            pl.BlockSpec((1, D_MODEL), lambda i, j: (0, 0)),
            pl.BlockSpec((D_MODEL, tn), lambda i, j: (0, j)),
        ] + [pl.BlockSpec((c.shape[0], tn), lambda i, j: (0, j)) for c in col_params],
        out_specs=[pl.BlockSpec((tm, tn), lambda i, j: (i, j)),
                   pl.BlockSpec((tm, D_MODEL), lambda i, j: (i, 0))],
        out_shape=[jax.ShapeDtypeStruct((t, n), F32), jax.ShapeDtypeStruct((t, D_MODEL), BF16)],
        scratch_shapes=[pltpu.VMEM((n // tn, SUBLANES, tn), F32)],
        compiler_params=pltpu.CompilerParams(
            dimension_semantics=("arbitrary", "arbitrary"), vmem_limit_bytes=VMEM_LIMIT),
        name="in_proj_%d" % n,
    )(x2d, pre_gain, w, *col_params)


def _project(kernel_fn, seq, h, w, col_params, carry):
    t, n = h.shape[0], w.shape[1]
    tm, tn = PROJ_TM[n], PROJ_TN[n]
    return pl.pallas_call(
        functools.partial(kernel_fn, seq // tm) if carry else kernel_fn,
        grid=(t // tm, n // tn),
        in_specs=[
            pl.BlockSpec((tm, D_MODEL), lambda i, j: (i, 0)),
            pl.BlockSpec((D_MODEL, tn), lambda i, j: (0, j)),
        ] + [pl.BlockSpec((c.shape[0], tn), lambda i, j: (0, j)) for c in col_params],
        out_specs=pl.BlockSpec((tm, tn), lambda i, j: (i, j)),
        out_shape=jax.ShapeDtypeStruct((t, n), F32),
        scratch_shapes=[pltpu.VMEM((n // tn, SUBLANES, tn), F32)] if carry else [],
        compiler_params=pltpu.CompilerParams(
            dimension_semantics=("arbitrary", "arbitrary"), vmem_limit_bytes=VMEM_LIMIT),
        name="in_proj_%d" % n,
    )(h, w, *col_params)


def _ssd_kernel(u_ref, z_ref, sm_ref, dtb_ref, alog_ref, dskip_ref, gain_ref, expand_ref,
                y_ref, st_ref, yd_ref):
    L = SSD_CHUNK

    @pl.when(pl.program_id(1) == 0)
    def _():
        st_ref[...] = jnp.zeros_like(st_ref)

    tri = _lower_tri(L, strict=False)
    tri_ones = _lower_tri_ones(L)
    head_a = lax.broadcasted_iota(jnp.int32, (L, LANES), 1) < SSM_HEAD_DIM
    for ci in range(SSD_CHUNKS_PER_STEP):
        rows = slice(ci * L, (ci + 1) * L)
        dt = _softplus(sm_ref[rows, LANES:2 * LANES] + dtb_ref[...])
        a = dt * (-jnp.exp(alog_ref[...]))
        acum = _dot_left_exact(tri_ones, a)
        acum_t = acum.T
        dt_t = dt.T
        a_last = acum[L - 1:L, :]
        dec_out_p = _two_terms_on_lanes(jnp.exp(acum))
        w_state_p = _two_terms_on_lanes(dt * jnp.exp(a_last - acum))

        for g in range(SSM_GROUPS):
            c0 = g * SSM_GROUP_WIDTH
            ex = expand_ref[:, c0:c0 + SSM_GROUP_WIDTH]
            xs = u_ref[rows, c0:c0 + SSM_GROUP_WIDTH]
            b0 = SSM_INNER + g * SSM_STATE
            bm = u_ref[rows, b0:b0 + SSM_STATE].astype(BF16)
            cm = u_ref[rows, b0 + SSM_GROUPS * SSM_STATE:b0 + (SSM_GROUPS + 1) * SSM_STATE].astype(BF16)
            dec_out_x = _dot(dec_out_p, ex)
            w_state_x = _dot(w_state_p, ex)
            xs_b = xs.astype(BF16)

            scores = _dot_nt(cm, bm)
            state = st_ref[g]
            y_off = _dot(cm, state.astype(BF16)) * dec_out_x
            st_ref[g] = state * dec_out_x[L - 1:L, :] + _dot_tn(bm, (xs * w_state_x).astype(BF16))

            def mix_of(h):
                seg = acum[:, h:h + 1] - acum_t[h:h + 1, :]
                return (scores * jnp.exp(jnp.where(tri, seg, -jnp.inf)) * dt_t[h:h + 1, :]).astype(BF16)

            for jp in range(SSM_GROUP_WIDTH // LANES):
                h = g * (SSM_GROUP_WIDTH // SSM_HEAD_DIM) + 2 * jp
                xp = xs_b[:, jp * LANES:(jp + 1) * LANES]
                x2 = jnp.concatenate([jnp.where(head_a, xp, 0.0), jnp.where(head_a, 0.0, xp)], axis=0)
                yd_ref[ci, :, jp * LANES:(jp + 1) * LANES] = _dot(
                    jnp.concatenate([mix_of(h), mix_of(h + 1)], axis=1), x2)

            y = yd_ref[ci] + y_off + dskip_ref[:, c0:c0 + SSM_GROUP_WIDTH] * xs
            z = z_ref[rows, c0:c0 + SSM_GROUP_WIDTH]
            y = y * _silu(z)
            ms = jnp.mean(y * y, axis=-1, keepdims=True)
            y_ref[rows, c0:c0 + SSM_GROUP_WIDTH] = (
                y * lax.rsqrt(ms + EPS) * gain_ref[:, c0:c0 + SSM_GROUP_WIDTH]).astype(BF16)


def _ssd_branch(u, zg, rw, batch, seq, dt_bias, a_log, dskip_x, norm_gain, expand):
    L = SSD_CHUNK * SSD_CHUNKS_PER_STEP
    nc = seq // L
    row = lambda b, c: b * nc + c
    const = lambda shape: pl.BlockSpec(shape, lambda b, c: (0, 0))
    return pl.pallas_call(
        _ssd_kernel,
        grid=(batch, nc),
        in_specs=[
            pl.BlockSpec((L, SSM_XBC), lambda b, c: (row(b, c), 0)),
            pl.BlockSpec((L, SSM_INNER), lambda b, c: (row(b, c), 0)),
            pl.BlockSpec((L, SMALL_WIDTH), lambda b, c: (row(b, c), 4 * RWKV_DIM // SMALL_WIDTH)),
            const((1, LANES)),
            const((1, LANES)),
            const((1, SSM_INNER)),
            const((1, SSM_INNER)),
            const((LANES, SSM_INNER)),
        ],
        out_specs=pl.BlockSpec((L, SSM_INNER), lambda b, c: (row(b, c), 0)),
        out_shape=jax.ShapeDtypeStruct((batch * seq, SSM_INNER), BF16),
        scratch_shapes=[
            pltpu.VMEM((SSM_GROUPS, SSM_STATE, SSM_GROUP_WIDTH), F32),
            pltpu.VMEM((SSD_CHUNKS_PER_STEP, SSD_CHUNK, SSM_GROUP_WIDTH), F32),
        ],
        compiler_params=pltpu.CompilerParams(
            dimension_semantics=("arbitrary", "arbitrary"), vmem_limit_bytes=VMEM_LIMIT),
        name="ssd_branch",
    )(u, zg, rw, dt_bias, a_log, dskip_x, norm_gain, expand)


def _rwkv_kernel(r_ref, k_ref, v_ref, g_ref, sm_ref, w0_ref, w2_ref, a0_ref, a2_ref, kks_ref, kas_ref,
                 rk_ref, gng_ref, gnb_ref, y_ref, s_ref):
    C = RWKV_CHUNK
    D = RWKV_HEAD_DIM
    NB = RWKV_BATCH_ROWS
    CP = RWKV_CHUNKS_PER_STEP
    R = NB * CP * C

    @pl.when(pl.program_id(1) == 0)
    def _():
        s_ref[...] = jnp.zeros_like(s_ref)

    rows = lambda x_ref: x_ref[...].reshape(R, x_ref.shape[-1])
    r, k, v, gate, sm = rows(r_ref), rows(k_ref), rows(v_ref), rows(g_ref), rows(sm_ref)
    wd = sm[:, 0:LORA]
    ad = sm[:, LORA:2 * LORA]

    lw = _dot(jnp.tanh(wd).astype(BF16), w2_ref[...])
    logw = (-DECAY_SCALE) * _sigmoid(w0_ref[...] + lw)
    neg_a = -_sigmoid(a0_ref[...] + _dot(ad.astype(BF16), a2_ref[...]))
    kkp = k * kks_ref[...]
    k2 = k * (1.0 - (neg_a + 1.0) * kas_ref[...])

    rr = lax.broadcasted_iota(jnp.int32, (R, R), 0)
    cc = lax.broadcasted_iota(jnp.int32, (R, R), 1)
    same_seq = 1 - jnp.clip(jnp.abs((rr // C) - (cc // C)), 0, 1)
    tri = (jnp.clip(rr - cc + 1, 0, 1) * same_seq).astype(F32).astype(BF16)
    cum = _dot_left_exact(tri, logw)
    e_pos = jnp.exp(cum)
    e_neg = 1.0 / e_pos
    kc_p = kkp * jnp.exp(cum - logw)
    rt = r * e_pos
    kt = k2 * e_neg
    nbt_p = kkp * neg_a * e_neg
    gamma_end = [jnp.exp(cum[(q + 1) * C - 1:(q + 1) * C, :]) for q in range(NB * CP)]
    rkk = r * k2 * rk_ref[...]
    gsilu = _silu(gate)

    P = 2 * D
    lane_a = lax.broadcasted_iota(jnp.int32, (C, P), 1) < D
    lane_a2 = lax.broadcasted_iota(jnp.int32, (2 * C, P), 1) < D
    row2 = lax.broadcasted_iota(jnp.int32, (2 * C, 2 * C), 0)
    col2 = lax.broadcasted_iota(jnp.int32, (2 * C, 2 * C), 1)
    colm = jnp.where(col2 >= C, col2 - C, col2)
    gmask = colm < jnp.where(row2 >= C, row2 - C + 1, row2)
    diag_blk = jnp.where(row2 >= C, 1, 0) == jnp.where(col2 >= C, 1, 0)
    anti_eye = jnp.where(col2 == jnp.where(row2 >= C, row2 - C, row2 + C), 1.0, 0.0)

    def seg_sum(x):
        sa = jnp.sum(jnp.where(lane_a, x, 0.0), axis=-1, keepdims=True)
        sb = jnp.sum(jnp.where(lane_a, 0.0, x), axis=-1, keepdims=True)
        return jnp.where(lane_a, sa, sb)

    def cross(lo, hi):
        return jnp.concatenate([lo, hi], axis=0).astype(BF16)

    units = [(b, j) for b in range(NB) for j in range(RWKV_HEADS // 2)]
    n_units = range(len(units))

    def chunk_chain(ci):
        idx = [(slice((b * CP + ci) * C, (b * CP + ci + 1) * C), slice(j * P, (j + 1) * P)) for b, j in units]
        xb, wa, g_top, g_bot, v_p = [], [], [], [], []
        for rs, sl in idx:
            kkp_p = kkp[rs, sl]
            rinv = 1.0 / jnp.maximum(jnp.sqrt(seg_sum(kkp_p * kkp_p)), 1e-12)
            nbt = nbt_p[rs, sl] * rinv
            xp = jnp.concatenate([kc_p[rs, sl] * rinv, rt[rs, sl]], axis=0)
            w_a = jnp.concatenate([nbt, kt[rs, sl]], axis=0).astype(BF16)
            w_b = jnp.concatenate([kt[rs, sl], nbt], axis=0).astype(BF16)
            g_a = jnp.where(gmask, _dot_nt(jnp.where(lane_a2, xp, 0.0).astype(BF16), w_a), 0.0)
            g_b = jnp.where(gmask, _dot_nt(jnp.where(lane_a2, 0.0, xp).astype(BF16), w_b), 0.0)
            xb.append(xp.astype(BF16))
            wa.append(w_a)
            g_top.append(jnp.concatenate([g_a[:C], g_b[:C]], axis=0))
            g_bot.append(jnp.concatenate([g_a[C:], g_b[C:]], axis=0))
            v_p.append(v[rs, sl])

        z = [_dot_nt(xb[i], s_ref[units[i][0], units[i][1]].astype(BF16)) for i in n_units]
        w1 = []
        for i in n_units:
            rw = _dot(g_top[i].astype(BF16),
                      cross(jnp.where(lane_a, 0.0, v_p[i]), jnp.where(lane_a, v_p[i], 0.0)))
            w1.append(jnp.where(lane_a, rw[:C], rw[C:]))

        pt = [jnp.where(diag_blk, g_top[i], anti_eye) for i in n_units]
        for _ in range(6):
            pt = [_dot(jnp.where(diag_blk, pt[i], 0.0).astype(BF16), pt[i].astype(BF16))
                  + jnp.where(diag_blk, 0.0, pt[i]) for i in n_units]

        u = []
        for i in n_units:
            rhs = z[i][:C] + w1[i]
            ru = _dot(pt[i].astype(BF16), cross(jnp.where(lane_a, 0.0, rhs), jnp.where(lane_a, rhs, 0.0)))
            u.append(ru[:C] + ru[C:])
        for i in n_units:
            (b, j), (rs, sl) = units[i], idx[i]
            ry = _dot(g_bot[i].astype(BF16),
                      cross(jnp.where(lane_a, u[i], v_p[i]), jnp.where(lane_a, v_p[i], u[i])))
            y = z[i][C:] + jnp.where(lane_a, ry[:C], ry[C:])
            upd = _dot_tn(jnp.concatenate([u[i], v_p[i]], axis=0).astype(BF16), wa[i])
            s_ref[b, j] = (s_ref[b, j] + jnp.where(diag_blk, upd, 0.0)) * gamma_end[b * CP + ci][:, sl]

            yc = y - seg_sum(y) * (1.0 / D)
            var = seg_sum(yc * yc) * (1.0 / D)
            yn = yc * lax.rsqrt(var + GN_EPS) * gng_ref[:, sl] + gnb_ref[:, sl]
            bonus = seg_sum(rkk[rs, sl]) * v_p[i]
            y_ref[b, ci * C:(ci + 1) * C, sl] = ((yn + bonus) * gsilu[rs, sl]).astype(BF16)

    for ci in range(CP):
        chunk_chain(ci)


def _rwkv_branch(rw, batch, seq, w0, w2, a0, a2, k_k, k_a, r_k, gn_gain, gn_bias):
    C = RWKV_CHUNK
    NB = RWKV_BATCH_ROWS
    CP = RWKV_CHUNKS_PER_STEP
    rw3 = rw.reshape(batch, seq, RW_COLS)
    const = lambda shape: pl.BlockSpec(shape, lambda b, c: (0, 0))
    col_block = lambda j: pl.BlockSpec((NB, CP * C, RWKV_DIM), lambda b, c: (b, c, j))
    return pl.pallas_call(
        _rwkv_kernel,
        grid=(batch // NB, seq // (CP * C)),
        in_specs=[
            col_block(0), col_block(1), col_block(2), col_block(3),
            pl.BlockSpec((NB, CP * C, SMALL_WIDTH), lambda b, c: (b, c, 4 * RWKV_DIM // SMALL_WIDTH)),
            const((1, RWKV_DIM)), const((LORA, RWKV_DIM)), const((1, RWKV_DIM)), const((LORA, RWKV_DIM)),
            const((1, RWKV_DIM)), const((1, RWKV_DIM)), const((1, RWKV_DIM)),
            const((1, RWKV_DIM)), const((1, RWKV_DIM)),
        ],
        out_specs=pl.BlockSpec((NB, CP * C, RWKV_DIM), lambda b, c: (b, c, 0)),
        out_shape=jax.ShapeDtypeStruct((batch, seq, RWKV_DIM), BF16),
        scratch_shapes=[
            pltpu.VMEM((NB, RWKV_HEADS // 2, 2 * RWKV_HEAD_DIM, 2 * RWKV_HEAD_DIM), F32),
        ],
        compiler_params=pltpu.CompilerParams(
            dimension_semantics=("arbitrary", "arbitrary"), vmem_limit_bytes=VMEM_LIMIT),
        name="rwkv_branch",
    )(rw3, rw3, rw3, rw3, rw3, w0, w2, a0, a2, k_k, k_a, r_k, gn_gain, gn_bias
      ).reshape(batch * seq, RWKV_DIM)


def _merge_kernel(x_ref, ys_ref, yr_ref, g0_ref, g1_ref, bg_ref, ws_ref, wr_ref, wo_ref, gain_ref, o_ref):
    g_ssm = _sigmoid(g0_ref[...] + bg_ref[:, 0:D_MODEL])
    g_rwkv = _sigmoid(g1_ref[...] + bg_ref[:, D_MODEL:2 * D_MODEL])
    merged = g_ssm * _dot(ys_ref[...], ws_ref[...]) + g_rwkv * _dot(yr_ref[...], wr_ref[...])
    out = _dot(merged.astype(BF16), wo_ref[...])
    ms = jnp.mean(out * out, axis=-1, keepdims=True)
    o_ref[...] = x_ref[...] + out * lax.rsqrt(ms + EPS) * gain_ref[...]


def _merge(x2d, y_ssm, y_rwkv, zg, b_gate, w_ssm, w_rwkv, w_out, post_gain):
    t = x2d.shape[0]
    tm = MERGE_TM
    const = lambda shape: pl.BlockSpec(shape, lambda i: (0, 0), pipeline_mode=pl.Buffered(1))
    return pl.pallas_call(
        _merge_kernel,
        grid=(t // tm,),
        in_specs=[
            pl.BlockSpec((tm, D_MODEL), lambda i: (i, 0)),
            pl.BlockSpec((tm, SSM_INNER), lambda i: (i, 0)),
            pl.BlockSpec((tm, RWKV_DIM), lambda i: (i, 0)),
            pl.BlockSpec((tm, D_MODEL), lambda i: (i, SSM_INNER // D_MODEL)),
            pl.BlockSpec((tm, D_MODEL), lambda i: (i, SSM_INNER // D_MODEL + 1)),
            const((1, 2 * D_MODEL)),
            const((SSM_INNER, D_MODEL)), const((RWKV_DIM, D_MODEL)), const((D_MODEL, D_MODEL)),
            const((1, D_MODEL)),
        ],
        out_specs=pl.BlockSpec((tm, D_MODEL), lambda i: (i, 0)),
        out_shape=jax.ShapeDtypeStruct((t, D_MODEL), F32),
        compiler_params=pltpu.CompilerParams(
            dimension_semantics=("arbitrary",), vmem_limit_bytes=VMEM_LIMIT),
        name="gated_merge",
    )(x2d, y_ssm, y_rwkv, zg, zg, b_gate, w_ssm, w_rwkv, w_out, post_gain)


def _pad_lanes(v, width):
    return jnp.pad(v, ((0, 0), (0, width - v.shape[-1])))


def _layer(x, pre_gain, w_in, b_gate, conv_w, conv_b, dt_bias, a_log, d_skip, ssm_norm_gain,
           rwkv_mu, decay_w0, decay_w2, iclr_a0, iclr_a2, k_k, k_a, r_k, gn_gain, gn_bias,
           w_branch_ssm, w_branch_rwkv, w_out, post_gain):
    batch, seq, _ = x.shape
    x2d = x.reshape(batch * seq, D_MODEL)
    row = lambda v: v.reshape(1, -1)

    o_xbc, o_dt, o_rw = SSM_INNER, SSM_INNER + SSM_XBC, SSM_INNER + SSM_XBC + SSM_HEADS
    o_lora, o_gate = o_rw + 4 * RWKV_DIM, o_rw + 4 * RWKV_DIM + 2 * LORA
    pad = SMALL_WIDTH - 2 * LORA - SSM_HEADS
    w_xbc = w_in[:, o_xbc:o_dt].astype(BF16)
    w_rw = jnp.concatenate([w_in[:, o_rw:o_gate], w_in[:, o_dt:o_rw], jnp.zeros((D_MODEL, pad + RW_PAD), w_in.dtype)],
                           axis=1).astype(BF16)
    w_zg = jnp.concatenate([w_in[:, :o_xbc], w_in[:, o_gate:]], axis=1).astype(BF16)
    mu = _pad_lanes(row(rwkv_mu), RW_COLS)

    gain = row(pre_gain)
    u, h = _project_normed(_proj_conv_kernel, seq, x2d, gain, w_xbc, [conv_w, row(conv_b)])
    rw = _project(_proj_shift_kernel, seq, h, w_rw, [mu], True)
    zg = _project(_proj_plain_kernel, seq, h, w_zg, [], False)

    head_of_col = jnp.arange(SSM_INNER) // SSM_HEAD_DIM
    expand = ((jnp.arange(LANES)[:, None] % SSM_HEADS == head_of_col[None, :])
              & (jnp.arange(LANES)[:, None] < 2 * SSM_HEADS)).astype(BF16)
    y_ssm = _ssd_branch(
        u, zg, rw, batch, seq, _pad_lanes(row(dt_bias), LANES), _pad_lanes(row(a_log), LANES),
        row(jnp.repeat(d_skip, SSM_HEAD_DIM)), row(ssm_norm_gain), expand)

    y_rwkv = _rwkv_branch(
        rw, batch, seq, row(decay_w0), decay_w2.astype(BF16), row(iclr_a0), iclr_a2.astype(BF16),
        row(k_k), row(k_a), row(r_k), row(gn_gain), row(gn_bias))

    out = _merge(x2d, y_ssm, y_rwkv, zg, row(b_gate), w_branch_ssm.astype(BF16),
                 w_branch_rwkv.astype(BF16), w_out.astype(BF16), row(post_gain))
    return out.reshape(batch, seq, D_MODEL)


def kernel(x, pre_gain, w_in, b_gate, conv_w, conv_b, dt_bias, a_log, d_skip, ssm_norm_gain, rwkv_mu,
           decay_w0, decay_w2, iclr_a0, iclr_a2, k_k, k_a, r_k, gn_gain, gn_bias, w_branch_ssm,
           w_branch_rwkv, w_out, post_gain):
    for layer in range(pre_gain.shape[0]):
        x = _layer(
            x, pre_gain[layer], w_in[layer], b_gate[layer], conv_w[layer], conv_b[layer], dt_bias[layer],
            a_log[layer], d_skip[layer], ssm_norm_gain[layer], rwkv_mu[layer], decay_w0[layer],
            decay_w2[layer], iclr_a0[layer], iclr_a2[layer], k_k[layer], k_a[layer], r_k[layer],
            gn_gain[layer], gn_bias[layer], w_branch_ssm[layer], w_branch_rwkv[layer], w_out[layer],
            post_gain[layer])
    return x
```

```python
import functools

import jax
import jax.numpy as jnp
from jax import lax
from jax.experimental import pallas as pl
from jax.experimental.pallas import tpu as pltpu

D_MODEL = 1024
EPS = 1e-6

SSM_INNER = 2048
SSM_HEAD_DIM = 64
SSM_HEADS = 32
SSM_STATE = 128
SSM_GROUPS = 4
SSM_GROUP_WIDTH = SSM_INNER // SSM_GROUPS
CONV_WIDTH = 4
SSM_XBC = 3072

RWKV_DIM = 1024
RWKV_HEAD_DIM = 64
RWKV_HEADS = 16
LORA = 64
GN_EPS = RWKV_HEAD_DIM * 1e-5
DECAY_SCALE = 0.6065306597126334

LANES = 128
SUBLANES = 8

SMALL_WIDTH = 256
RW_PAD = 256
RW_COLS = 4 * RWKV_DIM + SMALL_WIDTH + RW_PAD
ZG_COLS = SSM_INNER + 2 * D_MODEL

PROJ_TM = {SSM_XBC: 1024, RW_COLS: 2048, ZG_COLS: 1024}
PROJ_SUBTILE = 256
PROJ_TN = {SSM_XBC: 1024, RW_COLS: RW_COLS // 6, ZG_COLS: 1024}
SSD_CHUNK = 128
SSD_CHUNKS_PER_STEP = 2
RWKV_CHUNK = 64
RWKV_BATCH_ROWS = 2
RWKV_CHUNKS_PER_STEP = 2
MERGE_TM = 512
VMEM_LIMIT = 48 * 1024 * 1024

BF16 = jnp.bfloat16
F32 = jnp.float32


def _dot(a, b):
    return jnp.dot(a, b, preferred_element_type=F32)


def _dot_nt(a, b):
    return lax.dot_general(a, b, (((1,), (1,)), ((), ())), preferred_element_type=F32)


def _dot_tn(a, b):
    return lax.dot_general(a, b, (((0,), (0,)), ((), ())), preferred_element_type=F32)


def _split_terms(x, n):
    terms = []
    rem = x
    for _ in range(n):
        t = rem.astype(BF16)
        terms.append(t)
        rem = rem - t.astype(F32)
    return terms


def _dot_left_exact(m_bf16, x, n=3):
    acc = None
    for t in _split_terms(x, n):
        p = _dot(m_bf16, t)
        acc = p if acc is None else acc + p
    return acc


def _two_terms_on_lanes(x):
    lane = lax.broadcasted_iota(jnp.int32, x.shape, 1)
    xm = jnp.where(lane < SSM_HEADS, x, 0.0)
    hi = xm.astype(BF16).astype(F32)
    return (hi + pltpu.roll(xm - hi, SSM_HEADS, 1)).astype(BF16)


def _softplus(x):
    return jnp.maximum(x, 0.0) + jnp.log(1.0 + jnp.exp(-jnp.abs(x)))


def _sigmoid(x):
    return 0.5 * jnp.tanh(0.5 * x) + 0.5


def _silu(x):
    h = 0.5 * x
    return h + h * jnp.tanh(h)


def _lower_tri(n, strict):
    row = lax.broadcasted_iota(jnp.int32, (n, n), 0)
    col = lax.broadcasted_iota(jnp.int32, (n, n), 1)
    return (col < row) if strict else (col <= row)


def _lower_tri_ones(n):
    row = lax.broadcasted_iota(jnp.int32, (n, n), 0)
    col = lax.broadcasted_iota(jnp.int32, (n, n), 1)
    return jnp.clip(row - col + 1, 0, 1).astype(F32).astype(BF16)


def _normed_input(x_ref, gain_ref, h_ref):
    @pl.when(pl.program_id(1) == 0)
    def _():
        x = x_ref[...]
        ms = jnp.mean(x * x, axis=-1, keepdims=True)
        h_ref[...] = (x * lax.rsqrt(ms + EPS) * gain_ref[...]).astype(BF16)


def _delayed(o, halo, k):
    head = jnp.concatenate([halo, o[:SUBLANES]], axis=0)[SUBLANES - k:2 * SUBLANES - k]
    if o.shape[0] == SUBLANES:
        return head
    return jnp.concatenate([head, pltpu.roll(o, k, 0)[SUBLANES:]], axis=0)


def _zero_carry_at_sequence_start(carry_ref, row_blocks_per_seq):
    i, j = pl.program_id(0), pl.program_id(1)

    @pl.when(i % row_blocks_per_seq == 0)
    def _():
        carry_ref[j] = jnp.zeros(carry_ref.shape[1:], F32)


def _resident_weights(w_ref, wq_ref):
    j = pl.program_id(1)

    @pl.when(pl.program_id(0) == 0)
    def _():
        wq_ref[j] = w_ref[...].astype(BF16)

    return wq_ref.at[j]


def _proj_plain_kernel(h_ref, w_ref, o_ref, wq_ref):
    w = _resident_weights(w_ref, wq_ref)
    o_ref[...] = _dot(h_ref[...], w[...])


def _proj_conv_kernel(row_blocks_per_seq, x_ref, gain_ref, w_ref, cw_ref, cb_ref, o_ref, h_ref, wq_ref,
                      carry_ref):
    _normed_input(x_ref, gain_ref, h_ref)
    w = _resident_weights(w_ref, wq_ref)
    _zero_carry_at_sequence_start(carry_ref, row_blocks_per_seq)
    j = pl.program_id(1)
    for c0 in range(0, o_ref.shape[1], PROJ_SUBTILE):
        cs = slice(c0, c0 + PROJ_SUBTILE)
        o = _dot(h_ref[...], w[:, cs])
        halo = carry_ref[j, :, cs]
        carry_ref[j, :, cs] = o[o.shape[0] - SUBLANES:, :]
        e = _delayed(o, halo, 1)
        e_halo = _delayed(halo, jnp.zeros_like(halo), 1)
        inner = cw_ref[1:2, cs] * o + cw_ref[0:1, cs] * e
        inner_halo = cw_ref[1:2, cs] * halo + cw_ref[0:1, cs] * e_halo
        acc = cb_ref[:, cs] + cw_ref[3:4, cs] * o + cw_ref[2:3, cs] * e + _delayed(inner, inner_halo, 2)
        o_ref[:, cs] = _silu(acc)


def _proj_shift_kernel(row_blocks_per_seq, h_ref, w_ref, mu_ref, o_ref, wq_ref, carry_ref):
    w = _resident_weights(w_ref, wq_ref)
    _zero_carry_at_sequence_start(carry_ref, row_blocks_per_seq)
    j = pl.program_id(1)
    for c0 in range(0, o_ref.shape[1], PROJ_SUBTILE):
        cs = slice(c0, c0 + PROJ_SUBTILE)
        o = _dot(h_ref[...], w[:, cs])
        halo = carry_ref[j, :, cs]
        carry_ref[j, :, cs] = o[o.shape[0] - SUBLANES:, :]
        o_ref[:, cs] = o + mu_ref[:, cs] * (_delayed(o, halo, 1) - o)


def _weight_spec(n, tn, col0):
    first, last = col0 // tn, col0 // tn + n // tn - 1
    return pl.BlockSpec((D_MODEL, tn), lambda i, j: (0, jnp.where(i == 0, first + j, last)))


def _project_normed(kernel_fn, seq, x2d, pre_gain, w, col0, n, col_params):
    t = x2d.shape[0]
    tm, tn = PROJ_TM[n], PROJ_TN[n]
    return pl.pallas_call(
        functools.partial(kernel_fn, seq // tm),
        grid=(t // tm, n // tn),
        in_specs=[
            pl.BlockSpec((tm, D_MODEL), lambda i, j: (i, 0)),
            pl.BlockSpec((1, D_MODEL), lambda i, j: (0, 0)),
            _weight_spec(n, tn, col0),
        ] + [pl.BlockSpec((c.shape[0], tn), lambda i, j: (0, j)) for c in col_params],
        out_specs=[pl.BlockSpec((tm, tn), lambda i, j: (i, j)),
                   pl.BlockSpec((tm, D_MODEL), lambda i, j: (i, 0))],
        out_shape=[jax.ShapeDtypeStruct((t, n), F32), jax.ShapeDtypeStruct((t, D_MODEL), BF16)],
        scratch_shapes=[pltpu.VMEM((n // tn, D_MODEL, tn), BF16), pltpu.VMEM((n // tn, SUBLANES, tn), F32)],
        compiler_params=pltpu.CompilerParams(
            dimension_semantics=("arbitrary", "arbitrary"), vmem_limit_bytes=VMEM_LIMIT),
        name="in_proj_%d" % n,
    )(x2d, pre_gain, w, *col_params)


def _project(kernel_fn, seq, h, w, col_params, carry):
    t, n, col0 = h.shape[0], w.shape[1], 0
    tm, tn = PROJ_TM[n], PROJ_TN[n]
    return pl.pallas_call(
        functools.partial(kernel_fn, seq // tm) if carry else kernel_fn,
        grid=(t // tm, n // tn),
        in_specs=[
            pl.BlockSpec((tm, D_MODEL), lambda i, j: (i, 0)),
            _weight_spec(n, tn, col0),
        ] + [pl.BlockSpec((c.shape[0], tn), lambda i, j: (0, j)) for c in col_params],
        out_specs=pl.BlockSpec((tm, tn), lambda i, j: (i, j)),
        out_shape=jax.ShapeDtypeStruct((t, n), F32),
        scratch_shapes=[pltpu.VMEM((n // tn, D_MODEL, tn), BF16)] + ([pltpu.VMEM((n // tn, SUBLANES, tn), F32)] if carry else []),
        compiler_params=pltpu.CompilerParams(
            dimension_semantics=("arbitrary", "arbitrary"), vmem_limit_bytes=VMEM_LIMIT),
        name="in_proj_%d" % n,
    )(h, w, *col_params)


def _ssd_kernel(u_ref, z_ref, sm_ref, dtb_ref, alog_ref, dskip_ref, gain_ref, expand_ref,
                y_ref, st_ref, yd_ref):
    L = SSD_CHUNK

    @pl.when(pl.program_id(1) == 0)
    def _():
        st_ref[...] = jnp.zeros_like(st_ref)

    tri = _lower_tri(L, strict=False)
    tri_ones = _lower_tri_ones(L)
    head_a = lax.broadcasted_iota(jnp.int32, (L, LANES), 1) < SSM_HEAD_DIM
    for ci in range(SSD_CHUNKS_PER_STEP):
        rows = slice(ci * L, (ci + 1) * L)
        dt = _softplus(sm_ref[rows, LANES:2 * LANES] + dtb_ref[...])
        a = dt * (-jnp.exp(alog_ref[...]))
        acum = _dot_left_exact(tri_ones, a)
        acum_t = acum.T
        dt_t = dt.T
        a_last = acum[L - 1:L, :]
        dec_out_p = _two_terms_on_lanes(jnp.exp(acum))
        w_state_p = _two_terms_on_lanes(dt * jnp.exp(a_last - acum))

        for g in range(SSM_GROUPS):
            c0 = g * SSM_GROUP_WIDTH
            ex = expand_ref[:, c0:c0 + SSM_GROUP_WIDTH]
            xs = u_ref[rows, c0:c0 + SSM_GROUP_WIDTH]
            b0 = SSM_INNER + g * SSM_STATE
            bm = u_ref[rows, b0:b0 + SSM_STATE].astype(BF16)
            cm = u_ref[rows, b0 + SSM_GROUPS * SSM_STATE:b0 + (SSM_GROUPS + 1) * SSM_STATE].astype(BF16)
            dec_out_x = _dot(dec_out_p, ex)
            w_state_x = _dot(w_state_p, ex)
            xs_b = xs.astype(BF16)

            scores = _dot_nt(cm, bm)
            state = st_ref[g]
            y_off = _dot(cm, state.astype(BF16)) * dec_out_x
            st_ref[g] = state * dec_out_x[L - 1:L, :] + _dot_tn(bm, (xs * w_state_x).astype(BF16))

            def mix_of(h):
                seg = acum[:, h:h + 1] - acum_t[h:h + 1, :]
                return (scores * jnp.exp(jnp.where(tri, seg, -jnp.inf)) * dt_t[h:h + 1, :]).astype(BF16)

            for jp in range(SSM_GROUP_WIDTH // LANES):
                h = g * (SSM_GROUP_WIDTH // SSM_HEAD_DIM) + 2 * jp
                xp = xs_b[:, jp * LANES:(jp + 1) * LANES]
                x2 = jnp.concatenate([jnp.where(head_a, xp, 0.0), jnp.where(head_a, 0.0, xp)], axis=0)
                yd_ref[ci, :, jp * LANES:(jp + 1) * LANES] = _dot(
                    jnp.concatenate([mix_of(h), mix_of(h + 1)], axis=1), x2)

            y = yd_ref[ci] + y_off + dskip_ref[:, c0:c0 + SSM_GROUP_WIDTH] * xs
            z = z_ref[rows, c0:c0 + SSM_GROUP_WIDTH]
            y = y * _silu(z)
            ms = jnp.mean(y * y, axis=-1, keepdims=True)
            y_ref[rows, c0:c0 + SSM_GROUP_WIDTH] = (
                y * lax.rsqrt(ms + EPS) * gain_ref[:, c0:c0 + SSM_GROUP_WIDTH]).astype(BF16)


def _ssd_branch(u, zg, rw, batch, seq, dt_bias, a_log, dskip_x, norm_gain, expand):
    L = SSD_CHUNK * SSD_CHUNKS_PER_STEP
    nc = seq // L
    row = lambda b, c: b * nc + c
    const = lambda shape: pl.BlockSpec(shape, lambda b, c: (0, 0))
    return pl.pallas_call(
        _ssd_kernel,
        grid=(batch, nc),
        in_specs=[
            pl.BlockSpec((L, SSM_XBC), lambda b, c: (row(b, c), 0)),
            pl.BlockSpec((L, SSM_INNER), lambda b, c: (row(b, c), 0)),
            pl.BlockSpec((L, SMALL_WIDTH), lambda b, c: (row(b, c), 4 * RWKV_DIM // SMALL_WIDTH)),
            const((1, LANES)),
            const((1, LANES)),
            const((1, SSM_INNER)),
            const((1, SSM_INNER)),
            const((LANES, SSM_INNER)),
        ],
        out_specs=pl.BlockSpec((L, SSM_INNER), lambda b, c: (row(b, c), 0)),
        out_shape=jax.ShapeDtypeStruct((batch * seq, SSM_INNER), BF16),
        scratch_shapes=[
            pltpu.VMEM((SSM_GROUPS, SSM_STATE, SSM_GROUP_WIDTH), F32),
            pltpu.VMEM((SSD_CHUNKS_PER_STEP, SSD_CHUNK, SSM_GROUP_WIDTH), F32),
        ],
        compiler_params=pltpu.CompilerParams(
            dimension_semantics=("arbitrary", "arbitrary"), vmem_limit_bytes=VMEM_LIMIT),
        name="ssd_branch",
    )(u, zg, rw, dt_bias, a_log, dskip_x, norm_gain, expand)


def _rwkv_kernel(r_ref, k_ref, v_ref, g_ref, sm_ref, w0_ref, w2_ref, a0_ref, a2_ref, kks_ref, kas_ref,
                 rk_ref, gng_ref, gnb_ref, y_ref, s_ref):
    C = RWKV_CHUNK
    D = RWKV_HEAD_DIM
    NB = RWKV_BATCH_ROWS
    CP = RWKV_CHUNKS_PER_STEP
    R = NB * CP * C

    @pl.when(pl.program_id(1) == 0)
    def _():
        s_ref[...] = jnp.zeros_like(s_ref)

    rows = lambda x_ref: x_ref[...].reshape(R, x_ref.shape[-1])
    r, k, v, gate, sm = rows(r_ref), rows(k_ref), rows(v_ref), rows(g_ref), rows(sm_ref)
    wd = sm[:, 0:LORA]
    ad = sm[:, LORA:2 * LORA]

    lw = _dot(jnp.tanh(wd).astype(BF16), w2_ref[...])
    logw = (-DECAY_SCALE) * _sigmoid(w0_ref[...] + lw)
    neg_a = -_sigmoid(a0_ref[...] + _dot(ad.astype(BF16), a2_ref[...]))
    kkp = k * kks_ref[...]
    k2 = k * (1.0 - (neg_a + 1.0) * kas_ref[...])

    rr = lax.broadcasted_iota(jnp.int32, (R, R), 0)
    cc = lax.broadcasted_iota(jnp.int32, (R, R), 1)
    same_seq = 1 - jnp.clip(jnp.abs((rr // C) - (cc // C)), 0, 1)
    tri = (jnp.clip(rr - cc + 1, 0, 1) * same_seq).astype(F32).astype(BF16)
    cum = _dot_left_exact(tri, logw)
    e_pos = jnp.exp(cum)
    e_neg = 1.0 / e_pos
    kc_p = kkp * jnp.exp(cum - logw)
    rt = r * e_pos
    kt = k2 * e_neg
    nbt_p = kkp * neg_a * e_neg
    gamma_end = [jnp.exp(cum[(q + 1) * C - 1:(q + 1) * C, :]) for q in range(NB * CP)]
    rkk = r * k2 * rk_ref[...]
    gsilu = _silu(gate)

    P = 2 * D
    lane_a = lax.broadcasted_iota(jnp.int32, (C, P), 1) < D
    lane_a2 = lax.broadcasted_iota(jnp.int32, (2 * C, P), 1) < D
    row2 = lax.broadcasted_iota(jnp.int32, (2 * C, 2 * C), 0)
    col2 = lax.broadcasted_iota(jnp.int32, (2 * C, 2 * C), 1)
    colm = jnp.where(col2 >= C, col2 - C, col2)
    gmask = colm < jnp.where(row2 >= C, row2 - C + 1, row2)
    diag_blk = jnp.where(row2 >= C, 1, 0) == jnp.where(col2 >= C, 1, 0)
    anti_eye = jnp.where(col2 == jnp.where(row2 >= C, row2 - C, row2 + C), 1.0, 0.0)

    def seg_sum(x):
        sa = jnp.sum(jnp.where(lane_a, x, 0.0), axis=-1, keepdims=True)
        sb = jnp.sum(jnp.where(lane_a, 0.0, x), axis=-1, keepdims=True)
        return jnp.where(lane_a, sa, sb)

    def cross(lo, hi):
        return jnp.concatenate([lo, hi], axis=0).astype(BF16)

    units = [(b, j) for b in range(NB) for j in range(RWKV_HEADS // 2)]
    n_units = range(len(units))

    def chunk_chain(ci):
        idx = [(slice((b * CP + ci) * C, (b * CP + ci + 1) * C), slice(j * P, (j + 1) * P)) for b, j in units]
        xb, wa, g_top, g_bot, v_p = [], [], [], [], []
        for rs, sl in idx:
            kkp_p = kkp[rs, sl]
            rinv = 1.0 / jnp.maximum(jnp.sqrt(seg_sum(kkp_p * kkp_p)), 1e-12)
            nbt = nbt_p[rs, sl] * rinv
            xp = jnp.concatenate([kc_p[rs, sl] * rinv, rt[rs, sl]], axis=0)
            w_a = jnp.concatenate([nbt, kt[rs, sl]], axis=0).astype(BF16)
            w_b = jnp.concatenate([kt[rs, sl], nbt], axis=0).astype(BF16)
            g_a = jnp.where(gmask, _dot_nt(jnp.where(lane_a2, xp, 0.0).astype(BF16), w_a), 0.0)
            g_b = jnp.where(gmask, _dot_nt(jnp.where(lane_a2, 0.0, xp).astype(BF16), w_b), 0.0)
            xb.append(xp.astype(BF16))
            wa.append(w_a)
            g_top.append(jnp.concatenate([g_a[:C], g_b[:C]], axis=0))
            g_bot.append(jnp.concatenate([g_a[C:], g_b[C:]], axis=0))
            v_p.append(v[rs, sl])

        z = [_dot_nt(xb[i], s_ref[units[i][0], units[i][1]].astype(BF16)) for i in n_units]
        w1 = []
        for i in n_units:
            rw = _dot(g_top[i].astype(BF16),
                      cross(jnp.where(lane_a, 0.0, v_p[i]), jnp.where(lane_a, v_p[i], 0.0)))
            w1.append(jnp.where(lane_a, rw[:C], rw[C:]))

        pt = [jnp.where(diag_blk, g_top[i], anti_eye) for i in n_units]
        for _ in range(6):
            pt = [_dot(jnp.where(diag_blk, pt[i], 0.0).astype(BF16), pt[i].astype(BF16))
                  + jnp.where(diag_blk, 0.0, pt[i]) for i in n_units]

        u = []
        for i in n_units:
            rhs = z[i][:C] + w1[i]
            ru = _dot(pt[i].astype(BF16), cross(jnp.where(lane_a, 0.0, rhs), jnp.where(lane_a, rhs, 0.0)))
            u.append(ru[:C] + ru[C:])
        for i in n_units:
            (b, j), (rs, sl) = units[i], idx[i]
            ry = _dot(g_bot[i].astype(BF16),
                      cross(jnp.where(lane_a, u[i], v_p[i]), jnp.where(lane_a, v_p[i], u[i])))
            y = z[i][C:] + jnp.where(lane_a, ry[:C], ry[C:])
            upd = _dot_tn(jnp.concatenate([u[i], v_p[i]], axis=0).astype(BF16), wa[i])
            s_ref[b, j] = (s_ref[b, j] + jnp.where(diag_blk, upd, 0.0)) * gamma_end[b * CP + ci][:, sl]

            yc = y - seg_sum(y) * (1.0 / D)
            var = seg_sum(yc * yc) * (1.0 / D)
            yn = yc * lax.rsqrt(var + GN_EPS) * gng_ref[:, sl] + gnb_ref[:, sl]
            bonus = seg_sum(rkk[rs, sl]) * v_p[i]
            y_ref[b, ci * C:(ci + 1) * C, sl] = ((yn + bonus) * gsilu[rs, sl]).astype(BF16)

    for ci in range(CP):
        chunk_chain(ci)


def _rwkv_branch(rw, batch, seq, w0, w2, a0, a2, k_k, k_a, r_k, gn_gain, gn_bias):
    C = RWKV_CHUNK
    NB = RWKV_BATCH_ROWS
    CP = RWKV_CHUNKS_PER_STEP
    rw3 = rw.reshape(batch, seq, RW_COLS)
    const = lambda shape: pl.BlockSpec(shape, lambda b, c: (0, 0))
    col_block = lambda j: pl.BlockSpec((NB, CP * C, RWKV_DIM), lambda b, c: (b, c, j))
    return pl.pallas_call(
        _rwkv_kernel,
        grid=(batch // NB, seq // (CP * C)),
        in_specs=[
            col_block(0), col_block(1), col_block(2), col_block(3),
            pl.BlockSpec((NB, CP * C, SMALL_WIDTH), lambda b, c: (b, c, 4 * RWKV_DIM // SMALL_WIDTH)),
            const((1, RWKV_DIM)), const((LORA, RWKV_DIM)), const((1, RWKV_DIM)), const((LORA, RWKV_DIM)),
            const((1, RWKV_DIM)), const((1, RWKV_DIM)), const((1, RWKV_DIM)),
            const((1, RWKV_DIM)), const((1, RWKV_DIM)),
        ],
        out_specs=pl.BlockSpec((NB, CP * C, RWKV_DIM), lambda b, c: (b, c, 0)),
        out_shape=jax.ShapeDtypeStruct((batch, seq, RWKV_DIM), BF16),
        scratch_shapes=[
            pltpu.VMEM((NB, RWKV_HEADS // 2, 2 * RWKV_HEAD_DIM, 2 * RWKV_HEAD_DIM), F32),
        ],
        compiler_params=pltpu.CompilerParams(
            dimension_semantics=("arbitrary", "arbitrary"), vmem_limit_bytes=VMEM_LIMIT),
        name="rwkv_branch",
    )(rw3, rw3, rw3, rw3, rw3, w0, w2, a0, a2, k_k, k_a, r_k, gn_gain, gn_bias
      ).reshape(batch * seq, RWKV_DIM)


def _merge_kernel(x_ref, ys_ref, yr_ref, g0_ref, g1_ref, bg_ref, ws_ref, wr_ref, wo_ref, gain_ref, o_ref):
    g_ssm = _sigmoid(g0_ref[...] + bg_ref[:, 0:D_MODEL])
    g_rwkv = _sigmoid(g1_ref[...] + bg_ref[:, D_MODEL:2 * D_MODEL])
    merged = g_ssm * _dot(ys_ref[...], ws_ref[...]) + g_rwkv * _dot(yr_ref[...], wr_ref[...])
    out = _dot(merged.astype(BF16), wo_ref[...])
    ms = jnp.mean(out * out, axis=-1, keepdims=True)
    o_ref[...] = x_ref[...] + out * lax.rsqrt(ms + EPS) * gain_ref[...]


def _merge(x2d, y_ssm, y_rwkv, zg, b_gate, w_ssm, w_rwkv, w_out, post_gain):
    t = x2d.shape[0]
    tm = MERGE_TM
    const = lambda shape: pl.BlockSpec(shape, lambda i: (0, 0), pipeline_mode=pl.Buffered(1))
    return pl.pallas_call(
        _merge_kernel,
        grid=(t // tm,),
        in_specs=[
            pl.BlockSpec((tm, D_MODEL), lambda i: (i, 0)),
            pl.BlockSpec((tm, SSM_INNER), lambda i: (i, 0)),
            pl.BlockSpec((tm, RWKV_DIM), lambda i: (i, 0)),
            pl.BlockSpec((tm, D_MODEL), lambda i: (i, SSM_INNER // D_MODEL)),
            pl.BlockSpec((tm, D_MODEL), lambda i: (i, SSM_INNER // D_MODEL + 1)),
            const((1, 2 * D_MODEL)),
            const((SSM_INNER, D_MODEL)), const((RWKV_DIM, D_MODEL)), const((D_MODEL, D_MODEL)),
            const((1, D_MODEL)),
        ],
        out_specs=pl.BlockSpec((tm, D_MODEL), lambda i: (i, 0)),
        out_shape=jax.ShapeDtypeStruct((t, D_MODEL), F32),
        compiler_params=pltpu.CompilerParams(
            dimension_semantics=("arbitrary",), vmem_limit_bytes=VMEM_LIMIT),
        name="gated_merge",
    )(x2d, y_ssm, y_rwkv, zg, zg, b_gate, w_ssm, w_rwkv, w_out, post_gain)


def _pad_lanes(v, width):
    return jnp.pad(v, ((0, 0), (0, width - v.shape[-1])))


def _layer(x, pre_gain, w_in, b_gate, conv_w, conv_b, dt_bias, a_log, d_skip, ssm_norm_gain,
           rwkv_mu, decay_w0, decay_w2, iclr_a0, iclr_a2, k_k, k_a, r_k, gn_gain, gn_bias,
           w_branch_ssm, w_branch_rwkv, w_out, post_gain):
    batch, seq, _ = x.shape
    x2d = x.reshape(batch * seq, D_MODEL)
    row = lambda v: v.reshape(1, -1)

    o_xbc, o_dt, o_rw = SSM_INNER, SSM_INNER + SSM_XBC, SSM_INNER + SSM_XBC + SSM_HEADS
    o_lora, o_gate = o_rw + 4 * RWKV_DIM, o_rw + 4 * RWKV_DIM + 2 * LORA
    pad = SMALL_WIDTH - 2 * LORA - SSM_HEADS
    w_rw = jnp.concatenate([w_in[:, o_rw:o_gate], w_in[:, o_dt:o_rw], jnp.zeros((D_MODEL, pad + RW_PAD), w_in.dtype)],
                           axis=1)
    w_zg = jnp.concatenate([w_in[:, :o_xbc], w_in[:, o_gate:]], axis=1)
    mu = _pad_lanes(row(rwkv_mu), RW_COLS)

    gain = row(pre_gain)
    u, h = _project_normed(_proj_conv_kernel, seq, x2d, gain, w_in, o_xbc, SSM_XBC, [conv_w, row(conv_b)])
    rw = _project(_proj_shift_kernel, seq, h, w_rw, [mu], True)
    zg = _project(_proj_plain_kernel, seq, h, w_zg, [], False)

    head_of_col = jnp.arange(SSM_INNER) // SSM_HEAD_DIM
    expand = ((jnp.arange(LANES)[:, None] % SSM_HEADS == head_of_col[None, :])
              & (jnp.arange(LANES)[:, None] < 2 * SSM_HEADS)).astype(BF16)
    y_ssm = _ssd_branch(
        u, zg, rw, batch, seq, _pad_lanes(row(dt_bias), LANES), _pad_lanes(row(a_log), LANES),
        row(jnp.repeat(d_skip, SSM_HEAD_DIM)), row(ssm_norm_gain), expand)

    y_rwkv = _rwkv_branch(
        rw, batch, seq, row(decay_w0), decay_w2.astype(BF16), row(iclr_a0), iclr_a2.astype(BF16),
        row(k_k), row(k_a), row(r_k), row(gn_gain), row(gn_bias))

    out = _merge(x2d, y_ssm, y_rwkv, zg, row(b_gate), w_branch_ssm.astype(BF16),
                 w_branch_rwkv.astype(BF16), w_out.astype(BF16), row(post_gain))
    return out.reshape(batch, seq, D_MODEL)


def kernel(x, pre_gain, w_in, b_gate, conv_w, conv_b, dt_bias, a_log, d_skip, ssm_norm_gain, rwkv_mu,
           decay_w0, decay_w2, iclr_a0, iclr_a2, k_k, k_a, r_k, gn_gain, gn_bias, w_branch_ssm,
           w_branch_rwkv, w_out, post_gain):
    for layer in range(pre_gain.shape[0]):
        x = _layer(
            x, pre_gain[layer], w_in[layer], b_gate[layer], conv_w[layer], conv_b[layer], dt_bias[layer],
            a_log[layer], d_skip[layer], ssm_norm_gain[layer], rwkv_mu[layer], decay_w0[layer],
            decay_w2[layer], iclr_a0[layer], iclr_a2[layer], k_k[layer], k_a[layer], r_k[layer],
            gn_gain[layer], gn_bias[layer], w_branch_ssm[layer], w_branch_rwkv[layer], w_out[layer],
            post_gain[layer])
    return x
```

```python
import functools

import jax
import jax.numpy as jnp
from jax import lax
from jax.experimental import pallas as pl
from jax.experimental.pallas import tpu as pltpu

D_MODEL = 1024
EPS = 1e-6

SSM_INNER = 2048
SSM_HEAD_DIM = 64
SSM_HEADS = 32
SSM_STATE = 128
SSM_GROUPS = 4
SSM_GROUP_WIDTH = SSM_INNER // SSM_GROUPS
CONV_WIDTH = 4
SSM_XBC = 3072

RWKV_DIM = 1024
RWKV_HEAD_DIM = 64
RWKV_HEADS = 16
LORA = 64
GN_EPS = RWKV_HEAD_DIM * 1e-5
DECAY_SCALE = 0.6065306597126334

LANES = 128
SUBLANES = 8

SMALL_WIDTH = 256
RW_PAD = 256
RW_COLS = 4 * RWKV_DIM + SMALL_WIDTH + RW_PAD
ZG_COLS = SSM_INNER + 2 * D_MODEL

PROJ_TM = {SSM_XBC: 1024, RW_COLS: 2048, ZG_COLS: 2048}
PROJ_SUBTILE = 256
PROJ_TN = {SSM_XBC: 1024, RW_COLS: RW_COLS // 6, ZG_COLS: 1024}
SSD_CHUNK = 128
SSD_CHUNKS_PER_STEP = 2
RWKV_CHUNK = 64
RWKV_BATCH_ROWS = 2
RWKV_CHUNKS_PER_STEP = 2
MERGE_TM = 512
VMEM_LIMIT = 48 * 1024 * 1024

BF16 = jnp.bfloat16
F32 = jnp.float32


def _dot(a, b):
    return jnp.dot(a, b, preferred_element_type=F32)


def _dot_nt(a, b):
    return lax.dot_general(a, b, (((1,), (1,)), ((), ())), preferred_element_type=F32)


def _dot_tn(a, b):
    return lax.dot_general(a, b, (((0,), (0,)), ((), ())), preferred_element_type=F32)


def _split_terms(x, n):
    terms = []
    rem = x
    for _ in range(n):
        t = rem.astype(BF16)
        terms.append(t)
        rem = rem - t.astype(F32)
    return terms


def _dot_left_exact(m_bf16, x, n=3):
    acc = None
    for t in _split_terms(x, n):
        p = _dot(m_bf16, t)
        acc = p if acc is None else acc + p
    return acc


def _two_terms_on_lanes(x):
    lane = lax.broadcasted_iota(jnp.int32, x.shape, 1)
    xm = jnp.where(lane < SSM_HEADS, x, 0.0)
    hi = xm.astype(BF16).astype(F32)
    return (hi + pltpu.roll(xm - hi, SSM_HEADS, 1)).astype(BF16)


def _softplus(x):
    return jnp.maximum(x, 0.0) + jnp.log(1.0 + jnp.exp(-jnp.abs(x)))


def _sigmoid(x):
    return 0.5 * jnp.tanh(0.5 * x) + 0.5


def _silu(x):
    h = 0.5 * x
    return h + h * jnp.tanh(h)


def _lower_tri(n, strict):
    row = lax.broadcasted_iota(jnp.int32, (n, n), 0)
    col = lax.broadcasted_iota(jnp.int32, (n, n), 1)
    return (col < row) if strict else (col <= row)


def _lower_tri_ones(n):
    row = lax.broadcasted_iota(jnp.int32, (n, n), 0)
    col = lax.broadcasted_iota(jnp.int32, (n, n), 1)
    return jnp.clip(row - col + 1, 0, 1).astype(F32).astype(BF16)


def _normed_input(x_ref, gain_ref, h_ref):
    @pl.when(pl.program_id(1) == 0)
    def _():
        x = x_ref[...]
        ms = jnp.mean(x * x, axis=-1, keepdims=True)
        h_ref[...] = (x * lax.rsqrt(ms + EPS) * gain_ref[...]).astype(BF16)


def _delayed(o, halo, k):
    head = jnp.concatenate([halo, o[:SUBLANES]], axis=0)[SUBLANES - k:2 * SUBLANES - k]
    if o.shape[0] == SUBLANES:
        return head
    return jnp.concatenate([head, pltpu.roll(o, k, 0)[SUBLANES:]], axis=0)


def _zero_carry_at_sequence_start(carry_ref, row_blocks_per_seq):
    i, j = pl.program_id(0), pl.program_id(1)

    @pl.when(i % row_blocks_per_seq == 0)
    def _():
        carry_ref[j] = jnp.zeros(carry_ref.shape[1:], F32)


def _proj_plain_kernel(h_ref, w_ref, o_ref):
    o_ref[...] = _dot(h_ref[...], w_ref[...])


def _proj_conv_kernel(row_blocks_per_seq, x_ref, gain_ref, w_ref, cw_ref, cb_ref, o_ref, h_ref, carry_ref):
    _normed_input(x_ref, gain_ref, h_ref)
    _zero_carry_at_sequence_start(carry_ref, row_blocks_per_seq)
    j = pl.program_id(1)
    for c0 in range(0, o_ref.shape[1], PROJ_SUBTILE):
        cs = slice(c0, c0 + PROJ_SUBTILE)
        o = _dot(h_ref[...], w_ref[:, cs])
        halo = carry_ref[j, :, cs]
        carry_ref[j, :, cs] = o[o.shape[0] - SUBLANES:, :]
        w0, w1, w2, w3 = (0.5 * cw_ref[t:t + 1, cs] for t in range(CONV_WIDTH))
        e = _delayed(o, halo, 1)
        e_halo = _delayed(halo, jnp.zeros_like(halo), 1)
        inner = w1 * o + w0 * e
        inner_halo = w1 * halo + w0 * e_halo
        h = 0.5 * cb_ref[:, cs] + w3 * o + w2 * e + _delayed(inner, inner_halo, 2)
        o_ref[:, cs] = h + h * jnp.tanh(h)


def _proj_shift_kernel(row_blocks_per_seq, h_ref, w_ref, mu_ref, o_ref, carry_ref):
    _zero_carry_at_sequence_start(carry_ref, row_blocks_per_seq)
    j = pl.program_id(1)
    for c0 in range(0, o_ref.shape[1], PROJ_SUBTILE):
        cs = slice(c0, c0 + PROJ_SUBTILE)
        o = _dot(h_ref[...], w_ref[:, cs])
        halo = carry_ref[j, :, cs]
        carry_ref[j, :, cs] = o[o.shape[0] - SUBLANES:, :]
        o_ref[:, cs] = o + mu_ref[:, cs] * (_delayed(o, halo, 1) - o)


def _project_normed(kernel_fn, seq, x2d, pre_gain, w, col_params):
    t, n = x2d.shape[0], w.shape[1]
    tm, tn = PROJ_TM[n], PROJ_TN[n]
    return pl.pallas_call(
        functools.partial(kernel_fn, seq // tm),
        grid=(t // tm, n // tn),
        in_specs=[
            pl.BlockSpec((tm, D_MODEL), lambda i, j: (i, 0)),
            pl.BlockSpec((1, D_MODEL), lambda i, j: (0, 0)),
            pl.BlockSpec((D_MODEL, tn), lambda i, j: (0, j)),
        ] + [pl.BlockSpec((c.shape[0], tn), lambda i, j: (0, j)) for c in col_params],
        out_specs=[pl.BlockSpec((tm, tn), lambda i, j: (i, j)),
                   pl.BlockSpec((tm, D_MODEL), lambda i, j: (i, 0))],
        out_shape=[jax.ShapeDtypeStruct((t, n), F32), jax.ShapeDtypeStruct((t, D_MODEL), BF16)],
        scratch_shapes=[pltpu.VMEM((n // tn, SUBLANES, tn), F32)],
        compiler_params=pltpu.CompilerParams(
            dimension_semantics=("arbitrary", "arbitrary"), vmem_limit_bytes=VMEM_LIMIT),
        name="in_proj_%d" % n,
    )(x2d, pre_gain, w, *col_params)


def _project(kernel_fn, seq, h, w, col_params, carry):
    t, n = h.shape[0], w.shape[1]
    tm, tn = PROJ_TM[n], PROJ_TN[n]
    return pl.pallas_call(
        functools.partial(kernel_fn, seq // tm) if carry else kernel_fn,
        grid=(t // tm, n // tn),
        in_specs=[
            pl.BlockSpec((tm, D_MODEL), lambda i, j: (i, 0)),
            pl.BlockSpec((D_MODEL, tn), lambda i, j: (0, j)),
        ] + [pl.BlockSpec((c.shape[0], tn), lambda i, j: (0, j)) for c in col_params],
        out_specs=pl.BlockSpec((tm, tn), lambda i, j: (i, j)),
        out_shape=jax.ShapeDtypeStruct((t, n), F32),
        scratch_shapes=[pltpu.VMEM((n // tn, SUBLANES, tn), F32)] if carry else [],
        compiler_params=pltpu.CompilerParams(
            dimension_semantics=("arbitrary", "arbitrary"), vmem_limit_bytes=VMEM_LIMIT),
        name="in_proj_%d" % n,
    )(h, w, *col_params)


def _ssd_kernel(u_ref, z_ref, sm_ref, dtb_ref, alog_ref, dskip_ref, gain_ref, expand_ref,
                y_ref, st_ref, yd_ref):
    L = SSD_CHUNK

    @pl.when(pl.program_id(1) == 0)
    def _():
        st_ref[...] = jnp.zeros_like(st_ref)

    tri = _lower_tri(L, strict=False)
    tri_ones = _lower_tri_ones(L)
    head_a = lax.broadcasted_iota(jnp.int32, (L, LANES), 1) < SSM_HEAD_DIM
    for ci in range(SSD_CHUNKS_PER_STEP):
        rows = slice(ci * L, (ci + 1) * L)
        dt = _softplus(sm_ref[rows, LANES:2 * LANES] + dtb_ref[...])
        a = dt * (-jnp.exp(alog_ref[...]))
        acum = _dot_left_exact(tri_ones, a)
        acum_t = acum.T
        dt_t = dt.T
        a_last = acum[L - 1:L, :]
        dec_out_p = _two_terms_on_lanes(jnp.exp(acum))
        w_state_p = _two_terms_on_lanes(dt * jnp.exp(a_last - acum))

        for g in range(SSM_GROUPS):
            c0 = g * SSM_GROUP_WIDTH
            ex = expand_ref[:, c0:c0 + SSM_GROUP_WIDTH]
            xs = u_ref[rows, c0:c0 + SSM_GROUP_WIDTH]
            b0 = SSM_INNER + g * SSM_STATE
            bm = u_ref[rows, b0:b0 + SSM_STATE].astype(BF16)
            cm = u_ref[rows, b0 + SSM_GROUPS * SSM_STATE:b0 + (SSM_GROUPS + 1) * SSM_STATE].astype(BF16)
            dec_out_x = _dot(dec_out_p, ex)
            w_state_x = _dot(w_state_p, ex)
            xs_b = xs.astype(BF16)

            scores = _dot_nt(cm, bm)
            state = st_ref[g]
            y_off = _dot(cm, state.astype(BF16)) * dec_out_x
            st_ref[g] = state * dec_out_x[L - 1:L, :] + _dot_tn(bm, (xs * w_state_x).astype(BF16))

            def mix_of(h):
                seg = acum[:, h:h + 1] - acum_t[h:h + 1, :]
                return (scores * jnp.exp(jnp.where(tri, seg, -jnp.inf)) * dt_t[h:h + 1, :]).astype(BF16)

            for jp in range(SSM_GROUP_WIDTH // LANES):
                h = g * (SSM_GROUP_WIDTH // SSM_HEAD_DIM) + 2 * jp
                xp = xs_b[:, jp * LANES:(jp + 1) * LANES]
                x2 = jnp.concatenate([jnp.where(head_a, xp, 0.0), jnp.where(head_a, 0.0, xp)], axis=0)
                yd_ref[ci, :, jp * LANES:(jp + 1) * LANES] = _dot(
                    jnp.concatenate([mix_of(h), mix_of(h + 1)], axis=1), x2)

            y = yd_ref[ci] + y_off + dskip_ref[:, c0:c0 + SSM_GROUP_WIDTH] * xs
            z = z_ref[rows, c0:c0 + SSM_GROUP_WIDTH]
            y = y * _silu(z)
            ms = jnp.mean(y * y, axis=-1, keepdims=True)
            y_ref[rows, c0:c0 + SSM_GROUP_WIDTH] = (
                y * lax.rsqrt(ms + EPS) * gain_ref[:, c0:c0 + SSM_GROUP_WIDTH]).astype(BF16)


def _ssd_branch(u, zg, rw, batch, seq, dt_bias, a_log, dskip_x, norm_gain, expand):
    L = SSD_CHUNK * SSD_CHUNKS_PER_STEP
    nc = seq // L
    row = lambda b, c: b * nc + c
    const = lambda shape: pl.BlockSpec(shape, lambda b, c: (0, 0))
    return pl.pallas_call(
        _ssd_kernel,
        grid=(batch, nc),
        in_specs=[
            pl.BlockSpec((L, SSM_XBC), lambda b, c: (row(b, c), 0)),
            pl.BlockSpec((L, SSM_INNER), lambda b, c: (row(b, c), 0)),
            pl.BlockSpec((L, SMALL_WIDTH), lambda b, c: (row(b, c), 4 * RWKV_DIM // SMALL_WIDTH)),
            const((1, LANES)),
            const((1, LANES)),
            const((1, SSM_INNER)),
            const((1, SSM_INNER)),
            const((LANES, SSM_INNER)),
        ],
        out_specs=pl.BlockSpec((L, SSM_INNER), lambda b, c: (row(b, c), 0)),
        out_shape=jax.ShapeDtypeStruct((batch * seq, SSM_INNER), BF16),
        scratch_shapes=[
            pltpu.VMEM((SSM_GROUPS, SSM_STATE, SSM_GROUP_WIDTH), F32),
            pltpu.VMEM((SSD_CHUNKS_PER_STEP, SSD_CHUNK, SSM_GROUP_WIDTH), F32),
        ],
        compiler_params=pltpu.CompilerParams(
            dimension_semantics=("arbitrary", "arbitrary"), vmem_limit_bytes=VMEM_LIMIT),
        name="ssd_branch",
    )(u, zg, rw, dt_bias, a_log, dskip_x, norm_gain, expand)


def _rwkv_kernel(r_ref, k_ref, v_ref, g_ref, sm_ref, w0_ref, w2_ref, a0_ref, a2_ref, kks_ref, kas_ref,
                 rk_ref, gng_ref, gnb_ref, y_ref, s_ref):
    C = RWKV_CHUNK
    D = RWKV_HEAD_DIM
    NB = RWKV_BATCH_ROWS
    CP = RWKV_CHUNKS_PER_STEP
    R = NB * CP * C

    @pl.when(pl.program_id(1) == 0)
    def _():
        s_ref[...] = jnp.zeros_like(s_ref)

    rows = lambda x_ref: x_ref[...].reshape(R, x_ref.shape[-1])
    r, k, v, gate, sm = rows(r_ref), rows(k_ref), rows(v_ref), rows(g_ref), rows(sm_ref)
    wd = sm[:, 0:LORA]
    ad = sm[:, LORA:2 * LORA]

    lw = _dot(jnp.tanh(wd).astype(BF16), w2_ref[...])
    logw = (-0.5 * DECAY_SCALE) * jnp.tanh(0.5 * (w0_ref[...] + lw)) - 0.5 * DECAY_SCALE
    neg_a = -0.5 * jnp.tanh(0.5 * (a0_ref[...] + _dot(ad.astype(BF16), a2_ref[...]))) - 0.5
    kkp = k * kks_ref[...]
    k2 = k * (1.0 - (neg_a + 1.0) * kas_ref[...])

    tri = _lower_tri_ones(C)
    cum = jnp.concatenate([_dot_left_exact(tri, logw[q * C:(q + 1) * C]) for q in range(NB * CP)], axis=0)
    e_pos = jnp.exp(cum)
    e_neg = 1.0 / e_pos
    kc_p = kkp * jnp.exp(cum - logw)
    rt = r * e_pos
    kt = k2 * e_neg
    nbt_p = kkp * neg_a * e_neg
    gamma_end = [jnp.exp(cum[(q + 1) * C - 1:(q + 1) * C, :]) for q in range(NB * CP)]
    rkk = r * k2 * rk_ref[...]
    gsilu = _silu(gate)

    P = 2 * D
    lane_a = lax.broadcasted_iota(jnp.int32, (C, P), 1) < D
    lane_a2 = lax.broadcasted_iota(jnp.int32, (2 * C, P), 1) < D
    row2 = lax.broadcasted_iota(jnp.int32, (2 * C, 2 * C), 0)
    col2 = lax.broadcasted_iota(jnp.int32, (2 * C, 2 * C), 1)
    colm = jnp.where(col2 >= C, col2 - C, col2)
    gmask = colm < jnp.where(row2 >= C, row2 - C + 1, row2)
    diag_blk = jnp.where(row2 >= C, 1, 0) == jnp.where(col2 >= C, 1, 0)
    anti_eye = jnp.where(col2 == jnp.where(row2 >= C, row2 - C, row2 + C), 1.0, 0.0)

    def seg_sum(x):
        sa = jnp.sum(jnp.where(lane_a, x, 0.0), axis=-1, keepdims=True)
        sb = jnp.sum(jnp.where(lane_a, 0.0, x), axis=-1, keepdims=True)
        return jnp.where(lane_a, sa, sb)

    def cross(lo, hi):
        return jnp.concatenate([lo, hi], axis=0).astype(BF16)

    units = [(b, j) for b in range(NB) for j in range(RWKV_HEADS // 2)]
    n_units = range(len(units))

    def chunk_chain(ci):
        idx = [(slice((b * CP + ci) * C, (b * CP + ci + 1) * C), slice(j * P, (j + 1) * P)) for b, j in units]
        xb, wa, g_top, g_bot, v_p = [], [], [], [], []
        for rs, sl in idx:
            kkp_p = kkp[rs, sl]
            rinv = jnp.minimum(lax.rsqrt(seg_sum(kkp_p * kkp_p)), 1e12)
            nbt = nbt_p[rs, sl] * rinv
            xp = jnp.concatenate([kc_p[rs, sl] * rinv, rt[rs, sl]], axis=0)
            w_a = jnp.concatenate([nbt, kt[rs, sl]], axis=0).astype(BF16)
            w_b = jnp.concatenate([kt[rs, sl], nbt], axis=0).astype(BF16)
            g_a = jnp.where(gmask, _dot_nt(jnp.where(lane_a2, xp, 0.0).astype(BF16), w_a), 0.0)
            g_b = jnp.where(gmask, _dot_nt(jnp.where(lane_a2, 0.0, xp).astype(BF16), w_b), 0.0)
            xb.append(xp.astype(BF16))
            wa.append(w_a)
            g_top.append(jnp.concatenate([g_a[:C], g_b[:C]], axis=0))
            g_bot.append(jnp.concatenate([g_a[C:], g_b[C:]], axis=0))
            v_p.append(v[rs, sl])

        z = [_dot_nt(xb[i], s_ref[units[i][0], units[i][1]].astype(BF16)) for i in n_units]
        w1 = []
        for i in n_units:
            rw = _dot(g_top[i].astype(BF16),
                      cross(jnp.where(lane_a, 0.0, v_p[i]), jnp.where(lane_a, v_p[i], 0.0)))
            w1.append(jnp.where(lane_a, rw[:C], rw[C:]))

        pt = [jnp.where(diag_blk, g_top[i], anti_eye) for i in n_units]
        for _ in range(6):
            pt = [_dot(jnp.where(diag_blk, pt[i], 0.0).astype(BF16), pt[i].astype(BF16))
                  + jnp.where(diag_blk, 0.0, pt[i]) for i in n_units]

        u = []
        for i in n_units:
            rhs = z[i][:C] + w1[i]
            ru = _dot(pt[i].astype(BF16), cross(jnp.where(lane_a, 0.0, rhs), jnp.where(lane_a, rhs, 0.0)))
            u.append(ru[:C] + ru[C:])
        for i in n_units:
            (b, j), (rs, sl) = units[i], idx[i]
            ry = _dot(g_bot[i].astype(BF16),
                      cross(jnp.where(lane_a, u[i], v_p[i]), jnp.where(lane_a, v_p[i], u[i])))
            y = z[i][C:] + jnp.where(lane_a, ry[:C], ry[C:])
            upd = _dot_tn(jnp.concatenate([u[i], v_p[i]], axis=0).astype(BF16), wa[i])
            s_ref[b, j] = (s_ref[b, j] + jnp.where(diag_blk, upd, 0.0)) * gamma_end[b * CP + ci][:, sl]

            yc = y - seg_sum(y) * (1.0 / D)
            var = seg_sum(yc * yc) * (1.0 / D)
            yn = yc * lax.rsqrt(var + GN_EPS) * gng_ref[:, sl] + gnb_ref[:, sl]
            bonus = seg_sum(rkk[rs, sl]) * v_p[i]
            y_ref[b, ci * C:(ci + 1) * C, sl] = ((yn + bonus) * gsilu[rs, sl]).astype(BF16)

    for ci in range(CP):
        chunk_chain(ci)


def _rwkv_branch(rw, batch, seq, w0, w2, a0, a2, k_k, k_a, r_k, gn_gain, gn_bias):
    C = RWKV_CHUNK
    NB = RWKV_BATCH_ROWS
    CP = RWKV_CHUNKS_PER_STEP
    rw3 = rw.reshape(batch, seq, RW_COLS)
    const = lambda shape: pl.BlockSpec(shape, lambda b, c: (0, 0))
    col_block = lambda j: pl.BlockSpec((NB, CP * C, RWKV_DIM), lambda b, c: (b, c, j))
    return pl.pallas_call(
        _rwkv_kernel,
        grid=(batch // NB, seq // (CP * C)),
        in_specs=[
            col_block(0), col_block(1), col_block(2), col_block(3),
            pl.BlockSpec((NB, CP * C, SMALL_WIDTH), lambda b, c: (b, c, 4 * RWKV_DIM // SMALL_WIDTH)),
            const((1, RWKV_DIM)), const((LORA, RWKV_DIM)), const((1, RWKV_DIM)), const((LORA, RWKV_DIM)),
            const((1, RWKV_DIM)), const((1, RWKV_DIM)), const((1, RWKV_DIM)),
            const((1, RWKV_DIM)), const((1, RWKV_DIM)),
        ],
        out_specs=pl.BlockSpec((NB, CP * C, RWKV_DIM), lambda b, c: (b, c, 0)),
        out_shape=jax.ShapeDtypeStruct((batch, seq, RWKV_DIM), BF16),
        scratch_shapes=[
            pltpu.VMEM((NB, RWKV_HEADS // 2, 2 * RWKV_HEAD_DIM, 2 * RWKV_HEAD_DIM), F32),
        ],
        compiler_params=pltpu.CompilerParams(
            dimension_semantics=("arbitrary", "arbitrary"), vmem_limit_bytes=VMEM_LIMIT),
        name="rwkv_branch",
    )(rw3, rw3, rw3, rw3, rw3, w0, w2, a0, a2, k_k, k_a, r_k, gn_gain, gn_bias
      ).reshape(batch * seq, RWKV_DIM)


def _merge_kernel(x_ref, ys_ref, yr_ref, g0_ref, g1_ref, bg_ref, ws_ref, wr_ref, wo_ref, gain_ref, o_ref):
    g_ssm = _sigmoid(g0_ref[...] + bg_ref[:, 0:D_MODEL])
    g_rwkv = _sigmoid(g1_ref[...] + bg_ref[:, D_MODEL:2 * D_MODEL])
    merged = g_ssm * _dot(ys_ref[...], ws_ref[...]) + g_rwkv * _dot(yr_ref[...], wr_ref[...])
    out = _dot(merged.astype(BF16), wo_ref[...])
    ms = jnp.mean(out * out, axis=-1, keepdims=True)
    o_ref[...] = x_ref[...] + out * lax.rsqrt(ms + EPS) * gain_ref[...]


def _merge(x2d, y_ssm, y_rwkv, zg, b_gate, w_ssm, w_rwkv, w_out, post_gain):
    t = x2d.shape[0]
    tm = MERGE_TM
    const = lambda shape: pl.BlockSpec(shape, lambda i: (0, 0), pipeline_mode=pl.Buffered(1))
    return pl.pallas_call(
        _merge_kernel,
        grid=(t // tm,),
        in_specs=[
            pl.BlockSpec((tm, D_MODEL), lambda i: (i, 0)),
            pl.BlockSpec((tm, SSM_INNER), lambda i: (i, 0)),
            pl.BlockSpec((tm, RWKV_DIM), lambda i: (i, 0)),
            pl.BlockSpec((tm, D_MODEL), lambda i: (i, SSM_INNER // D_MODEL)),
            pl.BlockSpec((tm, D_MODEL), lambda i: (i, SSM_INNER // D_MODEL + 1)),
            const((1, 2 * D_MODEL)),
            const((SSM_INNER, D_MODEL)), const((RWKV_DIM, D_MODEL)), const((D_MODEL, D_MODEL)),
            const((1, D_MODEL)),
        ],
        out_specs=pl.BlockSpec((tm, D_MODEL), lambda i: (i, 0)),
        out_shape=jax.ShapeDtypeStruct((t, D_MODEL), F32),
        compiler_params=pltpu.CompilerParams(
            dimension_semantics=("arbitrary",), vmem_limit_bytes=VMEM_LIMIT),
        name="gated_merge",
    )(x2d, y_ssm, y_rwkv, zg, zg, b_gate, w_ssm, w_rwkv, w_out, post_gain)


def _pad_lanes(v, width):
    return jnp.pad(v, ((0, 0), (0, width - v.shape[-1])))


def _layer(x, pre_gain, w_in, b_gate, conv_w, conv_b, dt_bias, a_log, d_skip, ssm_norm_gain,
           rwkv_mu, decay_w0, decay_w2, iclr_a0, iclr_a2, k_k, k_a, r_k, gn_gain, gn_bias,
           w_branch_ssm, w_branch_rwkv, w_out, post_gain):
    batch, seq, _ = x.shape
    x2d = x.reshape(batch * seq, D_MODEL)
    row = lambda v: v.reshape(1, -1)

    o_xbc, o_dt, o_rw = SSM_INNER, SSM_INNER + SSM_XBC, SSM_INNER + SSM_XBC + SSM_HEADS
    o_lora, o_gate = o_rw + 4 * RWKV_DIM, o_rw + 4 * RWKV_DIM + 2 * LORA
    pad = SMALL_WIDTH - 2 * LORA - SSM_HEADS
    w_xbc = w_in[:, o_xbc:o_dt].astype(BF16)
    w_rw = jnp.concatenate([w_in[:, o_rw:o_gate], w_in[:, o_dt:o_rw], jnp.zeros((D_MODEL, pad + RW_PAD), w_in.dtype)],
                           axis=1).astype(BF16)
    w_zg = jnp.concatenate([w_in[:, :o_xbc], w_in[:, o_gate:]], axis=1).astype(BF16)
    mu = _pad_lanes(row(rwkv_mu), RW_COLS)

    gain = row(pre_gain)
    u, h = _project_normed(_proj_conv_kernel, seq, x2d, gain, w_xbc, [conv_w, row(conv_b)])
    rw = _project(_proj_shift_kernel, seq, h, w_rw, [mu], True)
    zg = _project(_proj_plain_kernel, seq, h, w_zg, [], False)

    head_of_col = jnp.arange(SSM_INNER) // SSM_HEAD_DIM
    expand = ((jnp.arange(LANES)[:, None] % SSM_HEADS == head_of_col[None, :])
              & (jnp.arange(LANES)[:, None] < 2 * SSM_HEADS)).astype(BF16)
    y_ssm = _ssd_branch(
        u, zg, rw, batch, seq, _pad_lanes(row(dt_bias), LANES), _pad_lanes(row(a_log), LANES),
        row(jnp.repeat(d_skip, SSM_HEAD_DIM)), row(ssm_norm_gain), expand)

    y_rwkv = _rwkv_branch(
        rw, batch, seq, row(decay_w0), decay_w2.astype(BF16), row(iclr_a0), iclr_a2.astype(BF16),
        row(k_k), row(k_a), row(r_k), row(gn_gain), row(gn_bias))

    out = _merge(x2d, y_ssm, y_rwkv, zg, row(b_gate), w_branch_ssm.astype(BF16),
                 w_branch_rwkv.astype(BF16), w_out.astype(BF16), row(post_gain))
    return out.reshape(batch, seq, D_MODEL)


def kernel(x, pre_gain, w_in, b_gate, conv_w, conv_b, dt_bias, a_log, d_skip, ssm_norm_gain, rwkv_mu,
           decay_w0, decay_w2, iclr_a0, iclr_a2, k_k, k_a, r_k, gn_gain, gn_bias, w_branch_ssm,
           w_branch_rwkv, w_out, post_gain):
    for layer in range(pre_gain.shape[0]):
        x = _layer(
            x, pre_gain[layer], w_in[layer], b_gate[layer], conv_w[layer], conv_b[layer], dt_bias[layer],
            a_log[layer], d_skip[layer], ssm_norm_gain[layer], rwkv_mu[layer], decay_w0[layer],
            decay_w2[layer], iclr_a0[layer], iclr_a2[layer], k_k[layer], k_a[layer], r_k[layer],
            gn_gain[layer], gn_bias[layer], w_branch_ssm[layer], w_branch_rwkv[layer], w_out[layer],
            post_gain[layer])
    return x
```

```python
import functools

import jax
import jax.numpy as jnp
from jax import lax
from jax.experimental import pallas as pl
from jax.experimental.pallas import tpu as pltpu

D_MODEL = 1024
EPS = 1e-6

SSM_INNER = 2048
SSM_HEAD_DIM = 64
SSM_HEADS = 32
SSM_STATE = 128
SSM_GROUPS = 4
SSM_GROUP_WIDTH = SSM_INNER // SSM_GROUPS
CONV_WIDTH = 4
SSM_XBC = 3072

RWKV_DIM = 1024
RWKV_HEAD_DIM = 64
RWKV_HEADS = 16
LORA = 64
GN_EPS = RWKV_HEAD_DIM * 1e-5
DECAY_SCALE = 0.6065306597126334

LANES = 128
SUBLANES = 8

SMALL_WIDTH = 256
RW_PAD = 256
RW_COLS = 4 * RWKV_DIM + SMALL_WIDTH + RW_PAD
ZG_COLS = SSM_INNER + 2 * D_MODEL

PROJ_TM = {SSM_XBC: 1024, RW_COLS: 2048, ZG_COLS: 2048}
PROJ_SUBTILE = 256
PROJ_TN = {SSM_XBC: 1024, RW_COLS: RW_COLS // 6, ZG_COLS: 1024}
SSD_CHUNK = 128
SSD_CHUNKS_PER_STEP = 4
RWKV_CHUNK = 64
RWKV_BATCH_ROWS = 2
RWKV_CHUNKS_PER_STEP = 2
MERGE_TM = 512
VMEM_LIMIT = 48 * 1024 * 1024

BF16 = jnp.bfloat16
F32 = jnp.float32


def _dot(a, b):
    return jnp.dot(a, b, preferred_element_type=F32)


def _dot_nt(a, b):
    return lax.dot_general(a, b, (((1,), (1,)), ((), ())), preferred_element_type=F32)


def _dot_tn(a, b):
    return lax.dot_general(a, b, (((0,), (0,)), ((), ())), preferred_element_type=F32)


def _split_terms(x, n):
    terms = []
    rem = x
    for _ in range(n):
        t = rem.astype(BF16)
        terms.append(t)
        rem = rem - t.astype(F32)
    return terms


def _dot_left_exact(m_bf16, x, n=3):
    acc = None
    for t in _split_terms(x, n):
        p = _dot(m_bf16, t)
        acc = p if acc is None else acc + p
    return acc


def _two_terms_on_lanes(x):
    lane = lax.broadcasted_iota(jnp.int32, x.shape, 1)
    xm = jnp.where(lane < SSM_HEADS, x, 0.0)
    hi = xm.astype(BF16).astype(F32)
    return (hi + pltpu.roll(xm - hi, SSM_HEADS, 1)).astype(BF16)


def _softplus(x):
    return jnp.maximum(x, 0.0) + jnp.log(1.0 + jnp.exp(-jnp.abs(x)))


def _sigmoid(x):
    return 0.5 * jnp.tanh(0.5 * x) + 0.5


def _silu(x):
    h = 0.5 * x
    return h + h * jnp.tanh(h)


def _lower_tri(n, strict):
    row = lax.broadcasted_iota(jnp.int32, (n, n), 0)
    col = lax.broadcasted_iota(jnp.int32, (n, n), 1)
    return (col < row) if strict else (col <= row)


def _lower_tri_ones(n):
    row = lax.broadcasted_iota(jnp.int32, (n, n), 0)
    col = lax.broadcasted_iota(jnp.int32, (n, n), 1)
    return jnp.clip(row - col + 1, 0, 1).astype(F32).astype(BF16)


def _normed_input(x_ref, gain_ref, h_ref):
    @pl.when(pl.program_id(1) == 0)
    def _():
        x = x_ref[...]
        ms = jnp.mean(x * x, axis=-1, keepdims=True)
        h_ref[...] = (x * lax.rsqrt(ms + EPS) * gain_ref[...]).astype(BF16)


def _delayed(o, halo, k):
    head = jnp.concatenate([halo, o[:SUBLANES]], axis=0)[SUBLANES - k:2 * SUBLANES - k]
    if o.shape[0] == SUBLANES:
        return head
    return jnp.concatenate([head, pltpu.roll(o, k, 0)[SUBLANES:]], axis=0)


def _zero_carry_at_sequence_start(carry_ref, row_blocks_per_seq):
    i, j = pl.program_id(0), pl.program_id(1)

    @pl.when(i % row_blocks_per_seq == 0)
    def _():
        carry_ref[j] = jnp.zeros(carry_ref.shape[1:], F32)


def _proj_plain_kernel(h_ref, w_ref, o_ref):
    o_ref[...] = _dot(h_ref[...], w_ref[...])


def _proj_conv_kernel(row_blocks_per_seq, x_ref, gain_ref, w_ref, cw_ref, cb_ref, o_ref, h_ref, carry_ref):
    _normed_input(x_ref, gain_ref, h_ref)
    _zero_carry_at_sequence_start(carry_ref, row_blocks_per_seq)
    j = pl.program_id(1)
    for c0 in range(0, o_ref.shape[1], PROJ_SUBTILE):
        cs = slice(c0, c0 + PROJ_SUBTILE)
        o = _dot(h_ref[...], w_ref[:, cs])
        halo = carry_ref[j, :, cs]
        carry_ref[j, :, cs] = o[o.shape[0] - SUBLANES:, :]
        w0, w1, w2, w3 = (0.5 * cw_ref[t:t + 1, cs] for t in range(CONV_WIDTH))
        e = _delayed(o, halo, 1)
        e_halo = _delayed(halo, jnp.zeros_like(halo), 1)
        inner = w1 * o + w0 * e
        inner_halo = w1 * halo + w0 * e_halo
        h = 0.5 * cb_ref[:, cs] + w3 * o + w2 * e + _delayed(inner, inner_halo, 2)
        o_ref[:, cs] = h + h * jnp.tanh(h)


def _proj_shift_kernel(row_blocks_per_seq, h_ref, w_ref, mu_ref, o_ref, carry_ref):
    _zero_carry_at_sequence_start(carry_ref, row_blocks_per_seq)
    j = pl.program_id(1)
    for c0 in range(0, o_ref.shape[1], PROJ_SUBTILE):
        cs = slice(c0, c0 + PROJ_SUBTILE)
        o = _dot(h_ref[...], w_ref[:, cs])
        halo = carry_ref[j, :, cs]
        carry_ref[j, :, cs] = o[o.shape[0] - SUBLANES:, :]
        o_ref[:, cs] = o + mu_ref[:, cs] * (_delayed(o, halo, 1) - o)


def _project_normed(kernel_fn, seq, x2d, pre_gain, w, col_params):
    t, n = x2d.shape[0], w.shape[1]
    tm, tn = PROJ_TM[n], PROJ_TN[n]
    return pl.pallas_call(
        functools.partial(kernel_fn, seq // tm),
        grid=(t // tm, n // tn),
        in_specs=[
            pl.BlockSpec((tm, D_MODEL), lambda i, j: (i, 0)),
            pl.BlockSpec((1, D_MODEL), lambda i, j: (0, 0)),
            pl.BlockSpec((D_MODEL, tn), lambda i, j: (0, j)),
        ] + [pl.BlockSpec((c.shape[0], tn), lambda i, j: (0, j)) for c in col_params],
        out_specs=[pl.BlockSpec((tm, tn), lambda i, j: (i, j)),
                   pl.BlockSpec((tm, D_MODEL), lambda i, j: (i, 0))],
        out_shape=[jax.ShapeDtypeStruct((t, n), F32), jax.ShapeDtypeStruct((t, D_MODEL), BF16)],
        scratch_shapes=[pltpu.VMEM((n // tn, SUBLANES, tn), F32)],
        compiler_params=pltpu.CompilerParams(
            dimension_semantics=("arbitrary", "arbitrary"), vmem_limit_bytes=VMEM_LIMIT),
        name="in_proj_%d" % n,
    )(x2d, pre_gain, w, *col_params)


def _project(kernel_fn, seq, h, w, col_params, carry):
    t, n = h.shape[0], w.shape[1]
    tm, tn = PROJ_TM[n], PROJ_TN[n]
    return pl.pallas_call(
        functools.partial(kernel_fn, seq // tm) if carry else kernel_fn,
        grid=(t // tm, n // tn),
        in_specs=[
            pl.BlockSpec((tm, D_MODEL), lambda i, j: (i, 0)),
            pl.BlockSpec((D_MODEL, tn), lambda i, j: (0, j)),
        ] + [pl.BlockSpec((c.shape[0], tn), lambda i, j: (0, j)) for c in col_params],
        out_specs=pl.BlockSpec((tm, tn), lambda i, j: (i, j)),
        out_shape=jax.ShapeDtypeStruct((t, n), F32),
        scratch_shapes=[pltpu.VMEM((n // tn, SUBLANES, tn), F32)] if carry else [],
        compiler_params=pltpu.CompilerParams(
            dimension_semantics=("arbitrary", "arbitrary"), vmem_limit_bytes=VMEM_LIMIT),
        name="in_proj_%d" % n,
    )(h, w, *col_params)


def _ssd_kernel(u_ref, z_ref, sm_ref, dtb_ref, alog_ref, dskip_ref, gain_ref, expand_ref,
                y_ref, st_ref, yd_ref):
    L = SSD_CHUNK

    @pl.when(pl.program_id(1) == 0)
    def _():
        st_ref[...] = jnp.zeros_like(st_ref)

    tri = _lower_tri(L, strict=False)
    tri_ones = _lower_tri_ones(L)
    head_a = lax.broadcasted_iota(jnp.int32, (L, LANES), 1) < SSM_HEAD_DIM
    for ci in range(SSD_CHUNKS_PER_STEP):
        rows = slice(ci * L, (ci + 1) * L)
        dt = _softplus(sm_ref[rows, LANES:2 * LANES] + dtb_ref[...])
        a = dt * (-jnp.exp(alog_ref[...]))
        acum = _dot_left_exact(tri_ones, a)
        acum_t = acum.T
        dt_t = dt.T
        a_last = acum[L - 1:L, :]
        dec_out_p = _two_terms_on_lanes(jnp.exp(acum))
        w_state_p = _two_terms_on_lanes(dt * jnp.exp(a_last - acum))

        for g in range(SSM_GROUPS):
            c0 = g * SSM_GROUP_WIDTH
            ex = expand_ref[:, c0:c0 + SSM_GROUP_WIDTH]
            xs = u_ref[rows, c0:c0 + SSM_GROUP_WIDTH]
            b0 = SSM_INNER + g * SSM_STATE
            bm = u_ref[rows, b0:b0 + SSM_STATE].astype(BF16)
            cm = u_ref[rows, b0 + SSM_GROUPS * SSM_STATE:b0 + (SSM_GROUPS + 1) * SSM_STATE].astype(BF16)
            dec_out_x = _dot(dec_out_p, ex)
            w_state_x = _dot(w_state_p, ex)
            xs_b = xs.astype(BF16)

            scores = _dot_nt(cm, bm)
            state = st_ref[g]
            y_off = _dot(cm, state.astype(BF16)) * dec_out_x
            st_ref[g] = state * dec_out_x[L - 1:L, :] + _dot_tn(bm, (xs * w_state_x).astype(BF16))

            def mix_of(h):
                seg = acum[:, h:h + 1] - acum_t[h:h + 1, :]
                return (scores * jnp.exp(jnp.where(tri, seg, -jnp.inf)) * dt_t[h:h + 1, :]).astype(BF16)

            for jp in range(SSM_GROUP_WIDTH // LANES):
                h = g * (SSM_GROUP_WIDTH // SSM_HEAD_DIM) + 2 * jp
                xp = xs_b[:, jp * LANES:(jp + 1) * LANES]
                x2 = jnp.concatenate([jnp.where(head_a, xp, 0.0), jnp.where(head_a, 0.0, xp)], axis=0)
                yd_ref[ci, :, jp * LANES:(jp + 1) * LANES] = _dot(
                    jnp.concatenate([mix_of(h), mix_of(h + 1)], axis=1), x2)

            y = yd_ref[ci] + y_off + dskip_ref[:, c0:c0 + SSM_GROUP_WIDTH] * xs
            z = z_ref[rows, c0:c0 + SSM_GROUP_WIDTH]
            y = y * _silu(z)
            ms = jnp.mean(y * y, axis=-1, keepdims=True)
            y_ref[rows, c0:c0 + SSM_GROUP_WIDTH] = (
                y * lax.rsqrt(ms + EPS) * gain_ref[:, c0:c0 + SSM_GROUP_WIDTH]).astype(BF16)


def _ssd_branch(u, zg, rw, batch, seq, dt_bias, a_log, dskip_x, norm_gain, expand):
    L = SSD_CHUNK * SSD_CHUNKS_PER_STEP
    nc = seq // L
    row = lambda b, c: b * nc + c
    const = lambda shape: pl.BlockSpec(shape, lambda b, c: (0, 0))
    return pl.pallas_call(
        _ssd_kernel,
        grid=(batch, nc),
        in_specs=[
            pl.BlockSpec((L, SSM_XBC), lambda b, c: (row(b, c), 0)),
            pl.BlockSpec((L, SSM_INNER), lambda b, c: (row(b, c), 0)),
            pl.BlockSpec((L, SMALL_WIDTH), lambda b, c: (row(b, c), 4 * RWKV_DIM // SMALL_WIDTH)),
            const((1, LANES)),
            const((1, LANES)),
            const((1, SSM_INNER)),
            const((1, SSM_INNER)),
            const((LANES, SSM_INNER)),
        ],
        out_specs=pl.BlockSpec((L, SSM_INNER), lambda b, c: (row(b, c), 0)),
        out_shape=jax.ShapeDtypeStruct((batch * seq, SSM_INNER), BF16),
        scratch_shapes=[
            pltpu.VMEM((SSM_GROUPS, SSM_STATE, SSM_GROUP_WIDTH), F32),
            pltpu.VMEM((SSD_CHUNKS_PER_STEP, SSD_CHUNK, SSM_GROUP_WIDTH), F32),
        ],
        compiler_params=pltpu.CompilerParams(
            dimension_semantics=("arbitrary", "arbitrary"), vmem_limit_bytes=VMEM_LIMIT),
        name="ssd_branch",
    )(u, zg, rw, dt_bias, a_log, dskip_x, norm_gain, expand)


def _rwkv_kernel(r_ref, k_ref, v_ref, g_ref, sm_ref, w0_ref, w2_ref, a0_ref, a2_ref, kks_ref, kas_ref,
                 rk_ref, gng_ref, gnb_ref, y_ref, s_ref):
    C = RWKV_CHUNK
    D = RWKV_HEAD_DIM
    NB = RWKV_BATCH_ROWS
    CP = RWKV_CHUNKS_PER_STEP
    R = NB * CP * C

    @pl.when(pl.program_id(1) == 0)
    def _():
        s_ref[...] = jnp.zeros_like(s_ref)

    rows = lambda x_ref: x_ref[...].reshape(R, x_ref.shape[-1])
    r, k, v, gate, sm = rows(r_ref), rows(k_ref), rows(v_ref), rows(g_ref), rows(sm_ref)
    wd = sm[:, 0:LORA]
    ad = sm[:, LORA:2 * LORA]

    lw = _dot(jnp.tanh(wd).astype(BF16), w2_ref[...])
    logw = (-0.5 * DECAY_SCALE) * jnp.tanh(0.5 * (w0_ref[...] + lw)) - 0.5 * DECAY_SCALE
    neg_a = -0.5 * jnp.tanh(0.5 * (a0_ref[...] + _dot(ad.astype(BF16), a2_ref[...]))) - 0.5
    kkp = k * kks_ref[...]
    k2 = k * (1.0 - (neg_a + 1.0) * kas_ref[...])

    tri = _lower_tri_ones(C)
    cum = jnp.concatenate([_dot_left_exact(tri, logw[q * C:(q + 1) * C]) for q in range(NB * CP)], axis=0)
    e_pos = jnp.exp(cum)
    e_neg = 1.0 / e_pos
    kc_p = kkp * jnp.exp(cum - logw)
    rt = r * e_pos
    kt = k2 * e_neg
    nbt_p = kkp * neg_a * e_neg
    gamma_end = [jnp.exp(cum[(q + 1) * C - 1:(q + 1) * C, :]) for q in range(NB * CP)]
    rkk = r * k2 * rk_ref[...]
    gsilu = _silu(gate)

    P = 2 * D
    lane_a = lax.broadcasted_iota(jnp.int32, (C, P), 1) < D
    lane_a2 = lax.broadcasted_iota(jnp.int32, (2 * C, P), 1) < D
    row2 = lax.broadcasted_iota(jnp.int32, (2 * C, 2 * C), 0)
    col2 = lax.broadcasted_iota(jnp.int32, (2 * C, 2 * C), 1)
    colm = jnp.where(col2 >= C, col2 - C, col2)
    gmask = colm < jnp.where(row2 >= C, row2 - C + 1, row2)
    diag_blk = jnp.where(row2 >= C, 1, 0) == jnp.where(col2 >= C, 1, 0)
    anti_eye = jnp.where(col2 == jnp.where(row2 >= C, row2 - C, row2 + C), 1.0, 0.0)

    def seg_sum(x):
        sa = jnp.sum(jnp.where(lane_a, x, 0.0), axis=-1, keepdims=True)
        sb = jnp.sum(jnp.where(lane_a, 0.0, x), axis=-1, keepdims=True)
        return jnp.where(lane_a, sa, sb)

    def cross(lo, hi):
        return jnp.concatenate([lo, hi], axis=0).astype(BF16)

    units = [(b, j) for b in range(NB) for j in range(RWKV_HEADS // 2)]
    n_units = range(len(units))

    def chunk_chain(ci):
        idx = [(slice((b * CP + ci) * C, (b * CP + ci + 1) * C), slice(j * P, (j + 1) * P)) for b, j in units]
        xb, wa, g_top, g_bot, v_p = [], [], [], [], []
        for rs, sl in idx:
            kkp_p = kkp[rs, sl]
            rinv = jnp.minimum(lax.rsqrt(seg_sum(kkp_p * kkp_p)), 1e12)
            nbt = nbt_p[rs, sl] * rinv
            xp = jnp.concatenate([kc_p[rs, sl] * rinv, rt[rs, sl]], axis=0)
            w_a = jnp.concatenate([nbt, kt[rs, sl]], axis=0).astype(BF16)
            w_b = jnp.concatenate([kt[rs, sl], nbt], axis=0).astype(BF16)
            g_a = jnp.where(gmask, _dot_nt(jnp.where(lane_a2, xp, 0.0).astype(BF16), w_a), 0.0)
            g_b = jnp.where(gmask, _dot_nt(jnp.where(lane_a2, 0.0, xp).astype(BF16), w_b), 0.0)
            xb.append(xp.astype(BF16))
            wa.append(w_a)
            g_top.append(jnp.concatenate([g_a[:C], g_b[:C]], axis=0))
            g_bot.append(jnp.concatenate([g_a[C:], g_b[C:]], axis=0))
            v_p.append(v[rs, sl])

        z = [_dot_nt(xb[i], s_ref[units[i][0], units[i][1]].astype(BF16)) for i in n_units]
        w1 = []
        for i in n_units:
            rw = _dot(g_top[i].astype(BF16),
                      cross(jnp.where(lane_a, 0.0, v_p[i]), jnp.where(lane_a, v_p[i], 0.0)))
            w1.append(jnp.where(lane_a, rw[:C], rw[C:]))

        pt = [jnp.where(diag_blk, g_top[i], anti_eye) for i in n_units]
        for _ in range(6):
            pt = [_dot(jnp.where(diag_blk, pt[i], 0.0).astype(BF16), pt[i].astype(BF16))
                  + jnp.where(diag_blk, 0.0, pt[i]) for i in n_units]

        u = []
        for i in n_units:
            rhs = z[i][:C] + w1[i]
            ru = _dot(pt[i].astype(BF16), cross(jnp.where(lane_a, 0.0, rhs), jnp.where(lane_a, rhs, 0.0)))
            u.append(ru[:C] + ru[C:])
        for i in n_units:
            (b, j), (rs, sl) = units[i], idx[i]
            ry = _dot(g_bot[i].astype(BF16),
                      cross(jnp.where(lane_a, u[i], v_p[i]), jnp.where(lane_a, v_p[i], u[i])))
            y = z[i][C:] + jnp.where(lane_a, ry[:C], ry[C:])
            upd = _dot_tn(jnp.concatenate([u[i], v_p[i]], axis=0).astype(BF16), wa[i])
            s_ref[b, j] = (s_ref[b, j] + jnp.where(diag_blk, upd, 0.0)) * gamma_end[b * CP + ci][:, sl]

            yc = y - seg_sum(y) * (1.0 / D)
            var = seg_sum(yc * yc) * (1.0 / D)
            yn = yc * lax.rsqrt(var + GN_EPS) * gng_ref[:, sl] + gnb_ref[:, sl]
            bonus = seg_sum(rkk[rs, sl]) * v_p[i]
            y_ref[b, ci * C:(ci + 1) * C, sl] = ((yn + bonus) * gsilu[rs, sl]).astype(BF16)

    for ci in range(CP):
        chunk_chain(ci)


def _rwkv_branch(rw, batch, seq, w0, w2, a0, a2, k_k, k_a, r_k, gn_gain, gn_bias):
    C = RWKV_CHUNK
    NB = RWKV_BATCH_ROWS
    CP = RWKV_CHUNKS_PER_STEP
    rw3 = rw.reshape(batch, seq, RW_COLS)
    const = lambda shape: pl.BlockSpec(shape, lambda b, c: (0, 0))
    col_block = lambda j: pl.BlockSpec((NB, CP * C, RWKV_DIM), lambda b, c: (b, c, j))
    return pl.pallas_call(
        _rwkv_kernel,
        grid=(batch // NB, seq // (CP * C)),
        in_specs=[
            col_block(0), col_block(1), col_block(2), col_block(3),
            pl.BlockSpec((NB, CP * C, SMALL_WIDTH), lambda b, c: (b, c, 4 * RWKV_DIM // SMALL_WIDTH)),
            const((1, RWKV_DIM)), const((LORA, RWKV_DIM)), const((1, RWKV_DIM)), const((LORA, RWKV_DIM)),
            const((1, RWKV_DIM)), const((1, RWKV_DIM)), const((1, RWKV_DIM)),
            const((1, RWKV_DIM)), const((1, RWKV_DIM)),
        ],
        out_specs=pl.BlockSpec((NB, CP * C, RWKV_DIM), lambda b, c: (b, c, 0)),
        out_shape=jax.ShapeDtypeStruct((batch, seq, RWKV_DIM), BF16),
        scratch_shapes=[
            pltpu.VMEM((NB, RWKV_HEADS // 2, 2 * RWKV_HEAD_DIM, 2 * RWKV_HEAD_DIM), F32),
        ],
        compiler_params=pltpu.CompilerParams(
            dimension_semantics=("arbitrary", "arbitrary"), vmem_limit_bytes=VMEM_LIMIT),
        name="rwkv_branch",
    )(rw3, rw3, rw3, rw3, rw3, w0, w2, a0, a2, k_k, k_a, r_k, gn_gain, gn_bias
      ).reshape(batch * seq, RWKV_DIM)


def _merge_kernel(x_ref, ys_ref, yr_ref, g0_ref, g1_ref, bg_ref, ws_ref, wr_ref, wo_ref, gain_ref, o_ref):
    g_ssm = _sigmoid(g0_ref[...] + bg_ref[:, 0:D_MODEL])
    g_rwkv = _sigmoid(g1_ref[...] + bg_ref[:, D_MODEL:2 * D_MODEL])
    merged = g_ssm * _dot(ys_ref[...], ws_ref[...]) + g_rwkv * _dot(yr_ref[...], wr_ref[...])
    out = _dot(merged.astype(BF16), wo_ref[...])
    ms = jnp.mean(out * out, axis=-1, keepdims=True)
    o_ref[...] = x_ref[...] + out * lax.rsqrt(ms + EPS) * gain_ref[...]


def _merge(x2d, y_ssm, y_rwkv, zg, b_gate, w_ssm, w_rwkv, w_out, post_gain):
    t = x2d.shape[0]
    tm = MERGE_TM
    const = lambda shape: pl.BlockSpec(shape, lambda i: (0, 0), pipeline_mode=pl.Buffered(1))
    return pl.pallas_call(
        _merge_kernel,
        grid=(t // tm,),
        in_specs=[
            pl.BlockSpec((tm, D_MODEL), lambda i: (i, 0)),
            pl.BlockSpec((tm, SSM_INNER), lambda i: (i, 0)),
            pl.BlockSpec((tm, RWKV_DIM), lambda i: (i, 0)),
            pl.BlockSpec((tm, D_MODEL), lambda i: (i, SSM_INNER // D_MODEL)),
            pl.BlockSpec((tm, D_MODEL), lambda i: (i, SSM_INNER // D_MODEL + 1)),
            const((1, 2 * D_MODEL)),
            const((SSM_INNER, D_MODEL)), const((RWKV_DIM, D_MODEL)), const((D_MODEL, D_MODEL)),
            const((1, D_MODEL)),
        ],
        out_specs=pl.BlockSpec((tm, D_MODEL), lambda i: (i, 0)),
        out_shape=jax.ShapeDtypeStruct((t, D_MODEL), F32),
        compiler_params=pltpu.CompilerParams(
            dimension_semantics=("arbitrary",), vmem_limit_bytes=VMEM_LIMIT),
        name="gated_merge",
    )(x2d, y_ssm, y_rwkv, zg, zg, b_gate, w_ssm, w_rwkv, w_out, post_gain)


def _pad_lanes(v, width):
    return jnp.pad(v, ((0, 0), (0, width - v.shape[-1])))


def _layer(x, pre_gain, w_in, b_gate, conv_w, conv_b, dt_bias, a_log, d_skip, ssm_norm_gain,
           rwkv_mu, decay_w0, decay_w2, iclr_a0, iclr_a2, k_k, k_a, r_k, gn_gain, gn_bias,
           w_branch_ssm, w_branch_rwkv, w_out, post_gain):
    batch, seq, _ = x.shape
    x2d = x.reshape(batch * seq, D_MODEL)
    row = lambda v: v.reshape(1, -1)

    o_xbc, o_dt, o_rw = SSM_INNER, SSM_INNER + SSM_XBC, SSM_INNER + SSM_XBC + SSM_HEADS
    o_lora, o_gate = o_rw + 4 * RWKV_DIM, o_rw + 4 * RWKV_DIM + 2 * LORA
    pad = SMALL_WIDTH - 2 * LORA - SSM_HEADS
    w_xbc = w_in[:, o_xbc:o_dt].astype(BF16)
    w_rw = jnp.concatenate([w_in[:, o_rw:o_gate], w_in[:, o_dt:o_rw], jnp.zeros((D_MODEL, pad + RW_PAD), w_in.dtype)],
                           axis=1).astype(BF16)
    w_zg = jnp.concatenate([w_in[:, :o_xbc], w_in[:, o_gate:]], axis=1).astype(BF16)
    mu = _pad_lanes(row(rwkv_mu), RW_COLS)

    gain = row(pre_gain)
    u, h = _project_normed(_proj_conv_kernel, seq, x2d, gain, w_xbc, [conv_w, row(conv_b)])
    rw = _project(_proj_shift_kernel, seq, h, w_rw, [mu], True)
    zg = _project(_proj_plain_kernel, seq, h, w_zg, [], False)

    head_of_col = jnp.arange(SSM_INNER) // SSM_HEAD_DIM
    expand = ((jnp.arange(LANES)[:, None] % SSM_HEADS == head_of_col[None, :])
              & (jnp.arange(LANES)[:, None] < 2 * SSM_HEADS)).astype(BF16)
    y_ssm = _ssd_branch(
        u, zg, rw, batch, seq, _pad_lanes(row(dt_bias), LANES), _pad_lanes(row(a_log), LANES),
        row(jnp.repeat(d_skip, SSM_HEAD_DIM)), row(ssm_norm_gain), expand)

    y_rwkv = _rwkv_branch(
        rw, batch, seq, row(decay_w0), decay_w2.astype(BF16), row(iclr_a0), iclr_a2.astype(BF16),
        row(k_k), row(k_a), row(r_k), row(gn_gain), row(gn_bias))

    out = _merge(x2d, y_ssm, y_rwkv, zg, row(b_gate), w_branch_ssm.astype(BF16),
                 w_branch_rwkv.astype(BF16), w_out.astype(BF16), row(post_gain))
    return out.reshape(batch, seq, D_MODEL)


def kernel(x, pre_gain, w_in, b_gate, conv_w, conv_b, dt_bias, a_log, d_skip, ssm_norm_gain, rwkv_mu,
           decay_w0, decay_w2, iclr_a0, iclr_a2, k_k, k_a, r_k, gn_gain, gn_bias, w_branch_ssm,
           w_branch_rwkv, w_out, post_gain):
    for layer in range(pre_gain.shape[0]):
        x = _layer(
            x, pre_gain[layer], w_in[layer], b_gate[layer], conv_w[layer], conv_b[layer], dt_bias[layer],
            a_log[layer], d_skip[layer], ssm_norm_gain[layer], rwkv_mu[layer], decay_w0[layer],
            decay_w2[layer], iclr_a0[layer], iclr_a2[layer], k_k[layer], k_a[layer], r_k[layer],
            gn_gain[layer], gn_bias[layer], w_branch_ssm[layer], w_branch_rwkv[layer], w_out[layer],
            post_gain[layer])
    return x
```

```python
import functools

import jax
import jax.numpy as jnp
from jax import lax
from jax.experimental import pallas as pl
from jax.experimental.pallas import tpu as pltpu

D_MODEL = 1024
EPS = 1e-6

SSM_INNER = 2048
SSM_HEAD_DIM = 64
SSM_HEADS = 32
SSM_STATE = 128
SSM_GROUPS = 4
SSM_GROUP_WIDTH = SSM_INNER // SSM_GROUPS
CONV_WIDTH = 4
SSM_XBC = 3072

RWKV_DIM = 1024
RWKV_HEAD_DIM = 64
RWKV_HEADS = 16
LORA = 64
GN_EPS = RWKV_HEAD_DIM * 1e-5
DECAY_SCALE = 0.6065306597126334

LANES = 128
SUBLANES = 8

SMALL_WIDTH = 256
RW_PAD = 256
RW_COLS = 4 * RWKV_DIM + SMALL_WIDTH + RW_PAD
ZG_COLS = SSM_INNER + 2 * D_MODEL

PROJ_TM = {SSM_XBC: 1024, RW_COLS: 2048, ZG_COLS: 2048}
PROJ_SUBTILE = 256
PROJ_TN = {SSM_XBC: 1024, RW_COLS: RW_COLS // 6, ZG_COLS: 1024}
SSD_CHUNK = 128
SSD_CHUNKS_PER_STEP = 4
RWKV_CHUNK = 64
RWKV_BATCH_ROWS = 2
RWKV_CHUNKS_PER_STEP = 2
MERGE_TM = 512
VMEM_LIMIT = 48 * 1024 * 1024

BF16 = jnp.bfloat16
F32 = jnp.float32


def _dot(a, b):
    return jnp.dot(a, b, preferred_element_type=F32)


def _dot_nt(a, b):
    return lax.dot_general(a, b, (((1,), (1,)), ((), ())), preferred_element_type=F32)


def _dot_tn(a, b):
    return lax.dot_general(a, b, (((0,), (0,)), ((), ())), preferred_element_type=F32)


def _split_terms(x, n):
    terms = []
    rem = x
    for _ in range(n):
        t = rem.astype(BF16)
        terms.append(t)
        rem = rem - t.astype(F32)
    return terms


def _dot_left_exact(m_bf16, x, n=3):
    acc = None
    for t in _split_terms(x, n):
        p = _dot(m_bf16, t)
        acc = p if acc is None else acc + p
    return acc


def _two_terms_on_lanes(x):
    lane = lax.broadcasted_iota(jnp.int32, x.shape, 1)
    xm = jnp.where(lane < SSM_HEADS, x, 0.0)
    hi = xm.astype(BF16).astype(F32)
    return (hi + pltpu.roll(xm - hi, SSM_HEADS, 1)).astype(BF16)


def _softplus(x):
    return jnp.maximum(x, 0.0) + jnp.log(1.0 + jnp.exp(-jnp.abs(x)))


def _sigmoid(x):
    return 0.5 * jnp.tanh(0.5 * x) + 0.5


def _silu(x):
    h = 0.5 * x
    return h + h * jnp.tanh(h)


def _lower_tri(n, strict):
    row = lax.broadcasted_iota(jnp.int32, (n, n), 0)
    col = lax.broadcasted_iota(jnp.int32, (n, n), 1)
    return (col < row) if strict else (col <= row)


def _lower_tri_ones(n):
    row = lax.broadcasted_iota(jnp.int32, (n, n), 0)
    col = lax.broadcasted_iota(jnp.int32, (n, n), 1)
    return jnp.clip(row - col + 1, 0, 1).astype(F32).astype(BF16)


def _normed_input(x_ref, gain_ref, h_ref):
    @pl.when(pl.program_id(1) == 0)
    def _():
        x = x_ref[...]
        ms = jnp.mean(x * x, axis=-1, keepdims=True)
        h_ref[...] = (x * lax.rsqrt(ms + EPS) * gain_ref[...]).astype(BF16)


def _delayed(o, halo, k):
    head = jnp.concatenate([halo, o[:SUBLANES]], axis=0)[SUBLANES - k:2 * SUBLANES - k]
    if o.shape[0] == SUBLANES:
        return head
    return jnp.concatenate([head, pltpu.roll(o, k, 0)[SUBLANES:]], axis=0)


def _zero_carry_at_sequence_start(carry_ref, row_blocks_per_seq):
    i, j = pl.program_id(0), pl.program_id(1)

    @pl.when(i % row_blocks_per_seq == 0)
    def _():
        carry_ref[j] = jnp.zeros(carry_ref.shape[1:], F32)


def _proj_plain_kernel(h_ref, w_ref, o_ref):
    o_ref[...] = _dot(h_ref[...], w_ref[...])


def _proj_conv_kernel(row_blocks_per_seq, x_ref, gain_ref, w_ref, cw_ref, cb_ref, o_ref, h_ref, carry_ref):
    _normed_input(x_ref, gain_ref, h_ref)
    _zero_carry_at_sequence_start(carry_ref, row_blocks_per_seq)
    j = pl.program_id(1)
    for c0 in range(0, o_ref.shape[1], PROJ_SUBTILE):
        cs = slice(c0, c0 + PROJ_SUBTILE)
        o = _dot(h_ref[...], w_ref[:, cs])
        halo = carry_ref[j, :, cs]
        carry_ref[j, :, cs] = o[o.shape[0] - SUBLANES:, :]
        w0, w1, w2, w3 = (0.5 * cw_ref[t:t + 1, cs] for t in range(CONV_WIDTH))
        e = _delayed(o, halo, 1)
        e_halo = _delayed(halo, jnp.zeros_like(halo), 1)
        inner = w1 * o + w0 * e
        inner_halo = w1 * halo + w0 * e_halo
        h = 0.5 * cb_ref[:, cs] + w3 * o + w2 * e + _delayed(inner, inner_halo, 2)
        o_ref[:, cs] = h + h * jnp.tanh(h)


def _proj_shift_kernel(row_blocks_per_seq, h_ref, w_ref, mu_ref, o_ref, carry_ref):
    _zero_carry_at_sequence_start(carry_ref, row_blocks_per_seq)
    j = pl.program_id(1)
    o = _dot(h_ref[...], w_ref[...])
    halo = carry_ref[j]
    carry_ref[j] = o[o.shape[0] - SUBLANES:, :]
    o_ref[...] = o + mu_ref[...] * (_delayed(o, halo, 1) - o)


def _project_normed(kernel_fn, seq, x2d, pre_gain, w, col_params):
    t, n = x2d.shape[0], w.shape[1]
    tm, tn = PROJ_TM[n], PROJ_TN[n]
    return pl.pallas_call(
        functools.partial(kernel_fn, seq // tm),
        grid=(t // tm, n // tn),
        in_specs=[
            pl.BlockSpec((tm, D_MODEL), lambda i, j: (i, 0)),
            pl.BlockSpec((1, D_MODEL), lambda i, j: (0, 0)),
            pl.BlockSpec((D_MODEL, tn), lambda i, j: (0, j)),
        ] + [pl.BlockSpec((c.shape[0], tn), lambda i, j: (0, j)) for c in col_params],
        out_specs=[pl.BlockSpec((tm, tn), lambda i, j: (i, j)),
                   pl.BlockSpec((tm, D_MODEL), lambda i, j: (i, 0))],
        out_shape=[jax.ShapeDtypeStruct((t, n), F32), jax.ShapeDtypeStruct((t, D_MODEL), BF16)],
        scratch_shapes=[pltpu.VMEM((n // tn, SUBLANES, tn), F32)],
        compiler_params=pltpu.CompilerParams(
            dimension_semantics=("arbitrary", "arbitrary"), vmem_limit_bytes=VMEM_LIMIT),
        name="in_proj_%d" % n,
    )(x2d, pre_gain, w, *col_params)


def _project(kernel_fn, seq, h, w, col_params, carry):
    t, n = h.shape[0], w.shape[1]
    tm, tn = PROJ_TM[n], PROJ_TN[n]
    return pl.pallas_call(
        functools.partial(kernel_fn, seq // tm) if carry else kernel_fn,
        grid=(t // tm, n // tn),
        in_specs=[
            pl.BlockSpec((tm, D_MODEL), lambda i, j: (i, 0)),
            pl.BlockSpec((D_MODEL, tn), lambda i, j: (0, j)),
        ] + [pl.BlockSpec((c.shape[0], tn), lambda i, j: (0, j)) for c in col_params],
        out_specs=pl.BlockSpec((tm, tn), lambda i, j: (i, j)),
        out_shape=jax.ShapeDtypeStruct((t, n), F32),
        scratch_shapes=[pltpu.VMEM((n // tn, SUBLANES, tn), F32)] if carry else [],
        compiler_params=pltpu.CompilerParams(
            dimension_semantics=("arbitrary", "arbitrary"), vmem_limit_bytes=VMEM_LIMIT),
        name="in_proj_%d" % n,
    )(h, w, *col_params)


def _ssd_kernel(u_ref, z_ref, sm_ref, dtb_ref, alog_ref, dskip_ref, gain_ref, expand_ref,
                y_ref, st_ref, yd_ref):
    L = SSD_CHUNK

    @pl.when(pl.program_id(1) == 0)
    def _():
        st_ref[...] = jnp.zeros_like(st_ref)

    tri = _lower_tri(L, strict=False)
    tri_ones = _lower_tri_ones(L)
    head_a = lax.broadcasted_iota(jnp.int32, (L, LANES), 1) < SSM_HEAD_DIM
    for ci in range(SSD_CHUNKS_PER_STEP):
        rows = slice(ci * L, (ci + 1) * L)
        dt = _softplus(sm_ref[rows, LANES:2 * LANES] + dtb_ref[...])
        a = dt * (-jnp.exp(alog_ref[...]))
        acum = _dot_left_exact(tri_ones, a)
        acum_t = acum.T
        dt_t = dt.T
        a_last = acum[L - 1:L, :]
        dec_out_p = _two_terms_on_lanes(jnp.exp(acum))
        w_state_p = _two_terms_on_lanes(dt * jnp.exp(a_last - acum))

        for g in range(SSM_GROUPS):
            c0 = g * SSM_GROUP_WIDTH
            ex = expand_ref[:, c0:c0 + SSM_GROUP_WIDTH]
            xs = u_ref[rows, c0:c0 + SSM_GROUP_WIDTH]
            b0 = SSM_INNER + g * SSM_STATE
            bm = u_ref[rows, b0:b0 + SSM_STATE].astype(BF16)
            cm = u_ref[rows, b0 + SSM_GROUPS * SSM_STATE:b0 + (SSM_GROUPS + 1) * SSM_STATE].astype(BF16)
            dec_out_x = _dot(dec_out_p, ex)
            w_state_x = _dot(w_state_p, ex)
            xs_b = xs.astype(BF16)

            scores = _dot_nt(cm, bm)
            state = st_ref[g]
            y_off = _dot(cm, state.astype(BF16)) * dec_out_x
            st_ref[g] = state * dec_out_x[L - 1:L, :] + _dot_tn(bm, (xs * w_state_x).astype(BF16))

            def mix_of(h):
                seg = acum[:, h:h + 1] - acum_t[h:h + 1, :]
                return (scores * jnp.exp(jnp.where(tri, seg, -jnp.inf)) * dt_t[h:h + 1, :]).astype(BF16)

            for jp in range(SSM_GROUP_WIDTH // LANES):
                h = g * (SSM_GROUP_WIDTH // SSM_HEAD_DIM) + 2 * jp
                xp = xs_b[:, jp * LANES:(jp + 1) * LANES]
                x2 = jnp.concatenate([jnp.where(head_a, xp, 0.0), jnp.where(head_a, 0.0, xp)], axis=0)
                yd_ref[ci, :, jp * LANES:(jp + 1) * LANES] = _dot(
                    jnp.concatenate([mix_of(h), mix_of(h + 1)], axis=1), x2)

            y = yd_ref[ci] + y_off + dskip_ref[:, c0:c0 + SSM_GROUP_WIDTH] * xs
            z = z_ref[rows, c0:c0 + SSM_GROUP_WIDTH]
            y = y * _silu(z)
            ms = jnp.mean(y * y, axis=-1, keepdims=True)
            y_ref[rows, c0:c0 + SSM_GROUP_WIDTH] = (
                y * lax.rsqrt(ms + EPS) * gain_ref[:, c0:c0 + SSM_GROUP_WIDTH]).astype(BF16)


def _ssd_branch(u, zg, rw, batch, seq, dt_bias, a_log, dskip_x, norm_gain, expand):
    L = SSD_CHUNK * SSD_CHUNKS_PER_STEP
    nc = seq // L
    row = lambda b, c: b * nc + c
    const = lambda shape: pl.BlockSpec(shape, lambda b, c: (0, 0))
    return pl.pallas_call(
        _ssd_kernel,
        grid=(batch, nc),
        in_specs=[
            pl.BlockSpec((L, SSM_XBC), lambda b, c: (row(b, c), 0)),
            pl.BlockSpec((L, SSM_INNER), lambda b, c: (row(b, c), 0)),
            pl.BlockSpec((L, SMALL_WIDTH), lambda b, c: (row(b, c), 4 * RWKV_DIM // SMALL_WIDTH)),
            const((1, LANES)),
            const((1, LANES)),
            const((1, SSM_INNER)),
            const((1, SSM_INNER)),
            const((LANES, SSM_INNER)),
        ],
        out_specs=pl.BlockSpec((L, SSM_INNER), lambda b, c: (row(b, c), 0)),
        out_shape=jax.ShapeDtypeStruct((batch * seq, SSM_INNER), BF16),
        scratch_shapes=[
            pltpu.VMEM((SSM_GROUPS, SSM_STATE, SSM_GROUP_WIDTH), F32),
            pltpu.VMEM((SSD_CHUNKS_PER_STEP, SSD_CHUNK, SSM_GROUP_WIDTH), F32),
        ],
        compiler_params=pltpu.CompilerParams(
            dimension_semantics=("arbitrary", "arbitrary"), vmem_limit_bytes=VMEM_LIMIT),
        name="ssd_branch",
    )(u, zg, rw, dt_bias, a_log, dskip_x, norm_gain, expand)


def _rwkv_kernel(r_ref, k_ref, v_ref, g_ref, sm_ref, w0_ref, w2_ref, a0_ref, a2_ref, kks_ref, kas_ref,
                 rk_ref, gng_ref, gnb_ref, y_ref, s_ref):
    C = RWKV_CHUNK
    D = RWKV_HEAD_DIM
    NB = RWKV_BATCH_ROWS
    CP = RWKV_CHUNKS_PER_STEP
    R = NB * CP * C

    @pl.when(pl.program_id(1) == 0)
    def _():
        s_ref[...] = jnp.zeros_like(s_ref)

    rows = lambda x_ref: x_ref[...].reshape(R, x_ref.shape[-1])
    r, k, v, gate, sm = rows(r_ref), rows(k_ref), rows(v_ref), rows(g_ref), rows(sm_ref)
    wd = sm[:, 0:LORA]
    ad = sm[:, LORA:2 * LORA]

    lw = _dot(jnp.tanh(wd).astype(BF16), w2_ref[...])
    logw = (-0.5 * DECAY_SCALE) * jnp.tanh(0.5 * (w0_ref[...] + lw)) - 0.5 * DECAY_SCALE
    th_a = jnp.tanh(0.5 * (a0_ref[...] + _dot(ad.astype(BF16), a2_ref[...])))
    neg_a = -0.5 * th_a - 0.5
    kkp = k * kks_ref[...]
    half_ka = 0.5 * kas_ref[...]
    k2 = k * ((1.0 - half_ka) + half_ka * th_a)

    tri = _lower_tri_ones(C)
    cum = jnp.concatenate([_dot_left_exact(tri, logw[q * C:(q + 1) * C]) for q in range(NB * CP)], axis=0)
    e_pos = jnp.exp(cum)
    e_neg = 1.0 / e_pos
    kc_p = kkp * jnp.exp(cum - logw)
    rt = r * e_pos
    kt = k2 * e_neg
    nbt_p = kkp * neg_a * e_neg
    gamma_end = [jnp.exp(cum[(q + 1) * C - 1:(q + 1) * C, :]) for q in range(NB * CP)]
    rkk = r * k2 * rk_ref[...]
    gsilu = _silu(gate)

    P = 2 * D
    lane_a = lax.broadcasted_iota(jnp.int32, (C, P), 1) < D
    lane_a2 = lax.broadcasted_iota(jnp.int32, (2 * C, P), 1) < D
    row2 = lax.broadcasted_iota(jnp.int32, (2 * C, 2 * C), 0)
    col2 = lax.broadcasted_iota(jnp.int32, (2 * C, 2 * C), 1)
    colm = jnp.where(col2 >= C, col2 - C, col2)
    gmask = colm < jnp.where(row2 >= C, row2 - C + 1, row2)
    diag_blk = jnp.where(row2 >= C, 1, 0) == jnp.where(col2 >= C, 1, 0)
    anti_eye = jnp.where(col2 == jnp.where(row2 >= C, row2 - C, row2 + C), 1.0, 0.0)

    def seg_sum(x):
        sa = jnp.sum(jnp.where(lane_a, x, 0.0), axis=-1, keepdims=True)
        sb = jnp.sum(jnp.where(lane_a, 0.0, x), axis=-1, keepdims=True)
        return jnp.where(lane_a, sa, sb)

    def cross(lo, hi):
        return jnp.concatenate([lo, hi], axis=0).astype(BF16)

    units = [(b, j) for b in range(NB) for j in range(RWKV_HEADS // 2)]
    n_units = range(len(units))

    def chunk_chain(ci):
        idx = [(slice((b * CP + ci) * C, (b * CP + ci + 1) * C), slice(j * P, (j + 1) * P)) for b, j in units]
        xb, wa, g_top, g_bot, v_p = [], [], [], [], []
        for rs, sl in idx:
            kkp_p = kkp[rs, sl]
            rinv = jnp.minimum(lax.rsqrt(seg_sum(kkp_p * kkp_p)), 1e12)
            nbt = nbt_p[rs, sl] * rinv
            xp = jnp.concatenate([kc_p[rs, sl] * rinv, rt[rs, sl]], axis=0)
            w_a = jnp.concatenate([nbt, kt[rs, sl]], axis=0).astype(BF16)
            w_b = jnp.concatenate([kt[rs, sl], nbt], axis=0).astype(BF16)
            g_a = jnp.where(gmask, _dot_nt(jnp.where(lane_a2, xp, 0.0).astype(BF16), w_a), 0.0)
            g_b = jnp.where(gmask, _dot_nt(jnp.where(lane_a2, 0.0, xp).astype(BF16), w_b), 0.0)
            xb.append(xp.astype(BF16))
            wa.append(w_a)
            g_top.append(jnp.concatenate([g_a[:C], g_b[:C]], axis=0))
            g_bot.append(jnp.concatenate([g_a[C:], g_b[C:]], axis=0))
            v_p.append(v[rs, sl])

        z = [_dot_nt(xb[i], s_ref[units[i][0], units[i][1]].astype(BF16)) for i in n_units]
        w1 = []
        for i in n_units:
            rw = _dot(g_top[i].astype(BF16),
                      cross(jnp.where(lane_a, 0.0, v_p[i]), jnp.where(lane_a, v_p[i], 0.0)))
            w1.append(jnp.where(lane_a, rw[:C], rw[C:]))

        pt = [jnp.where(diag_blk, g_top[i], anti_eye) for i in n_units]
        for _ in range(6):
            pt = [_dot(jnp.where(diag_blk, pt[i], 0.0).astype(BF16), pt[i].astype(BF16))
                  + jnp.where(diag_blk, 0.0, pt[i]) for i in n_units]

        u = []
        for i in n_units:
            rhs = z[i][:C] + w1[i]
            ru = _dot(pt[i].astype(BF16), cross(jnp.where(lane_a, 0.0, rhs), jnp.where(lane_a, rhs, 0.0)))
            u.append(ru[:C] + ru[C:])
        for i in n_units:
            (b, j), (rs, sl) = units[i], idx[i]
            ry = _dot(g_bot[i].astype(BF16),
                      cross(jnp.where(lane_a, u[i], v_p[i]), jnp.where(lane_a, v_p[i], u[i])))
            y = z[i][C:] + jnp.where(lane_a, ry[:C], ry[C:])
            upd = _dot_tn(jnp.concatenate([u[i], v_p[i]], axis=0).astype(BF16), wa[i])
            s_ref[b, j] = (s_ref[b, j] + jnp.where(diag_blk, upd, 0.0)) * gamma_end[b * CP + ci][:, sl]

            yc = y - seg_sum(y) * (1.0 / D)
            var = seg_sum(yc * yc) * (1.0 / D)
            yn = yc * lax.rsqrt(var + GN_EPS) * gng_ref[:, sl] + gnb_ref[:, sl]
            bonus = seg_sum(rkk[rs, sl]) * v_p[i]
            y_ref[b, ci * C:(ci + 1) * C, sl] = ((yn + bonus) * gsilu[rs, sl]).astype(BF16)

    for ci in range(CP):
        chunk_chain(ci)


def _rwkv_branch(rw, batch, seq, w0, w2, a0, a2, k_k, k_a, r_k, gn_gain, gn_bias):
    C = RWKV_CHUNK
    NB = RWKV_BATCH_ROWS
    CP = RWKV_CHUNKS_PER_STEP
    rw3 = rw.reshape(batch, seq, RW_COLS)
    const = lambda shape: pl.BlockSpec(shape, lambda b, c: (0, 0))
    col_block = lambda j: pl.BlockSpec((NB, CP * C, RWKV_DIM), lambda b, c: (b, c, j))
    return pl.pallas_call(
        _rwkv_kernel,
        grid=(batch // NB, seq // (CP * C)),
        in_specs=[
            col_block(0), col_block(1), col_block(2), col_block(3),
            pl.BlockSpec((NB, CP * C, SMALL_WIDTH), lambda b, c: (b, c, 4 * RWKV_DIM // SMALL_WIDTH)),
            const((1, RWKV_DIM)), const((LORA, RWKV_DIM)), const((1, RWKV_DIM)), const((LORA, RWKV_DIM)),
            const((1, RWKV_DIM)), const((1, RWKV_DIM)), const((1, RWKV_DIM)),
            const((1, RWKV_DIM)), const((1, RWKV_DIM)),
        ],
        out_specs=pl.BlockSpec((NB, CP * C, RWKV_DIM), lambda b, c: (b, c, 0)),
        out_shape=jax.ShapeDtypeStruct((batch, seq, RWKV_DIM), BF16),
        scratch_shapes=[
            pltpu.VMEM((NB, RWKV_HEADS // 2, 2 * RWKV_HEAD_DIM, 2 * RWKV_HEAD_DIM), F32),
        ],
        compiler_params=pltpu.CompilerParams(
            dimension_semantics=("arbitrary", "arbitrary"), vmem_limit_bytes=VMEM_LIMIT),
        name="rwkv_branch",
    )(rw3, rw3, rw3, rw3, rw3, w0, w2, a0, a2, k_k, k_a, r_k, gn_gain, gn_bias
      ).reshape(batch * seq, RWKV_DIM)


def _merge_kernel(x_ref, ys_ref, yr_ref, g0_ref, g1_ref, bg_ref, ws_ref, wr_ref, wo_ref, gain_ref, o_ref):
    g_ssm = _sigmoid(g0_ref[...] + bg_ref[:, 0:D_MODEL])
    g_rwkv = _sigmoid(g1_ref[...] + bg_ref[:, D_MODEL:2 * D_MODEL])
    merged = g_ssm * _dot(ys_ref[...], ws_ref[...]) + g_rwkv * _dot(yr_ref[...], wr_ref[...])
    out = _dot(merged.astype(BF16), wo_ref[...])
    ms = jnp.mean(out * out, axis=-1, keepdims=True)
    o_ref[...] = x_ref[...] + out * lax.rsqrt(ms + EPS) * gain_ref[...]


def _merge(x2d, y_ssm, y_rwkv, zg, b_gate, w_ssm, w_rwkv, w_out, post_gain):
    t = x2d.shape[0]
    tm = MERGE_TM
    const = lambda shape: pl.BlockSpec(shape, lambda i: (0, 0), pipeline_mode=pl.Buffered(1))
    return pl.pallas_call(
        _merge_kernel,
        grid=(t // tm,),
        in_specs=[
            pl.BlockSpec((tm, D_MODEL), lambda i: (i, 0)),
            pl.BlockSpec((tm, SSM_INNER), lambda i: (i, 0)),
            pl.BlockSpec((tm, RWKV_DIM), lambda i: (i, 0)),
            pl.BlockSpec((tm, D_MODEL), lambda i: (i, SSM_INNER // D_MODEL)),
            pl.BlockSpec((tm, D_MODEL), lambda i: (i, SSM_INNER // D_MODEL + 1)),
            const((1, 2 * D_MODEL)),
            const((SSM_INNER, D_MODEL)), const((RWKV_DIM, D_MODEL)), const((D_MODEL, D_MODEL)),
            const((1, D_MODEL)),
        ],
        out_specs=pl.BlockSpec((tm, D_MODEL), lambda i: (i, 0)),
        out_shape=jax.ShapeDtypeStruct((t, D_MODEL), F32),
        compiler_params=pltpu.CompilerParams(
            dimension_semantics=("arbitrary",), vmem_limit_bytes=VMEM_LIMIT),
        name="gated_merge",
    )(x2d, y_ssm, y_rwkv, zg, zg, b_gate, w_ssm, w_rwkv, w_out, post_gain)


def _pad_lanes(v, width):
    return jnp.pad(v, ((0, 0), (0, width - v.shape[-1])))


def _layer(x, pre_gain, w_in, b_gate, conv_w, conv_b, dt_bias, a_log, d_skip, ssm_norm_gain,
           rwkv_mu, decay_w0, decay_w2, iclr_a0, iclr_a2, k_k, k_a, r_k, gn_gain, gn_bias,
           w_branch_ssm, w_branch_rwkv, w_out, post_gain):
    batch, seq, _ = x.shape
    x2d = x.reshape(batch * seq, D_MODEL)
    row = lambda v: v.reshape(1, -1)

    o_xbc, o_dt, o_rw = SSM_INNER, SSM_INNER + SSM_XBC, SSM_INNER + SSM_XBC + SSM_HEADS
    o_lora, o_gate = o_rw + 4 * RWKV_DIM, o_rw + 4 * RWKV_DIM + 2 * LORA
    pad = SMALL_WIDTH - 2 * LORA - SSM_HEADS
    w_xbc = w_in[:, o_xbc:o_dt].astype(BF16)
    w_rw = jnp.concatenate([w_in[:, o_rw:o_gate], w_in[:, o_dt:o_rw], jnp.zeros((D_MODEL, pad + RW_PAD), w_in.dtype)],
                           axis=1).astype(BF16)
    w_zg = jnp.concatenate([w_in[:, :o_xbc], w_in[:, o_gate:]], axis=1).astype(BF16)
    mu = _pad_lanes(row(rwkv_mu), RW_COLS)

    gain = row(pre_gain)
    u, h = _project_normed(_proj_conv_kernel, seq, x2d, gain, w_xbc, [conv_w, row(conv_b)])
    rw = _project(_proj_shift_kernel, seq, h, w_rw, [mu], True)
    zg = _project(_proj_plain_kernel, seq, h, w_zg, [], False)

    head_of_col = jnp.arange(SSM_INNER) // SSM_HEAD_DIM
    expand = ((jnp.arange(LANES)[:, None] % SSM_HEADS == head_of_col[None, :])
              & (jnp.arange(LANES)[:, None] < 2 * SSM_HEADS)).astype(BF16)
    y_ssm = _ssd_branch(
        u, zg, rw, batch, seq, _pad_lanes(row(dt_bias), LANES), _pad_lanes(row(a_log), LANES),
        row(jnp.repeat(d_skip, SSM_HEAD_DIM)), row(ssm_norm_gain), expand)

    y_rwkv = _rwkv_branch(
        rw, batch, seq, row(decay_w0), decay_w2.astype(BF16), row(iclr_a0), iclr_a2.astype(BF16),
        row(k_k), row(k_a), row(r_k), row(gn_gain), row(gn_bias))

    out = _merge(x2d, y_ssm, y_rwkv, zg, row(b_gate), w_branch_ssm.astype(BF16),
                 w_branch_rwkv.astype(BF16), w_out.astype(BF16), row(post_gain))
    return out.reshape(batch, seq, D_MODEL)


def kernel(x, pre_gain, w_in, b_gate, conv_w, conv_b, dt_bias, a_log, d_skip, ssm_norm_gain, rwkv_mu,
           decay_w0, decay_w2, iclr_a0, iclr_a2, k_k, k_a, r_k, gn_gain, gn_bias, w_branch_ssm,
           w_branch_rwkv, w_out, post_gain):
    for layer in range(pre_gain.shape[0]):
        x = _layer(
            x, pre_gain[layer], w_in[layer], b_gate[layer], conv_w[layer], conv_b[layer], dt_bias[layer],
            a_log[layer], d_skip[layer], ssm_norm_gain[layer], rwkv_mu[layer], decay_w0[layer],
            decay_w2[layer], iclr_a0[layer], iclr_a2[layer], k_k[layer], k_a[layer], r_k[layer],
            gn_gain[layer], gn_bias[layer], w_branch_ssm[layer], w_branch_rwkv[layer], w_out[layer],
            post_gain[layer])
    return x
```

```python
import functools

import jax
import jax.numpy as jnp
from jax import lax
from jax.experimental import pallas as pl
from jax.experimental.pallas import tpu as pltpu

D_MODEL = 1024
EPS = 1e-6

SSM_INNER = 2048
SSM_HEAD_DIM = 64
SSM_HEADS = 32
SSM_STATE = 128
SSM_GROUPS = 4
SSM_GROUP_WIDTH = SSM_INNER // SSM_GROUPS
CONV_WIDTH = 4
SSM_XBC = 3072

RWKV_DIM = 1024
RWKV_HEAD_DIM = 64
RWKV_HEADS = 16
LORA = 64
GN_EPS = RWKV_HEAD_DIM * 1e-5
DECAY_SCALE = 0.6065306597126334

LANES = 128
SUBLANES = 8

SMALL_WIDTH = 256
RW_PAD = 256
RW_COLS = 4 * RWKV_DIM + SMALL_WIDTH + RW_PAD
ZG_COLS = SSM_INNER + 2 * D_MODEL

PROJ_TM = {SSM_XBC: 1024, RW_COLS: 2048, ZG_COLS: 2048}
PROJ_SUBTILE = 256
PROJ_TN = {SSM_XBC: 1024, RW_COLS: RW_COLS // 6, ZG_COLS: 1024}
SSD_CHUNK = 128
SSD_CHUNKS_PER_STEP = 4
RWKV_CHUNK = 64
RWKV_BATCH_ROWS = 2
RWKV_CHUNKS_PER_STEP = 2
MERGE_TM = 512
VMEM_LIMIT = 48 * 1024 * 1024

BF16 = jnp.bfloat16
F32 = jnp.float32


def _dot(a, b):
    return jnp.dot(a, b, preferred_element_type=F32)


def _dot_nt(a, b):
    return lax.dot_general(a, b, (((1,), (1,)), ((), ())), preferred_element_type=F32)


def _dot_tn(a, b):
    return lax.dot_general(a, b, (((0,), (0,)), ((), ())), preferred_element_type=F32)


def _split_terms(x, n):
    terms = []
    rem = x
    for _ in range(n):
        t = rem.astype(BF16)
        terms.append(t)
        rem = rem - t.astype(F32)
    return terms


def _dot_left_exact(m_bf16, x, n=3):
    acc = None
    for t in _split_terms(x, n):
        p = _dot(m_bf16, t)
        acc = p if acc is None else acc + p
    return acc


def _two_terms_on_lanes(x):
    lane = lax.broadcasted_iota(jnp.int32, x.shape, 1)
    xm = jnp.where(lane < SSM_HEADS, x, 0.0)
    hi = xm.astype(BF16).astype(F32)
    return (hi + pltpu.roll(xm - hi, SSM_HEADS, 1)).astype(BF16)


def _softplus(x):
    return jnp.maximum(x, 0.0) + jnp.log(1.0 + jnp.exp(-jnp.abs(x)))


def _sigmoid(x):
    return 0.5 * jnp.tanh(0.5 * x) + 0.5


def _silu(x):
    h = 0.5 * x
    return h + h * jnp.tanh(h)


def _lower_tri(n, strict):
    row = lax.broadcasted_iota(jnp.int32, (n, n), 0)
    col = lax.broadcasted_iota(jnp.int32, (n, n), 1)
    return (col < row) if strict else (col <= row)


def _lower_tri_ones(n):
    row = lax.broadcasted_iota(jnp.int32, (n, n), 0)
    col = lax.broadcasted_iota(jnp.int32, (n, n), 1)
    return jnp.clip(row - col + 1, 0, 1).astype(F32).astype(BF16)


def _normed_input(x_ref, gain_ref, h_ref):
    @pl.when(pl.program_id(1) == 0)
    def _():
        x = x_ref[...]
        ms = jnp.mean(x * x, axis=-1, keepdims=True)
        h_ref[...] = (x * lax.rsqrt(ms + EPS) * gain_ref[...]).astype(BF16)


def _delayed(o, halo, k):
    head = jnp.concatenate([halo, o[:SUBLANES]], axis=0)[SUBLANES - k:2 * SUBLANES - k]
    if o.shape[0] == SUBLANES:
        return head
    return jnp.concatenate([head, pltpu.roll(o, k, 0)[SUBLANES:]], axis=0)


def _zero_carry_at_sequence_start(carry_ref, row_blocks_per_seq):
    i, j = pl.program_id(0), pl.program_id(1)

    @pl.when(i % row_blocks_per_seq == 0)
    def _():
        carry_ref[j] = jnp.zeros(carry_ref.shape[1:], F32)


def _proj_plain_kernel(h_ref, w_ref, o_ref):
    o_ref[...] = _dot(h_ref[...], w_ref[...])


def _proj_conv_kernel(row_blocks_per_seq, x_ref, gain_ref, w_ref, cw_ref, cb_ref, o_ref, h_ref, carry_ref):
    _normed_input(x_ref, gain_ref, h_ref)
    _zero_carry_at_sequence_start(carry_ref, row_blocks_per_seq)
    j = pl.program_id(1)
    for c0 in range(0, o_ref.shape[1], PROJ_SUBTILE):
        cs = slice(c0, c0 + PROJ_SUBTILE)
        o = _dot(h_ref[...], w_ref[:, cs])
        halo = carry_ref[j, :, cs]
        carry_ref[j, :, cs] = o[o.shape[0] - SUBLANES:, :]
        w0, w1, w2, w3 = (0.5 * cw_ref[t:t + 1, cs] for t in range(CONV_WIDTH))
        e = _delayed(o, halo, 1)
        e_halo = _delayed(halo, jnp.zeros_like(halo), 1)
        inner = w1 * o + w0 * e
        inner_halo = w1 * halo + w0 * e_halo
        h = 0.5 * cb_ref[:, cs] + w3 * o + w2 * e + _delayed(inner, inner_halo, 2)
        o_ref[:, cs] = h + h * jnp.tanh(h)


def _proj_shift_kernel(row_blocks_per_seq, h_ref, w_ref, mu_ref, o_ref, carry_ref):
    _zero_carry_at_sequence_start(carry_ref, row_blocks_per_seq)
    j = pl.program_id(1)
    o = _dot(h_ref[...], w_ref[...])
    halo = carry_ref[j]
    carry_ref[j] = o[o.shape[0] - SUBLANES:, :]
    o_ref[...] = o + mu_ref[...] * (_delayed(o, halo, 1) - o)


def _project_normed(kernel_fn, seq, x2d, pre_gain, w, col_params):
    t, n = x2d.shape[0], w.shape[1]
    tm, tn = PROJ_TM[n], PROJ_TN[n]
    return pl.pallas_call(
        functools.partial(kernel_fn, seq // tm),
        grid=(t // tm, n // tn),
        in_specs=[
            pl.BlockSpec((tm, D_MODEL), lambda i, j: (i, 0)),
            pl.BlockSpec((1, D_MODEL), lambda i, j: (0, 0)),
            pl.BlockSpec((D_MODEL, tn), lambda i, j: (0, j)),
        ] + [pl.BlockSpec((c.shape[0], tn), lambda i, j: (0, j)) for c in col_params],
        out_specs=[pl.BlockSpec((tm, tn), lambda i, j: (i, j)),
                   pl.BlockSpec((tm, D_MODEL), lambda i, j: (i, 0))],
        out_shape=[jax.ShapeDtypeStruct((t, n), F32), jax.ShapeDtypeStruct((t, D_MODEL), BF16)],
        scratch_shapes=[pltpu.VMEM((n // tn, SUBLANES, tn), F32)],
        compiler_params=pltpu.CompilerParams(
            dimension_semantics=("arbitrary", "arbitrary"), vmem_limit_bytes=VMEM_LIMIT),
        name="in_proj_%d" % n,
    )(x2d, pre_gain, w, *col_params)


def _project(kernel_fn, seq, h, w, col_params, carry):
    t, n = h.shape[0], w.shape[1]
    tm, tn = PROJ_TM[n], PROJ_TN[n]
    return pl.pallas_call(
        functools.partial(kernel_fn, seq // tm) if carry else kernel_fn,
        grid=(t // tm, n // tn),
        in_specs=[
            pl.BlockSpec((tm, D_MODEL), lambda i, j: (i, 0)),
            pl.BlockSpec((D_MODEL, tn), lambda i, j: (0, j)),
        ] + [pl.BlockSpec((c.shape[0], tn), lambda i, j: (0, j)) for c in col_params],
        out_specs=pl.BlockSpec((tm, tn), lambda i, j: (i, j)),
        out_shape=jax.ShapeDtypeStruct((t, n), F32),
        scratch_shapes=[pltpu.VMEM((n // tn, SUBLANES, tn), F32)] if carry else [],
        compiler_params=pltpu.CompilerParams(
            dimension_semantics=("arbitrary", "arbitrary"), vmem_limit_bytes=VMEM_LIMIT),
        name="in_proj_%d" % n,
    )(h, w, *col_params)


def _ssd_kernel(u_ref, z_ref, sm_ref, dtb_ref, alog_ref, dskip_ref, gain_ref, expand_ref,
                y_ref, st_ref, yd_ref):
    L = SSD_CHUNK

    @pl.when(pl.program_id(1) == 0)
    def _():
        st_ref[...] = jnp.zeros_like(st_ref)

    tri = _lower_tri(L, strict=False)
    tri_ones = _lower_tri_ones(L)
    head_a = lax.broadcasted_iota(jnp.int32, (L, LANES), 1) < SSM_HEAD_DIM
    for ci in range(SSD_CHUNKS_PER_STEP):
        rows = slice(ci * L, (ci + 1) * L)
        dt = _softplus(sm_ref[rows, LANES:2 * LANES] + dtb_ref[...])
        a = dt * (-jnp.exp(alog_ref[...]))
        acum = _dot_left_exact(tri_ones, a)
        acum_t = acum.T
        dt_t = dt.T
        a_last = acum[L - 1:L, :]
        dec_out_p = _two_terms_on_lanes(jnp.exp(acum))
        w_state_p = _two_terms_on_lanes(dt * jnp.exp(a_last - acum))

        for g in range(SSM_GROUPS):
            c0 = g * SSM_GROUP_WIDTH
            ex = expand_ref[:, c0:c0 + SSM_GROUP_WIDTH]
            xs = u_ref[rows, c0:c0 + SSM_GROUP_WIDTH]
            b0 = SSM_INNER + g * SSM_STATE
            bm = u_ref[rows, b0:b0 + SSM_STATE].astype(BF16)
            cm = u_ref[rows, b0 + SSM_GROUPS * SSM_STATE:b0 + (SSM_GROUPS + 1) * SSM_STATE].astype(BF16)
            dec_out_x = _dot(dec_out_p, ex)
            w_state_x = _dot(w_state_p, ex)
            xs_b = xs.astype(BF16)

            scores = _dot_nt(cm, bm)
            state = st_ref[g]
            y_off = _dot(cm, state.astype(BF16)) * dec_out_x
            st_ref[g] = state * dec_out_x[L - 1:L, :] + _dot_tn(bm, (xs * w_state_x).astype(BF16))

            def mix_of(h):
                seg = acum[:, h:h + 1] - acum_t[h:h + 1, :]
                return (scores * jnp.exp(jnp.where(tri, seg, -jnp.inf)) * dt_t[h:h + 1, :]).astype(BF16)

            for jp in range(SSM_GROUP_WIDTH // LANES):
                h = g * (SSM_GROUP_WIDTH // SSM_HEAD_DIM) + 2 * jp
                xp = xs_b[:, jp * LANES:(jp + 1) * LANES]
                x2 = jnp.concatenate([jnp.where(head_a, xp, 0.0), jnp.where(head_a, 0.0, xp)], axis=0)
                yd_ref[ci, :, jp * LANES:(jp + 1) * LANES] = _dot(
                    jnp.concatenate([mix_of(h), mix_of(h + 1)], axis=1), x2)

            y = yd_ref[ci] + y_off + dskip_ref[:, c0:c0 + SSM_GROUP_WIDTH] * xs
            z = z_ref[rows, c0:c0 + SSM_GROUP_WIDTH]
            y = y * _silu(z)
            ms = jnp.mean(y * y, axis=-1, keepdims=True)
            y_ref[rows, c0:c0 + SSM_GROUP_WIDTH] = (
                y * lax.rsqrt(ms + EPS) * gain_ref[:, c0:c0 + SSM_GROUP_WIDTH]).astype(BF16)


def _ssd_branch(u, zg, rw, batch, seq, dt_bias, a_log, dskip_x, norm_gain, expand):
    L = SSD_CHUNK * SSD_CHUNKS_PER_STEP
    nc = seq // L
    row = lambda b, c: b * nc + c
    const = lambda shape: pl.BlockSpec(shape, lambda b, c: (0, 0))
    return pl.pallas_call(
        _ssd_kernel,
        grid=(batch, nc),
        in_specs=[
            pl.BlockSpec((L, SSM_XBC), lambda b, c: (row(b, c), 0)),
            pl.BlockSpec((L, SSM_INNER), lambda b, c: (row(b, c), 0)),
            pl.BlockSpec((L, SMALL_WIDTH), lambda b, c: (row(b, c), 4 * RWKV_DIM // SMALL_WIDTH)),
            const((1, LANES)),
            const((1, LANES)),
            const((1, SSM_INNER)),
            const((1, SSM_INNER)),
            const((LANES, SSM_INNER)),
        ],
        out_specs=pl.BlockSpec((L, SSM_INNER), lambda b, c: (row(b, c), 0)),
        out_shape=jax.ShapeDtypeStruct((batch * seq, SSM_INNER), BF16),
        scratch_shapes=[
            pltpu.VMEM((SSM_GROUPS, SSM_STATE, SSM_GROUP_WIDTH), F32),
            pltpu.VMEM((SSD_CHUNKS_PER_STEP, SSD_CHUNK, SSM_GROUP_WIDTH), F32),
        ],
        compiler_params=pltpu.CompilerParams(
            dimension_semantics=("arbitrary", "arbitrary"), vmem_limit_bytes=VMEM_LIMIT),
        name="ssd_branch",
    )(u, zg, rw, dt_bias, a_log, dskip_x, norm_gain, expand)


def _rwkv_kernel(r_ref, k_ref, v_ref, g_ref, sm_ref, w0_ref, w2_ref, a0_ref, a2_ref, kks_ref, kas_ref,
                 rk_ref, gng_ref, gnb_ref, y_ref, s_ref):
    C = RWKV_CHUNK
    D = RWKV_HEAD_DIM
    NB = RWKV_BATCH_ROWS
    CP = RWKV_CHUNKS_PER_STEP
    R = NB * CP * C

    @pl.when(pl.program_id(1) == 0)
    def _():
        s_ref[...] = jnp.zeros_like(s_ref)

    rows = lambda x_ref: x_ref[...].reshape(R, x_ref.shape[-1])
    r, k, v, gate, sm = rows(r_ref), rows(k_ref), rows(v_ref), rows(g_ref), rows(sm_ref)
    wd = sm[:, 0:LORA]
    ad = sm[:, LORA:2 * LORA]

    lw = _dot(jnp.tanh(wd).astype(BF16), w2_ref[...])
    logw = (-0.5 * DECAY_SCALE) * jnp.tanh(0.5 * (w0_ref[...] + lw)) - 0.5 * DECAY_SCALE
    th_a = jnp.tanh(0.5 * (a0_ref[...] + _dot(ad.astype(BF16), a2_ref[...])))
    neg_a = -0.5 * th_a - 0.5
    kkp = k * kks_ref[...]
    half_ka = 0.5 * kas_ref[...]
    k2 = k * ((1.0 - half_ka) + half_ka * th_a)

    tri = _lower_tri_ones(C)
    cum = jnp.concatenate([_dot_left_exact(tri, logw[q * C:(q + 1) * C]) for q in range(NB * CP)], axis=0)
    e_pos = jnp.exp(cum)
    e_neg = 1.0 / e_pos
    kc_p = kkp * jnp.exp(cum - logw)
    rt = r * e_pos
    kt = k2 * e_neg
    nbt_p = kkp * neg_a * e_neg
    gamma_end = [jnp.exp(cum[(q + 1) * C - 1:(q + 1) * C, :]) for q in range(NB * CP)]
    rkk = r * k2 * rk_ref[...]
    gsilu = _silu(gate)

    P = 2 * D
    lane_a = lax.broadcasted_iota(jnp.int32, (C, P), 1) < D
    lane_a2 = lax.broadcasted_iota(jnp.int32, (2 * C, P), 1) < D
    row2 = lax.broadcasted_iota(jnp.int32, (2 * C, 2 * C), 0)
    col2 = lax.broadcasted_iota(jnp.int32, (2 * C, 2 * C), 1)
    colm = jnp.where(col2 >= C, col2 - C, col2)
    gmask = colm < jnp.where(row2 >= C, row2 - C + 1, row2)
    diag_blk = jnp.where(row2 >= C, 1, 0) == jnp.where(col2 >= C, 1, 0)
    anti_eye = jnp.where(col2 == jnp.where(row2 >= C, row2 - C, row2 + C), 1.0, 0.0)

    def seg_sum(x):
        sa = jnp.sum(jnp.where(lane_a, x, 0.0), axis=-1, keepdims=True)
        sb = jnp.sum(jnp.where(lane_a, 0.0, x), axis=-1, keepdims=True)
        return jnp.where(lane_a, sa, sb)

    def cross(lo, hi):
        return jnp.concatenate([lo, hi], axis=0).astype(BF16)

    units = [(b, j) for b in range(NB) for j in range(RWKV_HEADS // 2)]
    n_units = range(len(units))

    def chunk_chain(ci):
        idx = [(slice((b * CP + ci) * C, (b * CP + ci + 1) * C), slice(j * P, (j + 1) * P)) for b, j in units]
        xb, wa, g_top, g_bot, v_p = [], [], [], [], []
        for rs, sl in idx:
            kkp_p = kkp[rs, sl]
            rinv = jnp.minimum(lax.rsqrt(seg_sum(kkp_p * kkp_p)), 1e12)
            nbt = nbt_p[rs, sl] * rinv
            xp = jnp.concatenate([kc_p[rs, sl] * rinv, rt[rs, sl]], axis=0)
            w_a = jnp.concatenate([nbt, kt[rs, sl]], axis=0).astype(BF16)
            w_b = jnp.concatenate([kt[rs, sl], nbt], axis=0).astype(BF16)
            g_a = jnp.where(gmask, _dot_nt(jnp.where(lane_a2, xp, 0.0).astype(BF16), w_a), 0.0)
            g_b = jnp.where(gmask, _dot_nt(jnp.where(lane_a2, 0.0, xp).astype(BF16), w_b), 0.0)
            xb.append(xp.astype(BF16))
            wa.append(w_a)
            g_top.append(jnp.concatenate([g_a[:C], g_b[:C]], axis=0))
            g_bot.append(jnp.concatenate([g_a[C:], g_b[C:]], axis=0))
            v_p.append(v[rs, sl])

        z = [_dot_nt(xb[i], s_ref[units[i][0], units[i][1]].astype(BF16)) for i in n_units]
        w1 = []
        for i in n_units:
            rw = _dot(g_top[i].astype(BF16),
                      cross(jnp.where(lane_a, 0.0, v_p[i]), jnp.where(lane_a, v_p[i], 0.0)))
            w1.append(jnp.where(lane_a, rw[:C], rw[C:]))

        pt = [jnp.where(diag_blk, g_top[i], anti_eye) for i in n_units]
        for _ in range(6):
            pt = [_dot(jnp.where(diag_blk, pt[i], 0.0).astype(BF16), pt[i].astype(BF16))
                  + jnp.where(diag_blk, 0.0, pt[i]) for i in n_units]

        u = []
        for i in n_units:
            rhs = z[i][:C] + w1[i]
            ru = _dot(pt[i].astype(BF16), cross(jnp.where(lane_a, 0.0, rhs), jnp.where(lane_a, rhs, 0.0)))
            u.append(ru[:C] + ru[C:])
        for i in n_units:
            (b, j), (rs, sl) = units[i], idx[i]
            ry = _dot(g_bot[i].astype(BF16),
                      cross(jnp.where(lane_a, u[i], v_p[i]), jnp.where(lane_a, v_p[i], u[i])))
            y = z[i][C:] + jnp.where(lane_a, ry[:C], ry[C:])
            upd = _dot_tn(jnp.concatenate([u[i], v_p[i]], axis=0).astype(BF16), wa[i])
            s_ref[b, j] = (s_ref[b, j] + jnp.where(diag_blk, upd, 0.0)) * gamma_end[b * CP + ci][:, sl]

            yc = y - seg_sum(y) * (1.0 / D)
            var = seg_sum(yc * yc) * (1.0 / D)
            yn = yc * lax.rsqrt(var + GN_EPS) * gng_ref[:, sl] + gnb_ref[:, sl]
            bonus = seg_sum(rkk[rs, sl]) * v_p[i]
            y_ref[b, ci * C:(ci + 1) * C, sl] = ((yn + bonus) * gsilu[rs, sl]).astype(BF16)

    for ci in range(CP):
        chunk_chain(ci)


def _rwkv_branch(rw, batch, seq, w0, w2, a0, a2, k_k, k_a, r_k, gn_gain, gn_bias):
    C = RWKV_CHUNK
    NB = RWKV_BATCH_ROWS
    CP = RWKV_CHUNKS_PER_STEP
    rw3 = rw.reshape(batch, seq, RW_COLS)
    const = lambda shape: pl.BlockSpec(shape, lambda b, c: (0, 0))
    col_block = lambda j: pl.BlockSpec((NB, CP * C, RWKV_DIM), lambda b, c: (b, c, j))
    return pl.pallas_call(
        _rwkv_kernel,
        grid=(batch // NB, seq // (CP * C)),
        in_specs=[
            col_block(0), col_block(1), col_block(2), col_block(3),
            pl.BlockSpec((NB, CP * C, SMALL_WIDTH), lambda b, c: (b, c, 4 * RWKV_DIM // SMALL_WIDTH)),
            const((1, RWKV_DIM)), const((LORA, RWKV_DIM)), const((1, RWKV_DIM)), const((LORA, RWKV_DIM)),
            const((1, RWKV_DIM)), const((1, RWKV_DIM)), const((1, RWKV_DIM)),
            const((1, RWKV_DIM)), const((1, RWKV_DIM)),
        ],
        out_specs=pl.BlockSpec((NB, CP * C, RWKV_DIM), lambda b, c: (b, c, 0)),
        out_shape=jax.ShapeDtypeStruct((batch, seq, RWKV_DIM), BF16),
        scratch_shapes=[
            pltpu.VMEM((NB, RWKV_HEADS // 2, 2 * RWKV_HEAD_DIM, 2 * RWKV_HEAD_DIM), F32),
        ],
        compiler_params=pltpu.CompilerParams(
            dimension_semantics=("arbitrary", "arbitrary"), vmem_limit_bytes=VMEM_LIMIT),
        name="rwkv_branch",
    )(rw3, rw3, rw3, rw3, rw3, w0, w2, a0, a2, k_k, k_a, r_k, gn_gain, gn_bias
      ).reshape(batch * seq, RWKV_DIM)


def _merge_kernel(x_ref, ys_ref, yr_ref, g0_ref, g1_ref, bg_ref, ws_ref, wr_ref, wo_ref, gain_ref, o_ref):
    g_ssm = _sigmoid(g0_ref[...] + bg_ref[:, 0:D_MODEL])
    g_rwkv = _sigmoid(g1_ref[...] + bg_ref[:, D_MODEL:2 * D_MODEL])
    merged = g_ssm * _dot(ys_ref[...], ws_ref[...]) + g_rwkv * _dot(yr_ref[...], wr_ref[...])
    out = _dot(merged.astype(BF16), wo_ref[...])
    ms = jnp.mean(out * out, axis=-1, keepdims=True)
    o_ref[...] = x_ref[...] + out * lax.rsqrt(ms + EPS) * gain_ref[...]


def _merge(x2d, y_ssm, y_rwkv, zg, b_gate, w_ssm, w_rwkv, w_out, post_gain):
    t = x2d.shape[0]
    tm = MERGE_TM
    const = lambda shape: pl.BlockSpec(shape, lambda i: (0, 0), pipeline_mode=pl.Buffered(1))
    return pl.pallas_call(
        _merge_kernel,
        grid=(t // tm,),
        in_specs=[
            pl.BlockSpec((tm, D_MODEL), lambda i: (i, 0)),
            pl.BlockSpec((tm, SSM_INNER), lambda i: (i, 0)),
            pl.BlockSpec((tm, RWKV_DIM), lambda i: (i, 0)),
            pl.BlockSpec((tm, D_MODEL), lambda i: (i, SSM_INNER // D_MODEL)),
            pl.BlockSpec((tm, D_MODEL), lambda i: (i, SSM_INNER // D_MODEL + 1)),
            const((1, 2 * D_MODEL)),
            const((SSM_INNER, D_MODEL)), const((RWKV_DIM, D_MODEL)), const((D_MODEL, D_MODEL)),
            const((1, D_MODEL)),
        ],
        out_specs=pl.BlockSpec((tm, D_MODEL), lambda i: (i, 0)),
        out_shape=jax.ShapeDtypeStruct((t, D_MODEL), F32),
        compiler_params=pltpu.CompilerParams(
            dimension_semantics=("arbitrary",), vmem_limit_bytes=VMEM_LIMIT),
        name="gated_merge",
    )(x2d, y_ssm, y_rwkv, zg, zg, b_gate, w_ssm, w_rwkv, w_out, post_gain)


def _pad_lanes(v, width):
    return jnp.pad(v, ((0, 0), (0, width - v.shape[-1])))


W_XBC = SSM_INNER
W_DT = W_XBC + SSM_XBC
W_RW = W_DT + SSM_HEADS
W_GATE = W_RW + 4 * RWKV_DIM + 2 * LORA
W_COLS = W_GATE + 2 * D_MODEL
WEIGHT_ROWS = 128


def _lane_aligned(col):
    return col // LANES * LANES


def _weights_kernel(w_ref, xbc_ref, rw_ref, zg_ref):
    xbc_ref[...] = w_ref[:, W_XBC:W_DT].astype(BF16)
    zg_ref[:, 0:SSM_INNER] = w_ref[:, 0:W_XBC].astype(BF16)
    g0 = _lane_aligned(W_GATE)
    zg_ref[:, SSM_INNER:ZG_COLS] = w_ref[:, g0:W_COLS][:, W_GATE - g0:W_COLS - g0].astype(BF16)
    n_rw = W_GATE - W_RW
    r0 = _lane_aligned(W_RW)
    r1 = _lane_aligned(W_GATE + LANES - 1)
    rw_ref[:, 0:n_rw] = w_ref[:, r0:r1][:, W_RW - r0:W_GATE - r0].astype(BF16)
    dt_tile = w_ref[:, W_DT:W_DT + LANES]
    lane = lax.broadcasted_iota(jnp.int32, dt_tile.shape, 1)
    rw_ref[:, n_rw:n_rw + LANES] = jnp.where(lane < SSM_HEADS, dt_tile, 0.0).astype(BF16)
    rw_ref[:, n_rw + LANES:RW_COLS] = jnp.zeros((w_ref.shape[0], RW_COLS - n_rw - LANES), BF16)


def _split_projection_weights(w_in):
    rows = lambda n: pl.BlockSpec((WEIGHT_ROWS, n), lambda i: (i, 0))
    return pl.pallas_call(
        _weights_kernel,
        grid=(D_MODEL // WEIGHT_ROWS,),
        in_specs=[rows(W_COLS)],
        out_specs=[rows(SSM_XBC), rows(RW_COLS), rows(ZG_COLS)],
        out_shape=[jax.ShapeDtypeStruct((D_MODEL, n), BF16) for n in (SSM_XBC, RW_COLS, ZG_COLS)],
        compiler_params=pltpu.CompilerParams(dimension_semantics=("arbitrary",), vmem_limit_bytes=VMEM_LIMIT),
        name="weight_split",
    )(w_in)


def _layer(x, pre_gain, w_in, b_gate, conv_w, conv_b, dt_bias, a_log, d_skip, ssm_norm_gain,
           rwkv_mu, decay_w0, decay_w2, iclr_a0, iclr_a2, k_k, k_a, r_k, gn_gain, gn_bias,
           w_branch_ssm, w_branch_rwkv, w_out, post_gain):
    batch, seq, _ = x.shape
    x2d = x.reshape(batch * seq, D_MODEL)
    row = lambda v: v.reshape(1, -1)

    w_xbc, w_rw, w_zg = _split_projection_weights(w_in)
    mu = _pad_lanes(row(rwkv_mu), RW_COLS)

    gain = row(pre_gain)
    u, h = _project_normed(_proj_conv_kernel, seq, x2d, gain, w_xbc, [conv_w, row(conv_b)])
    rw = _project(_proj_shift_kernel, seq, h, w_rw, [mu], True)
    zg = _project(_proj_plain_kernel, seq, h, w_zg, [], False)

    head_of_col = jnp.arange(SSM_INNER) // SSM_HEAD_DIM
    expand = ((jnp.arange(LANES)[:, None] % SSM_HEADS == head_of_col[None, :])
              & (jnp.arange(LANES)[:, None] < 2 * SSM_HEADS)).astype(BF16)
    y_ssm = _ssd_branch(
        u, zg, rw, batch, seq, _pad_lanes(row(dt_bias), LANES), _pad_lanes(row(a_log), LANES),
        row(jnp.repeat(d_skip, SSM_HEAD_DIM)), row(ssm_norm_gain), expand)

    y_rwkv = _rwkv_branch(
        rw, batch, seq, row(decay_w0), decay_w2.astype(BF16), row(iclr_a0), iclr_a2.astype(BF16),
        row(k_k), row(k_a), row(r_k), row(gn_gain), row(gn_bias))

    out = _merge(x2d, y_ssm, y_rwkv, zg, row(b_gate), w_branch_ssm.astype(BF16),
                 w_branch_rwkv.astype(BF16), w_out.astype(BF16), row(post_gain))
    return out.reshape(batch, seq, D_MODEL)


def kernel(x, pre_gain, w_in, b_gate, conv_w, conv_b, dt_bias, a_log, d_skip, ssm_norm_gain, rwkv_mu,
           decay_w0, decay_w2, iclr_a0, iclr_a2, k_k, k_a, r_k, gn_gain, gn_bias, w_branch_ssm,
           w_branch_rwkv, w_out, post_gain):
    for layer in range(pre_gain.shape[0]):
        x = _layer(
            x, pre_gain[layer], w_in[layer], b_gate[layer], conv_w[layer], conv_b[layer], dt_bias[layer],
            a_log[layer], d_skip[layer], ssm_norm_gain[layer], rwkv_mu[layer], decay_w0[layer],
            decay_w2[layer], iclr_a0[layer], iclr_a2[layer], k_k[layer], k_a[layer], r_k[layer],
            gn_gain[layer], gn_bias[layer], w_branch_ssm[layer], w_branch_rwkv[layer], w_out[layer],
            post_gain[layer])
    return x
```

```python
import functools

import jax
import jax.numpy as jnp
from jax import lax
from jax.experimental import pallas as pl
from jax.experimental.pallas import tpu as pltpu

D_MODEL = 1024
EPS = 1e-6

SSM_INNER = 2048
SSM_HEAD_DIM = 64
SSM_HEADS = 32
SSM_STATE = 128
SSM_GROUPS = 4
SSM_GROUP_WIDTH = SSM_INNER // SSM_GROUPS
CONV_WIDTH = 4
SSM_XBC = 3072

RWKV_DIM = 1024
RWKV_HEAD_DIM = 64
RWKV_HEADS = 16
LORA = 64
GN_EPS = RWKV_HEAD_DIM * 1e-5
DECAY_SCALE = 0.6065306597126334

LANES = 128
SUBLANES = 8

SMALL_WIDTH = 256
RW_PAD = 256
RW_COLS = 4 * RWKV_DIM + SMALL_WIDTH + RW_PAD
ZG_COLS = SSM_INNER + 2 * D_MODEL

PROJ_TM = {SSM_XBC: 1024, RW_COLS: 2048, ZG_COLS: 2048}
PROJ_SUBTILE = 256
PROJ_TN = {SSM_XBC: 1024, RW_COLS: RW_COLS // 6, ZG_COLS: 1024}
SSD_CHUNK = 128
SSD_CHUNKS_PER_STEP = 4
RWKV_CHUNK = 64
RWKV_BATCH_ROWS = 2
RWKV_CHUNKS_PER_STEP = 2
MERGE_TM = 512
VMEM_LIMIT = 48 * 1024 * 1024

BF16 = jnp.bfloat16
F32 = jnp.float32


def _dot(a, b):
    return jnp.dot(a, b, preferred_element_type=F32)


def _dot_nt(a, b):
    return lax.dot_general(a, b, (((1,), (1,)), ((), ())), preferred_element_type=F32)


def _dot_tn(a, b):
    return lax.dot_general(a, b, (((0,), (0,)), ((), ())), preferred_element_type=F32)


def _split_terms(x, n):
    terms = []
    rem = x
    for _ in range(n):
        t = rem.astype(BF16)
        terms.append(t)
        rem = rem - t.astype(F32)
    return terms


def _dot_left_exact(m_bf16, x, n=3):
    acc = None
    for t in _split_terms(x, n):
        p = _dot(m_bf16, t)
        acc = p if acc is None else acc + p
    return acc


def _two_terms_on_lanes(x):
    lane = lax.broadcasted_iota(jnp.int32, x.shape, 1)
    xm = jnp.where(lane < SSM_HEADS, x, 0.0)
    hi = xm.astype(BF16).astype(F32)
    return (hi + pltpu.roll(xm - hi, SSM_HEADS, 1)).astype(BF16)


def _softplus(x):
    return jnp.maximum(x, 0.0) + jnp.log(1.0 + jnp.exp(-jnp.abs(x)))


def _sigmoid(x):
    return 0.5 * jnp.tanh(0.5 * x) + 0.5


def _silu(x):
    h = 0.5 * x
    return h + h * jnp.tanh(h)


def _lower_tri(n, strict):
    row = lax.broadcasted_iota(jnp.int32, (n, n), 0)
    col = lax.broadcasted_iota(jnp.int32, (n, n), 1)
    return (col < row) if strict else (col <= row)


def _lower_tri_ones(n):
    row = lax.broadcasted_iota(jnp.int32, (n, n), 0)
    col = lax.broadcasted_iota(jnp.int32, (n, n), 1)
    return jnp.clip(row - col + 1, 0, 1).astype(F32).astype(BF16)


def _normed_input(x_ref, gain_ref, h_ref):
    @pl.when(pl.program_id(1) == 0)
    def _():
        x = x_ref[...]
        ms = jnp.mean(x * x, axis=-1, keepdims=True)
        h_ref[...] = (x * lax.rsqrt(ms + EPS) * gain_ref[...]).astype(BF16)


def _delayed(o, halo, k):
    head = jnp.concatenate([halo, o[:SUBLANES]], axis=0)[SUBLANES - k:2 * SUBLANES - k]
    if o.shape[0] == SUBLANES:
        return head
    return jnp.concatenate([head, pltpu.roll(o, k, 0)[SUBLANES:]], axis=0)


def _zero_carry_at_sequence_start(carry_ref, row_blocks_per_seq):
    i, j = pl.program_id(0), pl.program_id(1)

    @pl.when(i % row_blocks_per_seq == 0)
    def _():
        carry_ref[j] = jnp.zeros(carry_ref.shape[1:], F32)


def _proj_plain_kernel(h_ref, wt_ref, o_ref):
    o_ref[...] = _dot_nt(h_ref[...], wt_ref[...])


def _proj_conv_kernel(row_blocks_per_seq, x_ref, gain_ref, wt_ref, cw_ref, cb_ref, o_ref, h_ref, carry_ref):
    _normed_input(x_ref, gain_ref, h_ref)
    _zero_carry_at_sequence_start(carry_ref, row_blocks_per_seq)
    j = pl.program_id(1)
    for c0 in range(0, o_ref.shape[1], PROJ_SUBTILE):
        cs = slice(c0, c0 + PROJ_SUBTILE)
        o = _dot_nt(h_ref[...], wt_ref[cs, :])
        halo = carry_ref[j, :, cs]
        carry_ref[j, :, cs] = o[o.shape[0] - SUBLANES:, :]
        w0, w1, w2, w3 = (0.5 * cw_ref[t:t + 1, cs] for t in range(CONV_WIDTH))
        e = _delayed(o, halo, 1)
        e_halo = _delayed(halo, jnp.zeros_like(halo), 1)
        inner = w1 * o + w0 * e
        inner_halo = w1 * halo + w0 * e_halo
        h = 0.5 * cb_ref[:, cs] + w3 * o + w2 * e + _delayed(inner, inner_halo, 2)
        o_ref[:, cs] = h + h * jnp.tanh(h)


def _proj_shift_kernel(row_blocks_per_seq, h_ref, wt_ref, mu_ref, o_ref, carry_ref):
    _zero_carry_at_sequence_start(carry_ref, row_blocks_per_seq)
    j = pl.program_id(1)
    o = _dot_nt(h_ref[...], wt_ref[...])
    halo = carry_ref[j]
    carry_ref[j] = o[o.shape[0] - SUBLANES:, :]
    o_ref[...] = o + mu_ref[...] * (_delayed(o, halo, 1) - o)


def _project_normed(kernel_fn, seq, x2d, pre_gain, wt, col_params):
    t, n = x2d.shape[0], wt.shape[0]
    tm, tn = PROJ_TM[n], PROJ_TN[n]
    return pl.pallas_call(
        functools.partial(kernel_fn, seq // tm),
        grid=(t // tm, n // tn),
        in_specs=[
            pl.BlockSpec((tm, D_MODEL), lambda i, j: (i, 0)),
            pl.BlockSpec((1, D_MODEL), lambda i, j: (0, 0)),
            pl.BlockSpec((tn, D_MODEL), lambda i, j: (j, 0)),
        ] + [pl.BlockSpec((c.shape[0], tn), lambda i, j: (0, j)) for c in col_params],
        out_specs=[pl.BlockSpec((tm, tn), lambda i, j: (i, j)),
                   pl.BlockSpec((tm, D_MODEL), lambda i, j: (i, 0))],
        out_shape=[jax.ShapeDtypeStruct((t, n), F32), jax.ShapeDtypeStruct((t, D_MODEL), BF16)],
        scratch_shapes=[pltpu.VMEM((n // tn, SUBLANES, tn), F32)],
        compiler_params=pltpu.CompilerParams(
            dimension_semantics=("arbitrary", "arbitrary"), vmem_limit_bytes=VMEM_LIMIT),
        name="in_proj_%d" % n,
    )(x2d, pre_gain, wt, *col_params)


def _project(kernel_fn, seq, h, wt, col_params, carry):
    t, n = h.shape[0], wt.shape[0]
    tm, tn = PROJ_TM[n], PROJ_TN[n]
    return pl.pallas_call(
        functools.partial(kernel_fn, seq // tm) if carry else kernel_fn,
        grid=(t // tm, n // tn),
        in_specs=[
            pl.BlockSpec((tm, D_MODEL), lambda i, j: (i, 0)),
            pl.BlockSpec((tn, D_MODEL), lambda i, j: (j, 0)),
        ] + [pl.BlockSpec((c.shape[0], tn), lambda i, j: (0, j)) for c in col_params],
        out_specs=pl.BlockSpec((tm, tn), lambda i, j: (i, j)),
        out_shape=jax.ShapeDtypeStruct((t, n), F32),
        scratch_shapes=[pltpu.VMEM((n // tn, SUBLANES, tn), F32)] if carry else [],
        compiler_params=pltpu.CompilerParams(
            dimension_semantics=("arbitrary", "arbitrary"), vmem_limit_bytes=VMEM_LIMIT),
        name="in_proj_%d" % n,
    )(h, wt, *col_params)


def _ssd_kernel(u_ref, z_ref, sm_ref, dtb_ref, alog_ref, dskip_ref, gain_ref, expand_ref,
                y_ref, st_ref, yd_ref):
    L = SSD_CHUNK

    @pl.when(pl.program_id(1) == 0)
    def _():
        st_ref[...] = jnp.zeros_like(st_ref)

    tri = _lower_tri(L, strict=False)
    tri_ones = _lower_tri_ones(L)
    head_a = lax.broadcasted_iota(jnp.int32, (L, LANES), 1) < SSM_HEAD_DIM
    for ci in range(SSD_CHUNKS_PER_STEP):
        rows = slice(ci * L, (ci + 1) * L)
        dt = _softplus(sm_ref[rows, LANES:2 * LANES] + dtb_ref[...])
        a = dt * (-jnp.exp(alog_ref[...]))
        acum = _dot_left_exact(tri_ones, a)
        acum_t = acum.T
        dt_t = dt.T
        a_last = acum[L - 1:L, :]
        dec_out_p = _two_terms_on_lanes(jnp.exp(acum))
        w_state_p = _two_terms_on_lanes(dt * jnp.exp(a_last - acum))

        for g in range(SSM_GROUPS):
            c0 = g * SSM_GROUP_WIDTH
            ex = expand_ref[:, c0:c0 + SSM_GROUP_WIDTH]
            xs = u_ref[rows, c0:c0 + SSM_GROUP_WIDTH]
            b0 = SSM_INNER + g * SSM_STATE
            bm = u_ref[rows, b0:b0 + SSM_STATE].astype(BF16)
            cm = u_ref[rows, b0 + SSM_GROUPS * SSM_STATE:b0 + (SSM_GROUPS + 1) * SSM_STATE].astype(BF16)
            dec_out_x = _dot(dec_out_p, ex)
            w_state_x = _dot(w_state_p, ex)
            xs_b = xs.astype(BF16)

            scores = _dot_nt(cm, bm)
            state = st_ref[g]
            y_off = _dot(cm, state.astype(BF16)) * dec_out_x
            st_ref[g] = state * dec_out_x[L - 1:L, :] + _dot_tn(bm, (xs * w_state_x).astype(BF16))

            def mix_of(h):
                seg = acum[:, h:h + 1] - acum_t[h:h + 1, :]
                return (scores * jnp.exp(jnp.where(tri, seg, -jnp.inf)) * dt_t[h:h + 1, :]).astype(BF16)

            for jp in range(SSM_GROUP_WIDTH // LANES):
                h = g * (SSM_GROUP_WIDTH // SSM_HEAD_DIM) + 2 * jp
                xp = xs_b[:, jp * LANES:(jp + 1) * LANES]
                x2 = jnp.concatenate([jnp.where(head_a, xp, 0.0), jnp.where(head_a, 0.0, xp)], axis=0)
                yd_ref[ci, :, jp * LANES:(jp + 1) * LANES] = _dot(
                    jnp.concatenate([mix_of(h), mix_of(h + 1)], axis=1), x2)

            y = yd_ref[ci] + y_off + dskip_ref[:, c0:c0 + SSM_GROUP_WIDTH] * xs
            z = z_ref[rows, c0:c0 + SSM_GROUP_WIDTH]
            y = y * _silu(z)
            ms = jnp.mean(y * y, axis=-1, keepdims=True)
            y_ref[rows, c0:c0 + SSM_GROUP_WIDTH] = (
                y * lax.rsqrt(ms + EPS) * gain_ref[:, c0:c0 + SSM_GROUP_WIDTH]).astype(BF16)


def _ssd_branch(u, zg, rw, batch, seq, dt_bias, a_log, dskip_x, norm_gain, expand):
    L = SSD_CHUNK * SSD_CHUNKS_PER_STEP
    nc = seq // L
    row = lambda b, c: b * nc + c
    const = lambda shape: pl.BlockSpec(shape, lambda b, c: (0, 0))
    return pl.pallas_call(
        _ssd_kernel,
        grid=(batch, nc),
        in_specs=[
            pl.BlockSpec((L, SSM_XBC), lambda b, c: (row(b, c), 0)),
            pl.BlockSpec((L, SSM_INNER), lambda b, c: (row(b, c), 0)),
            pl.BlockSpec((L, SMALL_WIDTH), lambda b, c: (row(b, c), 4 * RWKV_DIM // SMALL_WIDTH)),
            const((1, LANES)),
            const((1, LANES)),
            const((1, SSM_INNER)),
            const((1, SSM_INNER)),
            const((LANES, SSM_INNER)),
        ],
        out_specs=pl.BlockSpec((L, SSM_INNER), lambda b, c: (row(b, c), 0)),
        out_shape=jax.ShapeDtypeStruct((batch * seq, SSM_INNER), BF16),
        scratch_shapes=[
            pltpu.VMEM((SSM_GROUPS, SSM_STATE, SSM_GROUP_WIDTH), F32),
            pltpu.VMEM((SSD_CHUNKS_PER_STEP, SSD_CHUNK, SSM_GROUP_WIDTH), F32),
        ],
        compiler_params=pltpu.CompilerParams(
            dimension_semantics=("arbitrary", "arbitrary"), vmem_limit_bytes=VMEM_LIMIT),
        name="ssd_branch",
    )(u, zg, rw, dt_bias, a_log, dskip_x, norm_gain, expand)


def _rwkv_kernel(r_ref, k_ref, v_ref, g_ref, sm_ref, w0_ref, w2_ref, a0_ref, a2_ref, kks_ref, kas_ref,
                 rk_ref, gng_ref, gnb_ref, y_ref, s_ref):
    C = RWKV_CHUNK
    D = RWKV_HEAD_DIM
    NB = RWKV_BATCH_ROWS
    CP = RWKV_CHUNKS_PER_STEP
    R = NB * CP * C

    @pl.when(pl.program_id(1) == 0)
    def _():
        s_ref[...] = jnp.zeros_like(s_ref)

    rows = lambda x_ref: x_ref[...].reshape(R, x_ref.shape[-1])
    r, k, v, gate, sm = rows(r_ref), rows(k_ref), rows(v_ref), rows(g_ref), rows(sm_ref)
    wd = sm[:, 0:LORA]
    ad = sm[:, LORA:2 * LORA]

    lw = _dot(jnp.tanh(wd).astype(BF16), w2_ref[...])
    logw = (-0.5 * DECAY_SCALE) * jnp.tanh(0.5 * (w0_ref[...] + lw)) - 0.5 * DECAY_SCALE
    th_a = jnp.tanh(0.5 * (a0_ref[...] + _dot(ad.astype(BF16), a2_ref[...])))
    neg_a = -0.5 * th_a - 0.5
    kkp = k * kks_ref[...]
    half_ka = 0.5 * kas_ref[...]
    k2 = k * ((1.0 - half_ka) + half_ka * th_a)

    tri = _lower_tri_ones(C)
    cum = jnp.concatenate([_dot_left_exact(tri, logw[q * C:(q + 1) * C]) for q in range(NB * CP)], axis=0)
    e_pos = jnp.exp(cum)
    e_neg = 1.0 / e_pos
    kc_p = kkp * jnp.exp(cum - logw)
    rt = r * e_pos
    kt = k2 * e_neg
    nbt_p = kkp * neg_a * e_neg
    gamma_end = [jnp.exp(cum[(q + 1) * C - 1:(q + 1) * C, :]) for q in range(NB * CP)]
    rkk = r * k2 * rk_ref[...]
    gsilu = _silu(gate)

    P = 2 * D
    lane_a = lax.broadcasted_iota(jnp.int32, (C, P), 1) < D
    lane_a2 = lax.broadcasted_iota(jnp.int32, (2 * C, P), 1) < D
    row2 = lax.broadcasted_iota(jnp.int32, (2 * C, 2 * C), 0)
    col2 = lax.broadcasted_iota(jnp.int32, (2 * C, 2 * C), 1)
    colm = jnp.where(col2 >= C, col2 - C, col2)
    gmask = colm < jnp.where(row2 >= C, row2 - C + 1, row2)
    diag_blk = jnp.where(row2 >= C, 1, 0) == jnp.where(col2 >= C, 1, 0)
    anti_eye = jnp.where(col2 == jnp.where(row2 >= C, row2 - C, row2 + C), 1.0, 0.0)

    def seg_sum(x):
        sa = jnp.sum(jnp.where(lane_a, x, 0.0), axis=-1, keepdims=True)
        sb = jnp.sum(jnp.where(lane_a, 0.0, x), axis=-1, keepdims=True)
        return jnp.where(lane_a, sa, sb)

    def cross(lo, hi):
        return jnp.concatenate([lo, hi], axis=0).astype(BF16)

    units = [(b, j) for b in range(NB) for j in range(RWKV_HEADS // 2)]
    n_units = range(len(units))

    def chunk_chain(ci):
        idx = [(slice((b * CP + ci) * C, (b * CP + ci + 1) * C), slice(j * P, (j + 1) * P)) for b, j in units]
        xb, wa, g_top, g_bot, v_p = [], [], [], [], []
        for rs, sl in idx:
            kkp_p = kkp[rs, sl]
            rinv = jnp.minimum(lax.rsqrt(seg_sum(kkp_p * kkp_p)), 1e12)
            nbt = nbt_p[rs, sl] * rinv
            xp = jnp.concatenate([kc_p[rs, sl] * rinv, rt[rs, sl]], axis=0)
            w_a = jnp.concatenate([nbt, kt[rs, sl]], axis=0).astype(BF16)
            w_b = jnp.concatenate([kt[rs, sl], nbt], axis=0).astype(BF16)
            g_a = jnp.where(gmask, _dot_nt(jnp.where(lane_a2, xp, 0.0).astype(BF16), w_a), 0.0)
            g_b = jnp.where(gmask, _dot_nt(jnp.where(lane_a2, 0.0, xp).astype(BF16), w_b), 0.0)
            xb.append(xp.astype(BF16))
            wa.append(w_a)
            g_top.append(jnp.concatenate([g_a[:C], g_b[:C]], axis=0))
            g_bot.append(jnp.concatenate([g_a[C:], g_b[C:]], axis=0))
            v_p.append(v[rs, sl])

        z = [_dot_nt(xb[i], s_ref[units[i][0], units[i][1]].astype(BF16)) for i in n_units]
        w1 = []
        for i in n_units:
            rw = _dot(g_top[i].astype(BF16),
                      cross(jnp.where(lane_a, 0.0, v_p[i]), jnp.where(lane_a, v_p[i], 0.0)))
            w1.append(jnp.where(lane_a, rw[:C], rw[C:]))

        pt = [jnp.where(diag_blk, g_top[i], anti_eye) for i in n_units]
        for _ in range(6):
            pt = [_dot(jnp.where(diag_blk, pt[i], 0.0).astype(BF16), pt[i].astype(BF16))
                  + jnp.where(diag_blk, 0.0, pt[i]) for i in n_units]

        u = []
        for i in n_units:
            rhs = z[i][:C] + w1[i]
            ru = _dot(pt[i].astype(BF16), cross(jnp.where(lane_a, 0.0, rhs), jnp.where(lane_a, rhs, 0.0)))
            u.append(ru[:C] + ru[C:])
        for i in n_units:
            (b, j), (rs, sl) = units[i], idx[i]
            ry = _dot(g_bot[i].astype(BF16),
                      cross(jnp.where(lane_a, u[i], v_p[i]), jnp.where(lane_a, v_p[i], u[i])))
            y = z[i][C:] + jnp.where(lane_a, ry[:C], ry[C:])
            upd = _dot_tn(jnp.concatenate([u[i], v_p[i]], axis=0).astype(BF16), wa[i])
            s_ref[b, j] = (s_ref[b, j] + jnp.where(diag_blk, upd, 0.0)) * gamma_end[b * CP + ci][:, sl]

            yc = y - seg_sum(y) * (1.0 / D)
            var = seg_sum(yc * yc) * (1.0 / D)
            yn = yc * lax.rsqrt(var + GN_EPS) * gng_ref[:, sl] + gnb_ref[:, sl]
            bonus = seg_sum(rkk[rs, sl]) * v_p[i]
            y_ref[b, ci * C:(ci + 1) * C, sl] = ((yn + bonus) * gsilu[rs, sl]).astype(BF16)

    for ci in range(CP):
        chunk_chain(ci)


def _rwkv_branch(rw, batch, seq, w0, w2, a0, a2, k_k, k_a, r_k, gn_gain, gn_bias):
    C = RWKV_CHUNK
    NB = RWKV_BATCH_ROWS
    CP = RWKV_CHUNKS_PER_STEP
    rw3 = rw.reshape(batch, seq, RW_COLS)
    const = lambda shape: pl.BlockSpec(shape, lambda b, c: (0, 0))
    col_block = lambda j: pl.BlockSpec((NB, CP * C, RWKV_DIM), lambda b, c: (b, c, j))
    return pl.pallas_call(
        _rwkv_kernel,
        grid=(batch // NB, seq // (CP * C)),
        in_specs=[
            col_block(0), col_block(1), col_block(2), col_block(3),
            pl.BlockSpec((NB, CP * C, SMALL_WIDTH), lambda b, c: (b, c, 4 * RWKV_DIM // SMALL_WIDTH)),
            const((1, RWKV_DIM)), const((LORA, RWKV_DIM)), const((1, RWKV_DIM)), const((LORA, RWKV_DIM)),
            const((1, RWKV_DIM)), const((1, RWKV_DIM)), const((1, RWKV_DIM)),
            const((1, RWKV_DIM)), const((1, RWKV_DIM)),
        ],
        out_specs=pl.BlockSpec((NB, CP * C, RWKV_DIM), lambda b, c: (b, c, 0)),
        out_shape=jax.ShapeDtypeStruct((batch, seq, RWKV_DIM), BF16),
        scratch_shapes=[
            pltpu.VMEM((NB, RWKV_HEADS // 2, 2 * RWKV_HEAD_DIM, 2 * RWKV_HEAD_DIM), F32),
        ],
        compiler_params=pltpu.CompilerParams(
            dimension_semantics=("arbitrary", "arbitrary"), vmem_limit_bytes=VMEM_LIMIT),
        name="rwkv_branch",
    )(rw3, rw3, rw3, rw3, rw3, w0, w2, a0, a2, k_k, k_a, r_k, gn_gain, gn_bias
      ).reshape(batch * seq, RWKV_DIM)


def _merge_kernel(x_ref, ys_ref, yr_ref, g0_ref, g1_ref, bg_ref, ws_ref, wr_ref, wo_ref, gain_ref, o_ref):
    g_ssm = _sigmoid(g0_ref[...] + bg_ref[:, 0:D_MODEL])
    g_rwkv = _sigmoid(g1_ref[...] + bg_ref[:, D_MODEL:2 * D_MODEL])
    merged = g_ssm * _dot(ys_ref[...], ws_ref[...]) + g_rwkv * _dot(yr_ref[...], wr_ref[...])
    out = _dot(merged.astype(BF16), wo_ref[...])
    ms = jnp.mean(out * out, axis=-1, keepdims=True)
    o_ref[...] = x_ref[...] + out * lax.rsqrt(ms + EPS) * gain_ref[...]


def _merge(x2d, y_ssm, y_rwkv, zg, b_gate, w_ssm, w_rwkv, w_out, post_gain):
    t = x2d.shape[0]
    tm = MERGE_TM
    const = lambda shape: pl.BlockSpec(shape, lambda i: (0, 0), pipeline_mode=pl.Buffered(1))
    return pl.pallas_call(
        _merge_kernel,
        grid=(t // tm,),
        in_specs=[
            pl.BlockSpec((tm, D_MODEL), lambda i: (i, 0)),
            pl.BlockSpec((tm, SSM_INNER), lambda i: (i, 0)),
            pl.BlockSpec((tm, RWKV_DIM), lambda i: (i, 0)),
            pl.BlockSpec((tm, D_MODEL), lambda i: (i, SSM_INNER // D_MODEL)),
            pl.BlockSpec((tm, D_MODEL), lambda i: (i, SSM_INNER // D_MODEL + 1)),
            const((1, 2 * D_MODEL)),
            const((SSM_INNER, D_MODEL)), const((RWKV_DIM, D_MODEL)), const((D_MODEL, D_MODEL)),
            const((1, D_MODEL)),
        ],
        out_specs=pl.BlockSpec((tm, D_MODEL), lambda i: (i, 0)),
        out_shape=jax.ShapeDtypeStruct((t, D_MODEL), F32),
        compiler_params=pltpu.CompilerParams(
            dimension_semantics=("arbitrary",), vmem_limit_bytes=VMEM_LIMIT),
        name="gated_merge",
    )(x2d, y_ssm, y_rwkv, zg, zg, b_gate, w_ssm, w_rwkv, w_out, post_gain)


def _pad_lanes(v, width):
    return jnp.pad(v, ((0, 0), (0, width - v.shape[-1])))


W_XBC = SSM_INNER
W_DT = W_XBC + SSM_XBC
W_RW = W_DT + SSM_HEADS
W_GATE = W_RW + 4 * RWKV_DIM + 2 * LORA
W_COLS = W_GATE + 2 * D_MODEL


def _split_projection_weights(w_in):
    wt = jnp.swapaxes(w_in, 0, 1)
    zeros = jnp.zeros((RW_COLS - (W_GATE - W_DT), D_MODEL), w_in.dtype)
    wt_xbc = wt[W_XBC:W_DT]
    wt_rw = jnp.concatenate([wt[W_RW:W_GATE], wt[W_DT:W_RW], zeros], axis=0)
    wt_zg = jnp.concatenate([wt[:W_XBC], wt[W_GATE:]], axis=0)
    return wt_xbc.astype(BF16), wt_rw.astype(BF16), wt_zg.astype(BF16)


def _layer(x, pre_gain, w_in, b_gate, conv_w, conv_b, dt_bias, a_log, d_skip, ssm_norm_gain,
           rwkv_mu, decay_w0, decay_w2, iclr_a0, iclr_a2, k_k, k_a, r_k, gn_gain, gn_bias,
           w_branch_ssm, w_branch_rwkv, w_out, post_gain):
    batch, seq, _ = x.shape
    x2d = x.reshape(batch * seq, D_MODEL)
    row = lambda v: v.reshape(1, -1)

    w_xbc, w_rw, w_zg = _split_projection_weights(w_in)
    mu = _pad_lanes(row(rwkv_mu), RW_COLS)

    gain = row(pre_gain)
    u, h = _project_normed(_proj_conv_kernel, seq, x2d, gain, w_xbc, [conv_w, row(conv_b)])
    rw = _project(_proj_shift_kernel, seq, h, w_rw, [mu], True)
    zg = _project(_proj_plain_kernel, seq, h, w_zg, [], False)

    head_of_col = jnp.arange(SSM_INNER) // SSM_HEAD_DIM
    expand = ((jnp.arange(LANES)[:, None] % SSM_HEADS == head_of_col[None, :])
              & (jnp.arange(LANES)[:, None] < 2 * SSM_HEADS)).astype(BF16)
    y_ssm = _ssd_branch(
        u, zg, rw, batch, seq, _pad_lanes(row(dt_bias), LANES), _pad_lanes(row(a_log), LANES),
        row(jnp.repeat(d_skip, SSM_HEAD_DIM)), row(ssm_norm_gain), expand)

    y_rwkv = _rwkv_branch(
        rw, batch, seq, row(decay_w0), decay_w2.astype(BF16), row(iclr_a0), iclr_a2.astype(BF16),
        row(k_k), row(k_a), row(r_k), row(gn_gain), row(gn_bias))

    out = _merge(x2d, y_ssm, y_rwkv, zg, row(b_gate), w_branch_ssm.astype(BF16),
                 w_branch_rwkv.astype(BF16), w_out.astype(BF16), row(post_gain))
    return out.reshape(batch, seq, D_MODEL)


def kernel(x, pre_gain, w_in, b_gate, conv_w, conv_b, dt_bias, a_log, d_skip, ssm_norm_gain, rwkv_mu,
           decay_w0, decay_w2, iclr_a0, iclr_a2, k_k, k_a, r_k, gn_gain, gn_bias, w_branch_ssm,
           w_branch_rwkv, w_out, post_gain):
    for layer in range(pre_gain.shape[0]):
        x = _layer(
            x, pre_gain[layer], w_in[layer], b_gate[layer], conv_w[layer], conv_b[layer], dt_bias[layer],
            a_log[layer], d_skip[layer], ssm_norm_gain[layer], rwkv_mu[layer], decay_w0[layer],
            decay_w2[layer], iclr_a0[layer], iclr_a2[layer], k_k[layer], k_a[layer], r_k[layer],
            gn_gain[layer], gn_bias[layer], w_branch_ssm[layer], w_branch_rwkv[layer], w_out[layer],
            post_gain[layer])
    return x
```

```python
import functools

import jax
import jax.numpy as jnp
from jax import lax
from jax.experimental import pallas as pl
from jax.experimental.pallas import tpu as pltpu

D_MODEL = 1024
EPS = 1e-6

SSM_INNER = 2048
SSM_HEAD_DIM = 64
SSM_HEADS = 32
SSM_STATE = 128
SSM_GROUPS = 4
SSM_GROUP_WIDTH = SSM_INNER // SSM_GROUPS
CONV_WIDTH = 4
SSM_XBC = 3072

RWKV_DIM = 1024
RWKV_HEAD_DIM = 64
RWKV_HEADS = 16
LORA = 64
GN_EPS = RWKV_HEAD_DIM * 1e-5
DECAY_SCALE = 0.6065306597126334

LANES = 128
SUBLANES = 8

SMALL_WIDTH = 256
RW_PAD = 256
RW_COLS = 4 * RWKV_DIM + SMALL_WIDTH + RW_PAD
ZG_COLS = SSM_INNER + 2 * D_MODEL

PROJ_TM = {SSM_XBC: 1024, RW_COLS: 2048, ZG_COLS: 2048}
PROJ_SUBTILE = 256
PROJ_TN = {SSM_XBC: 1024, RW_COLS: RW_COLS // 6, ZG_COLS: 1024}
SSD_CHUNK = 128
SSD_CHUNKS_PER_STEP = 4
RWKV_CHUNK = 64
RWKV_BATCH_ROWS = 2
RWKV_CHUNKS_PER_STEP = 2
MERGE_TM = 512
VMEM_LIMIT = 48 * 1024 * 1024

BF16 = jnp.bfloat16
F32 = jnp.float32


def _dot(a, b):
    return jnp.dot(a, b, preferred_element_type=F32)


def _dot_nt(a, b):
    return lax.dot_general(a, b, (((1,), (1,)), ((), ())), preferred_element_type=F32)


def _dot_tn(a, b):
    return lax.dot_general(a, b, (((0,), (0,)), ((), ())), preferred_element_type=F32)


def _split_terms(x, n):
    terms = []
    rem = x
    for _ in range(n):
        t = rem.astype(BF16)
        terms.append(t)
        rem = rem - t.astype(F32)
    return terms


def _dot_left_exact(m_bf16, x, n=3):
    acc = None
    for t in _split_terms(x, n):
        p = _dot(m_bf16, t)
        acc = p if acc is None else acc + p
    return acc


def _two_terms_on_lanes(x):
    lane = lax.broadcasted_iota(jnp.int32, x.shape, 1)
    xm = jnp.where(lane < SSM_HEADS, x, 0.0)
    hi = xm.astype(BF16).astype(F32)
    return (hi + pltpu.roll(xm - hi, SSM_HEADS, 1)).astype(BF16)


def _softplus(x):
    return jnp.maximum(x, 0.0) + jnp.log(1.0 + jnp.exp(-jnp.abs(x)))


def _sigmoid(x):
    return 0.5 * jnp.tanh(0.5 * x) + 0.5


def _silu(x):
    h = 0.5 * x
    return h + h * jnp.tanh(h)


def _lower_tri(n, strict):
    row = lax.broadcasted_iota(jnp.int32, (n, n), 0)
    col = lax.broadcasted_iota(jnp.int32, (n, n), 1)
    return (col < row) if strict else (col <= row)


def _lower_tri_ones(n):
    row = lax.broadcasted_iota(jnp.int32, (n, n), 0)
    col = lax.broadcasted_iota(jnp.int32, (n, n), 1)
    return jnp.clip(row - col + 1, 0, 1).astype(F32).astype(BF16)


def _normed_input(x_ref, gain_ref, h_ref):
    @pl.when(pl.program_id(1) == 0)
    def _():
        x = x_ref[...]
        ms = jnp.mean(x * x, axis=-1, keepdims=True)
        h_ref[...] = (x * lax.rsqrt(ms + EPS) * gain_ref[...]).astype(BF16)


def _delayed(o, halo, k):
    head = jnp.concatenate([halo, o[:SUBLANES]], axis=0)[SUBLANES - k:2 * SUBLANES - k]
    if o.shape[0] == SUBLANES:
        return head
    return jnp.concatenate([head, pltpu.roll(o, k, 0)[SUBLANES:]], axis=0)


def _zero_carry_at_sequence_start(carry_ref, row_blocks_per_seq):
    i, j = pl.program_id(0), pl.program_id(1)

    @pl.when(i % row_blocks_per_seq == 0)
    def _():
        carry_ref[j] = jnp.zeros(carry_ref.shape[1:], F32)


def _proj_plain_kernel(h_ref, w_ref, o_ref):
    o_ref[...] = _dot(h_ref[...], w_ref[...])


def _proj_conv_kernel(row_blocks_per_seq, x_ref, gain_ref, w_ref, cw_ref, cb_ref, o_ref, h_ref, carry_ref):
    _normed_input(x_ref, gain_ref, h_ref)
    _zero_carry_at_sequence_start(carry_ref, row_blocks_per_seq)
    j = pl.program_id(1)
    for c0 in range(0, o_ref.shape[1], PROJ_SUBTILE):
        cs = slice(c0, c0 + PROJ_SUBTILE)
        o = _dot(h_ref[...], w_ref[:, cs])
        halo = carry_ref[j, :, cs]
        carry_ref[j, :, cs] = o[o.shape[0] - SUBLANES:, :]
        w0, w1, w2, w3 = (0.5 * cw_ref[t:t + 1, cs] for t in range(CONV_WIDTH))
        e = _delayed(o, halo, 1)
        e_halo = _delayed(halo, jnp.zeros_like(halo), 1)
        inner = w1 * o + w0 * e
        inner_halo = w1 * halo + w0 * e_halo
        h = 0.5 * cb_ref[:, cs] + w3 * o + w2 * e + _delayed(inner, inner_halo, 2)
        o_ref[:, cs] = h + h * jnp.tanh(h)


def _proj_shift_kernel(row_blocks_per_seq, h_ref, w_ref, mu_ref, o_ref, carry_ref):
    _zero_carry_at_sequence_start(carry_ref, row_blocks_per_seq)
    j = pl.program_id(1)
    o = _dot(h_ref[...], w_ref[...])
    halo = carry_ref[j]
    carry_ref[j] = o[o.shape[0] - SUBLANES:, :]
    o_ref[...] = o + mu_ref[...] * (_delayed(o, halo, 1) - o)


def _project_normed(kernel_fn, seq, x2d, pre_gain, w, col_params):
    t, n = x2d.shape[0], w.shape[1]
    tm, tn = PROJ_TM[n], PROJ_TN[n]
    return pl.pallas_call(
        functools.partial(kernel_fn, seq // tm),
        grid=(t // tm, n // tn),
        in_specs=[
            pl.BlockSpec((tm, D_MODEL), lambda i, j: (i, 0)),
            pl.BlockSpec((1, D_MODEL), lambda i, j: (0, 0)),
            pl.BlockSpec((D_MODEL, tn), lambda i, j: (0, j)),
        ] + [pl.BlockSpec((c.shape[0], tn), lambda i, j: (0, j)) for c in col_params],
        out_specs=[pl.BlockSpec((tm, tn), lambda i, j: (i, j)),
                   pl.BlockSpec((tm, D_MODEL), lambda i, j: (i, 0))],
        out_shape=[jax.ShapeDtypeStruct((t, n), F32), jax.ShapeDtypeStruct((t, D_MODEL), BF16)],
        scratch_shapes=[pltpu.VMEM((n // tn, SUBLANES, tn), F32)],
        compiler_params=pltpu.CompilerParams(
            dimension_semantics=("arbitrary", "arbitrary"), vmem_limit_bytes=VMEM_LIMIT),
        name="in_proj_%d" % n,
    )(x2d, pre_gain, w, *col_params)


def _project(kernel_fn, seq, h, w, col_params, carry):
    t, n = h.shape[0], w.shape[1]
    tm, tn = PROJ_TM[n], PROJ_TN[n]
    return pl.pallas_call(
        functools.partial(kernel_fn, seq // tm) if carry else kernel_fn,
        grid=(t // tm, n // tn),
        in_specs=[
            pl.BlockSpec((tm, D_MODEL), lambda i, j: (i, 0)),
            pl.BlockSpec((D_MODEL, tn), lambda i, j: (0, j)),
        ] + [pl.BlockSpec((c.shape[0], tn), lambda i, j: (0, j)) for c in col_params],
        out_specs=pl.BlockSpec((tm, tn), lambda i, j: (i, j)),
        out_shape=jax.ShapeDtypeStruct((t, n), F32),
        scratch_shapes=[pltpu.VMEM((n // tn, SUBLANES, tn), F32)] if carry else [],
        compiler_params=pltpu.CompilerParams(
            dimension_semantics=("arbitrary", "arbitrary"), vmem_limit_bytes=VMEM_LIMIT),
        name="in_proj_%d" % n,
    )(h, w, *col_params)


def _ssd_kernel(u_ref, z_ref, sm_ref, dtb_ref, alog_ref, dskip_ref, gain_ref, expand_ref,
                y_ref, st_ref, yd_ref):
    L = SSD_CHUNK

    @pl.when(pl.program_id(1) == 0)
    def _():
        st_ref[...] = jnp.zeros_like(st_ref)

    tri = _lower_tri(L, strict=False)
    tri_ones = _lower_tri_ones(L)
    head_a = lax.broadcasted_iota(jnp.int32, (L, LANES), 1) < SSM_HEAD_DIM
    for ci in range(SSD_CHUNKS_PER_STEP):
        rows = slice(ci * L, (ci + 1) * L)
        dt = _softplus(sm_ref[rows, LANES:2 * LANES] + dtb_ref[...])
        a = dt * (-jnp.exp(alog_ref[...]))
        acum = _dot_left_exact(tri_ones, a)
        acum_t = acum.T
        dt_t = dt.T
        a_last = acum[L - 1:L, :]
        dec_out_p = _two_terms_on_lanes(jnp.exp(acum))
        w_state_p = _two_terms_on_lanes(dt * jnp.exp(a_last - acum))

        for g in range(SSM_GROUPS):
            c0 = g * SSM_GROUP_WIDTH
            ex = expand_ref[:, c0:c0 + SSM_GROUP_WIDTH]
            xs = u_ref[rows, c0:c0 + SSM_GROUP_WIDTH]
            b0 = SSM_INNER + g * SSM_STATE
            bm = u_ref[rows, b0:b0 + SSM_STATE].astype(BF16)
            cm = u_ref[rows, b0 + SSM_GROUPS * SSM_STATE:b0 + (SSM_GROUPS + 1) * SSM_STATE].astype(BF16)
            dec_out_x = _dot(dec_out_p, ex)
            w_state_x = _dot(w_state_p, ex)
            xs_b = xs.astype(BF16)

            scores = _dot_nt(cm, bm)
            state = st_ref[g]
            y_off = _dot(cm, state.astype(BF16)) * dec_out_x
            st_ref[g] = state * dec_out_x[L - 1:L, :] + _dot_tn(bm, (xs * w_state_x).astype(BF16))

            def mix_of(h):
                seg = acum[:, h:h + 1] - acum_t[h:h + 1, :]
                return (scores * jnp.exp(jnp.where(tri, seg, -jnp.inf)) * dt_t[h:h + 1, :]).astype(BF16)

            for jp in range(SSM_GROUP_WIDTH // LANES):
                h = g * (SSM_GROUP_WIDTH // SSM_HEAD_DIM) + 2 * jp
                xp = xs_b[:, jp * LANES:(jp + 1) * LANES]
                x2 = jnp.concatenate([jnp.where(head_a, xp, 0.0), jnp.where(head_a, 0.0, xp)], axis=0)
                yd_ref[ci, :, jp * LANES:(jp + 1) * LANES] = _dot(
                    jnp.concatenate([mix_of(h), mix_of(h + 1)], axis=1), x2)

            y = yd_ref[ci] + y_off + dskip_ref[:, c0:c0 + SSM_GROUP_WIDTH] * xs
            z = z_ref[rows, c0:c0 + SSM_GROUP_WIDTH]
            y = y * _silu(z)
            ms = jnp.mean(y * y, axis=-1, keepdims=True)
            y_ref[rows, c0:c0 + SSM_GROUP_WIDTH] = (
                y * lax.rsqrt(ms + EPS) * gain_ref[:, c0:c0 + SSM_GROUP_WIDTH]).astype(BF16)


def _ssd_branch(u, zg, rw, batch, seq, dt_bias, a_log, dskip_x, norm_gain, expand):
    L = SSD_CHUNK * SSD_CHUNKS_PER_STEP
    nc = seq // L
    row = lambda b, c: b * nc + c
    const = lambda shape: pl.BlockSpec(shape, lambda b, c: (0, 0))
    return pl.pallas_call(
        _ssd_kernel,
        grid=(batch, nc),
        in_specs=[
            pl.BlockSpec((L, SSM_XBC), lambda b, c: (row(b, c), 0)),
            pl.BlockSpec((L, SSM_INNER), lambda b, c: (row(b, c), 0)),
            pl.BlockSpec((L, SMALL_WIDTH), lambda b, c: (row(b, c), 4 * RWKV_DIM // SMALL_WIDTH)),
            const((1, LANES)),
            const((1, LANES)),
            const((1, SSM_INNER)),
            const((1, SSM_INNER)),
            const((LANES, SSM_INNER)),
        ],
        out_specs=pl.BlockSpec((L, SSM_INNER), lambda b, c: (row(b, c), 0)),
        out_shape=jax.ShapeDtypeStruct((batch * seq, SSM_INNER), BF16),
        scratch_shapes=[
            pltpu.VMEM((SSM_GROUPS, SSM_STATE, SSM_GROUP_WIDTH), F32),
            pltpu.VMEM((SSD_CHUNKS_PER_STEP, SSD_CHUNK, SSM_GROUP_WIDTH), F32),
        ],
        compiler_params=pltpu.CompilerParams(
            dimension_semantics=("arbitrary", "arbitrary"), vmem_limit_bytes=VMEM_LIMIT),
        name="ssd_branch",
    )(u, zg, rw, dt_bias, a_log, dskip_x, norm_gain, expand)


def _rwkv_kernel(r_ref, k_ref, v_ref, g_ref, sm_ref, w0_ref, w2_ref, a0_ref, a2_ref, kks_ref, kas_ref,
                 rk_ref, gng_ref, gnb_ref, y_ref, s_ref):
    C = RWKV_CHUNK
    D = RWKV_HEAD_DIM
    NB = RWKV_BATCH_ROWS
    CP = RWKV_CHUNKS_PER_STEP
    R = NB * CP * C

    @pl.when(pl.program_id(1) == 0)
    def _():
        s_ref[...] = jnp.zeros_like(s_ref)

    rows = lambda x_ref: x_ref[...].reshape(R, x_ref.shape[-1])
    r, k, v, gate, sm = rows(r_ref), rows(k_ref), rows(v_ref), rows(g_ref), rows(sm_ref)
    wd = sm[:, 0:LORA]
    ad = sm[:, LORA:2 * LORA]

    lw = _dot(jnp.tanh(wd).astype(BF16), w2_ref[...])
    logw = (-0.5 * DECAY_SCALE) * jnp.tanh(0.5 * (w0_ref[...] + lw)) - 0.5 * DECAY_SCALE
    th_a = jnp.tanh(0.5 * (a0_ref[...] + _dot(ad.astype(BF16), a2_ref[...])))
    neg_a = -0.5 * th_a - 0.5
    kkp = k * kks_ref[...]
    half_ka = 0.5 * kas_ref[...]
    k2 = k * ((1.0 - half_ka) + half_ka * th_a)

    tri = _lower_tri_ones(C)
    cum = jnp.concatenate([_dot_left_exact(tri, logw[q * C:(q + 1) * C]) for q in range(NB * CP)], axis=0)
    e_pos = jnp.exp(cum)
    e_neg = 1.0 / e_pos
    kc_p = kkp * jnp.exp(cum - logw)
    rt = r * e_pos
    kt = k2 * e_neg
    nbt_p = kkp * neg_a * e_neg
    gamma_end = [jnp.exp(cum[(q + 1) * C - 1:(q + 1) * C, :]) for q in range(NB * CP)]
    rkk = r * k2 * rk_ref[...]
    gsilu = _silu(gate)

    P = 2 * D
    lane_a = lax.broadcasted_iota(jnp.int32, (C, P), 1) < D
    lane_a2 = lax.broadcasted_iota(jnp.int32, (2 * C, P), 1) < D
    row2 = lax.broadcasted_iota(jnp.int32, (2 * C, 2 * C), 0)
    col2 = lax.broadcasted_iota(jnp.int32, (2 * C, 2 * C), 1)
    colm = jnp.where(col2 >= C, col2 - C, col2)
    gmask = colm < jnp.where(row2 >= C, row2 - C + 1, row2)
    diag_blk = jnp.where(row2 >= C, 1, 0) == jnp.where(col2 >= C, 1, 0)
    anti_eye = jnp.where(col2 == jnp.where(row2 >= C, row2 - C, row2 + C), 1.0, 0.0)

    def seg_sum(x):
        sa = jnp.sum(jnp.where(lane_a, x, 0.0), axis=-1, keepdims=True)
        sb = jnp.sum(jnp.where(lane_a, 0.0, x), axis=-1, keepdims=True)
        return jnp.where(lane_a, sa, sb)

    def cross(lo, hi):
        return jnp.concatenate([lo, hi], axis=0).astype(BF16)

    units = [(b, j) for b in range(NB) for j in range(RWKV_HEADS // 2)]
    n_units = range(len(units))

    def chunk_chain(ci):
        idx = [(slice((b * CP + ci) * C, (b * CP + ci + 1) * C), slice(j * P, (j + 1) * P)) for b, j in units]
        xb, wa, g_top, g_bot, v_p = [], [], [], [], []
        for rs, sl in idx:
            kkp_p = kkp[rs, sl]
            rinv = jnp.minimum(lax.rsqrt(seg_sum(kkp_p * kkp_p)), 1e12)
            nbt = nbt_p[rs, sl] * rinv
            xp = jnp.concatenate([kc_p[rs, sl] * rinv, rt[rs, sl]], axis=0)
            w_a = jnp.concatenate([nbt, kt[rs, sl]], axis=0).astype(BF16)
            w_b = jnp.concatenate([kt[rs, sl], nbt], axis=0).astype(BF16)
            g_a = jnp.where(gmask, _dot_nt(jnp.where(lane_a2, xp, 0.0).astype(BF16), w_a), 0.0)
            g_b = jnp.where(gmask, _dot_nt(jnp.where(lane_a2, 0.0, xp).astype(BF16), w_b), 0.0)
            xb.append(xp.astype(BF16))
            wa.append(w_a)
            g_top.append(jnp.concatenate([g_a[:C], g_b[:C]], axis=0))
            g_bot.append(jnp.concatenate([g_a[C:], g_b[C:]], axis=0))
            v_p.append(v[rs, sl])

        z = [_dot_nt(xb[i], s_ref[units[i][0], units[i][1]].astype(BF16)) for i in n_units]
        w1 = []
        for i in n_units:
            rw = _dot(g_top[i].astype(BF16),
                      cross(jnp.where(lane_a, 0.0, v_p[i]), jnp.where(lane_a, v_p[i], 0.0)))
            w1.append(jnp.where(lane_a, rw[:C], rw[C:]))

        pt = [jnp.where(diag_blk, g_top[i], anti_eye) for i in n_units]
        for _ in range(6):
            pt = [_dot(jnp.where(diag_blk, pt[i], 0.0).astype(BF16), pt[i].astype(BF16))
                  + jnp.where(diag_blk, 0.0, pt[i]) for i in n_units]

        u = []
        for i in n_units:
            rhs = z[i][:C] + w1[i]
            ru = _dot(pt[i].astype(BF16), cross(jnp.where(lane_a, 0.0, rhs), jnp.where(lane_a, rhs, 0.0)))
            u.append(ru[:C] + ru[C:])
        for i in n_units:
            (b, j), (rs, sl) = units[i], idx[i]
            ry = _dot(g_bot[i].astype(BF16),
                      cross(jnp.where(lane_a, u[i], v_p[i]), jnp.where(lane_a, v_p[i], u[i])))
            y = z[i][C:] + jnp.where(lane_a, ry[:C], ry[C:])
            upd = _dot_tn(jnp.concatenate([u[i], v_p[i]], axis=0).astype(BF16), wa[i])
            s_ref[b, j] = (s_ref[b, j] + jnp.where(diag_blk, upd, 0.0)) * gamma_end[b * CP + ci][:, sl]

            yc = y - seg_sum(y) * (1.0 / D)
            var = seg_sum(yc * yc) * (1.0 / D)
            yn = yc * lax.rsqrt(var + GN_EPS) * gng_ref[:, sl] + gnb_ref[:, sl]
            bonus = seg_sum(rkk[rs, sl]) * v_p[i]
            y_ref[b, ci * C:(ci + 1) * C, sl] = ((yn + bonus) * gsilu[rs, sl]).astype(BF16)

    for ci in range(CP):
        chunk_chain(ci)


def _rwkv_branch(rw, batch, seq, w0, w2, a0, a2, k_k, k_a, r_k, gn_gain, gn_bias):
    C = RWKV_CHUNK
    NB = RWKV_BATCH_ROWS
    CP = RWKV_CHUNKS_PER_STEP
    rw3 = rw.reshape(batch, seq, RW_COLS)
    const = lambda shape: pl.BlockSpec(shape, lambda b, c: (0, 0))
    col_block = lambda j: pl.BlockSpec((NB, CP * C, RWKV_DIM), lambda b, c: (b, c, j))
    return pl.pallas_call(
        _rwkv_kernel,
        grid=(batch // NB, seq // (CP * C)),
        in_specs=[
            col_block(0), col_block(1), col_block(2), col_block(3),
            pl.BlockSpec((NB, CP * C, SMALL_WIDTH), lambda b, c: (b, c, 4 * RWKV_DIM // SMALL_WIDTH)),
            const((1, RWKV_DIM)), const((LORA, RWKV_DIM)), const((1, RWKV_DIM)), const((LORA, RWKV_DIM)),
            const((1, RWKV_DIM)), const((1, RWKV_DIM)), const((1, RWKV_DIM)),
            const((1, RWKV_DIM)), const((1, RWKV_DIM)),
        ],
        out_specs=pl.BlockSpec((NB, CP * C, RWKV_DIM), lambda b, c: (b, c, 0)),
        out_shape=jax.ShapeDtypeStruct((batch, seq, RWKV_DIM), BF16),
        scratch_shapes=[
            pltpu.VMEM((NB, RWKV_HEADS // 2, 2 * RWKV_HEAD_DIM, 2 * RWKV_HEAD_DIM), F32),
        ],
        compiler_params=pltpu.CompilerParams(
            dimension_semantics=("arbitrary", "arbitrary"), vmem_limit_bytes=VMEM_LIMIT),
        name="rwkv_branch",
    )(rw3, rw3, rw3, rw3, rw3, w0, w2, a0, a2, k_k, k_a, r_k, gn_gain, gn_bias
      ).reshape(batch * seq, RWKV_DIM)


def _merge_kernel(x_ref, ys_ref, yr_ref, g0_ref, g1_ref, bg_ref, ws_ref, wr_ref, wo_ref, gain_ref, o_ref):
    g_ssm = _sigmoid(g0_ref[...] + bg_ref[:, 0:D_MODEL])
    g_rwkv = _sigmoid(g1_ref[...] + bg_ref[:, D_MODEL:2 * D_MODEL])
    merged = g_ssm * _dot(ys_ref[...], ws_ref[...]) + g_rwkv * _dot(yr_ref[...], wr_ref[...])
    out = _dot(merged.astype(BF16), wo_ref[...])
    ms = jnp.mean(out * out, axis=-1, keepdims=True)
    o_ref[...] = x_ref[...] + out * lax.rsqrt(ms + EPS) * gain_ref[...]


def _merge(x2d, y_ssm, y_rwkv, zg, b_gate, w_ssm, w_rwkv, w_out, post_gain):
    t = x2d.shape[0]
    tm = MERGE_TM
    const = lambda shape: pl.BlockSpec(shape, lambda i: (0, 0), pipeline_mode=pl.Buffered(1))
    return pl.pallas_call(
        _merge_kernel,
        grid=(t // tm,),
        in_specs=[
            pl.BlockSpec((tm, D_MODEL), lambda i: (i, 0)),
            pl.BlockSpec((tm, SSM_INNER), lambda i: (i, 0)),
            pl.BlockSpec((tm, RWKV_DIM), lambda i: (i, 0)),
            pl.BlockSpec((tm, D_MODEL), lambda i: (i, SSM_INNER // D_MODEL)),
            pl.BlockSpec((tm, D_MODEL), lambda i: (i, SSM_INNER // D_MODEL + 1)),
            const((1, 2 * D_MODEL)),
            const((SSM_INNER, D_MODEL)), const((RWKV_DIM, D_MODEL)), const((D_MODEL, D_MODEL)),
            const((1, D_MODEL)),
        ],
        out_specs=pl.BlockSpec((tm, D_MODEL), lambda i: (i, 0)),
        out_shape=jax.ShapeDtypeStruct((t, D_MODEL), F32),
        compiler_params=pltpu.CompilerParams(
            dimension_semantics=("arbitrary",), vmem_limit_bytes=VMEM_LIMIT),
        name="gated_merge",
    )(x2d, y_ssm, y_rwkv, zg, zg, b_gate, w_ssm, w_rwkv, w_out, post_gain)


def _pad_lanes(v, width):
    return jnp.pad(v, ((0, 0), (0, width - v.shape[-1])))


W_XBC = SSM_INNER
W_DT = W_XBC + SSM_XBC
W_RW = W_DT + SSM_HEADS
W_GATE = W_RW + 4 * RWKV_DIM + 2 * LORA
W_COLS = W_GATE + 2 * D_MODEL
WEIGHT_ROWS = 128


def _lane_aligned(col):
    return col // LANES * LANES


def _weights_kernel(w_ref, xbc_ref, rw_ref, zg_ref):
    xbc_ref[...] = w_ref[:, W_XBC:W_DT]
    zg_ref[:, 0:SSM_INNER] = w_ref[:, 0:W_XBC]
    g0 = _lane_aligned(W_GATE)
    zg_ref[:, SSM_INNER:ZG_COLS] = w_ref[:, g0:W_COLS][:, W_GATE - g0:W_COLS - g0]
    n_rw = W_GATE - W_RW
    r0 = _lane_aligned(W_RW)
    r1 = _lane_aligned(W_GATE + LANES - 1)
    rw_ref[:, 0:n_rw] = w_ref[:, r0:r1][:, W_RW - r0:W_GATE - r0]
    dt_tile = w_ref[:, W_DT:W_DT + LANES]
    lane = lax.broadcasted_iota(jnp.int32, dt_tile.shape, 1)
    rw_ref[:, n_rw:n_rw + LANES] = jnp.where(lane < SSM_HEADS, dt_tile, jnp.zeros_like(dt_tile))
    rw_ref[:, n_rw + LANES:RW_COLS] = jnp.zeros((w_ref.shape[0], RW_COLS - n_rw - LANES), BF16)


def _split_projection_weights(w_in):
    rows = lambda n: pl.BlockSpec((WEIGHT_ROWS, n), lambda i: (i, 0))
    return pl.pallas_call(
        _weights_kernel,
        grid=(D_MODEL // WEIGHT_ROWS,),
        in_specs=[rows(W_COLS)],
        out_specs=[rows(SSM_XBC), rows(RW_COLS), rows(ZG_COLS)],
        out_shape=[jax.ShapeDtypeStruct((D_MODEL, n), BF16) for n in (SSM_XBC, RW_COLS, ZG_COLS)],
        compiler_params=pltpu.CompilerParams(dimension_semantics=("arbitrary",), vmem_limit_bytes=VMEM_LIMIT),
        name="weight_split",
    )(w_in)


def _layer(x, pre_gain, w_in, b_gate, conv_w, conv_b, dt_bias, a_log, d_skip, ssm_norm_gain,
           rwkv_mu, decay_w0, decay_w2, iclr_a0, iclr_a2, k_k, k_a, r_k, gn_gain, gn_bias,
           w_branch_ssm, w_branch_rwkv, w_out, post_gain):
    batch, seq, _ = x.shape
    x2d = x.reshape(batch * seq, D_MODEL)
    row = lambda v: v.reshape(1, -1)

    w_xbc, w_rw, w_zg = _split_projection_weights(w_in.astype(BF16))
    mu = _pad_lanes(row(rwkv_mu), RW_COLS)

    gain = row(pre_gain)
    u, h = _project_normed(_proj_conv_kernel, seq, x2d, gain, w_xbc, [conv_w, row(conv_b)])
    rw = _project(_proj_shift_kernel, seq, h, w_rw, [mu], True)
    zg = _project(_proj_plain_kernel, seq, h, w_zg, [], False)

    head_of_col = jnp.arange(SSM_INNER) // SSM_HEAD_DIM
    expand = ((jnp.arange(LANES)[:, None] % SSM_HEADS == head_of_col[None, :])
              & (jnp.arange(LANES)[:, None] < 2 * SSM_HEADS)).astype(BF16)
    y_ssm = _ssd_branch(
        u, zg, rw, batch, seq, _pad_lanes(row(dt_bias), LANES), _pad_lanes(row(a_log), LANES),
        row(jnp.repeat(d_skip, SSM_HEAD_DIM)), row(ssm_norm_gain), expand)

    y_rwkv = _rwkv_branch(
        rw, batch, seq, row(decay_w0), decay_w2.astype(BF16), row(iclr_a0), iclr_a2.astype(BF16),
        row(k_k), row(k_a), row(r_k), row(gn_gain), row(gn_bias))

    out = _merge(x2d, y_ssm, y_rwkv, zg, row(b_gate), w_branch_ssm.astype(BF16),
                 w_branch_rwkv.astype(BF16), w_out.astype(BF16), row(post_gain))
    return out.reshape(batch, seq, D_MODEL)


def kernel(x, pre_gain, w_in, b_gate, conv_w, conv_b, dt_bias, a_log, d_skip, ssm_norm_gain, rwkv_mu,
           decay_w0, decay_w2, iclr_a0, iclr_a2, k_k, k_a, r_k, gn_gain, gn_bias, w_branch_ssm,
           w_branch_rwkv, w_out, post_gain):
    for layer in range(pre_gain.shape[0]):
        x = _layer(
            x, pre_gain[layer], w_in[layer], b_gate[layer], conv_w[layer], conv_b[layer], dt_bias[layer],
            a_log[layer], d_skip[layer], ssm_norm_gain[layer], rwkv_mu[layer], decay_w0[layer],
            decay_w2[layer], iclr_a0[layer], iclr_a2[layer], k_k[layer], k_a[layer], r_k[layer],
            gn_gain[layer], gn_bias[layer], w_branch_ssm[layer], w_branch_rwkv[layer], w_out[layer],
            post_gain[layer])
    return x
```

```python
import functools

import jax
import jax.numpy as jnp
from jax import lax
from jax.experimental import pallas as pl
from jax.experimental.pallas import tpu as pltpu

D_MODEL = 1024
EPS = 1e-6

SSM_INNER = 2048
SSM_HEAD_DIM = 64
SSM_HEADS = 32
SSM_STATE = 128
SSM_GROUPS = 4
SSM_GROUP_WIDTH = SSM_INNER // SSM_GROUPS
CONV_WIDTH = 4
SSM_XBC = 3072

RWKV_DIM = 1024
RWKV_HEAD_DIM = 64
RWKV_HEADS = 16
LORA = 64
GN_EPS = RWKV_HEAD_DIM * 1e-5
DECAY_SCALE = 0.6065306597126334
LOG2_E = 1.4426950408889634

LANES = 128
SUBLANES = 8

SMALL_WIDTH = 256
RW_PAD = 256
RW_COLS = 4 * RWKV_DIM + SMALL_WIDTH + RW_PAD
ZG_COLS = SSM_INNER + 2 * D_MODEL

PROJ_TM = {SSM_XBC: 1024, RW_COLS: 2048, ZG_COLS: 2048}
PROJ_SUBTILE = 256
PROJ_TN = {SSM_XBC: 1024, RW_COLS: RW_COLS // 6, ZG_COLS: 1024}
SSD_CHUNK = 128
SSD_CHUNKS_PER_STEP = 4
RWKV_CHUNK = 64
RWKV_BATCH_ROWS = 2
RWKV_CHUNKS_PER_STEP = 2
MERGE_TM = 512
VMEM_LIMIT = 48 * 1024 * 1024

BF16 = jnp.bfloat16
F32 = jnp.float32


def _dot(a, b):
    return jnp.dot(a, b, preferred_element_type=F32)


def _dot_nt(a, b):
    return lax.dot_general(a, b, (((1,), (1,)), ((), ())), preferred_element_type=F32)


def _dot_tn(a, b):
    return lax.dot_general(a, b, (((0,), (0,)), ((), ())), preferred_element_type=F32)


def _split_terms(x, n):
    terms = []
    rem = x
    for _ in range(n):
        t = rem.astype(BF16)
        terms.append(t)
        rem = rem - t.astype(F32)
    return terms


def _dot_left_exact(m_bf16, x, n=3):
    acc = None
    for t in _split_terms(x, n):
        p = _dot(m_bf16, t)
        acc = p if acc is None else acc + p
    return acc


def _two_terms_on_lanes(x):
    lane = lax.broadcasted_iota(jnp.int32, x.shape, 1)
    xm = jnp.where(lane < SSM_HEADS, x, 0.0)
    hi = xm.astype(BF16).astype(F32)
    return (hi + pltpu.roll(xm - hi, SSM_HEADS, 1)).astype(BF16)


def _softplus(x):
    return jnp.maximum(x, 0.0) + jnp.log(1.0 + jnp.exp(-jnp.abs(x)))


def _sigmoid(x):
    return 0.5 * jnp.tanh(0.5 * x) + 0.5


def _silu(x):
    h = 0.5 * x
    return h + h * jnp.tanh(h)


def _lower_tri(n, strict):
    row = lax.broadcasted_iota(jnp.int32, (n, n), 0)
    col = lax.broadcasted_iota(jnp.int32, (n, n), 1)
    return (col < row) if strict else (col <= row)


def _lower_tri_ones(n):
    row = lax.broadcasted_iota(jnp.int32, (n, n), 0)
    col = lax.broadcasted_iota(jnp.int32, (n, n), 1)
    return jnp.clip(row - col + 1, 0, 1).astype(F32).astype(BF16)


def _normed_input(x_ref, gain_ref, h_ref):
    @pl.when(pl.program_id(1) == 0)
    def _():
        x = x_ref[...]
        ms = jnp.mean(x * x, axis=-1, keepdims=True)
        h_ref[...] = (x * lax.rsqrt(ms + EPS) * gain_ref[...]).astype(BF16)


def _delayed(o, halo, k):
    head = jnp.concatenate([halo, o[:SUBLANES]], axis=0)[SUBLANES - k:2 * SUBLANES - k]
    if o.shape[0] == SUBLANES:
        return head
    return jnp.concatenate([head, pltpu.roll(o, k, 0)[SUBLANES:]], axis=0)


def _zero_carry_at_sequence_start(carry_ref, row_blocks_per_seq):
    i, j = pl.program_id(0), pl.program_id(1)

    @pl.when(i % row_blocks_per_seq == 0)
    def _():
        carry_ref[j] = jnp.zeros(carry_ref.shape[1:], F32)


def _proj_plain_kernel(h_ref, w_ref, o_ref):
    o_ref[...] = _dot(h_ref[...], w_ref[...])


def _proj_conv_kernel(row_blocks_per_seq, x_ref, gain_ref, w_ref, cw_ref, cb_ref, o_ref, h_ref, carry_ref):
    _normed_input(x_ref, gain_ref, h_ref)
    _zero_carry_at_sequence_start(carry_ref, row_blocks_per_seq)
    j = pl.program_id(1)
    for c0 in range(0, o_ref.shape[1], PROJ_SUBTILE):
        cs = slice(c0, c0 + PROJ_SUBTILE)
        o = _dot(h_ref[...], w_ref[:, cs])
        halo = carry_ref[j, :, cs]
        carry_ref[j, :, cs] = o[o.shape[0] - SUBLANES:, :]
        w0, w1, w2, w3 = (0.5 * cw_ref[t:t + 1, cs] for t in range(CONV_WIDTH))
        e = _delayed(o, halo, 1)
        e_halo = _delayed(halo, jnp.zeros_like(halo), 1)
        inner = w1 * o + w0 * e
        inner_halo = w1 * halo + w0 * e_halo
        h = 0.5 * cb_ref[:, cs] + w3 * o + w2 * e + _delayed(inner, inner_halo, 2)
        o_ref[:, cs] = h + h * jnp.tanh(h)


def _proj_shift_kernel(row_blocks_per_seq, h_ref, w_ref, mu_ref, o_ref, carry_ref):
    _zero_carry_at_sequence_start(carry_ref, row_blocks_per_seq)
    j = pl.program_id(1)
    o = _dot(h_ref[...], w_ref[...])
    halo = carry_ref[j]
    carry_ref[j] = o[o.shape[0] - SUBLANES:, :]
    o_ref[...] = o + mu_ref[...] * (_delayed(o, halo, 1) - o)


def _project_normed(kernel_fn, seq, x2d, pre_gain, w, col_params):
    t, n = x2d.shape[0], w.shape[1]
    tm, tn = PROJ_TM[n], PROJ_TN[n]
    return pl.pallas_call(
        functools.partial(kernel_fn, seq // tm),
        grid=(t // tm, n // tn),
        in_specs=[
            pl.BlockSpec((tm, D_MODEL), lambda i, j: (i, 0)),
            pl.BlockSpec((1, D_MODEL), lambda i, j: (0, 0)),
            pl.BlockSpec((D_MODEL, tn), lambda i, j: (0, j)),
        ] + [pl.BlockSpec((c.shape[0], tn), lambda i, j: (0, j)) for c in col_params],
        out_specs=[pl.BlockSpec((tm, tn), lambda i, j: (i, j)),
                   pl.BlockSpec((tm, D_MODEL), lambda i, j: (i, 0))],
        out_shape=[jax.ShapeDtypeStruct((t, n), F32), jax.ShapeDtypeStruct((t, D_MODEL), BF16)],
        scratch_shapes=[pltpu.VMEM((n // tn, SUBLANES, tn), F32)],
        compiler_params=pltpu.CompilerParams(
            dimension_semantics=("arbitrary", "arbitrary"), vmem_limit_bytes=VMEM_LIMIT),
        name="in_proj_%d" % n,
    )(x2d, pre_gain, w, *col_params)


def _project(kernel_fn, seq, h, w, col_params, carry):
    t, n = h.shape[0], w.shape[1]
    tm, tn = PROJ_TM[n], PROJ_TN[n]
    return pl.pallas_call(
        functools.partial(kernel_fn, seq // tm) if carry else kernel_fn,
        grid=(t // tm, n // tn),
        in_specs=[
            pl.BlockSpec((tm, D_MODEL), lambda i, j: (i, 0)),
            pl.BlockSpec((D_MODEL, tn), lambda i, j: (0, j)),
        ] + [pl.BlockSpec((c.shape[0], tn), lambda i, j: (0, j)) for c in col_params],
        out_specs=pl.BlockSpec((tm, tn), lambda i, j: (i, j)),
        out_shape=jax.ShapeDtypeStruct((t, n), F32),
        scratch_shapes=[pltpu.VMEM((n // tn, SUBLANES, tn), F32)] if carry else [],
        compiler_params=pltpu.CompilerParams(
            dimension_semantics=("arbitrary", "arbitrary"), vmem_limit_bytes=VMEM_LIMIT),
        name="in_proj_%d" % n,
    )(h, w, *col_params)


def _ssd_kernel(u_ref, z_ref, sm_ref, dtb_ref, alog_ref, dskip_ref, gain_ref, expand_ref,
                y_ref, st_ref, yd_ref):
    L = SSD_CHUNK

    @pl.when(pl.program_id(1) == 0)
    def _():
        st_ref[...] = jnp.zeros_like(st_ref)

    tri = _lower_tri(L, strict=False)
    tri_ones = _lower_tri_ones(L)
    head_a = lax.broadcasted_iota(jnp.int32, (L, LANES), 1) < SSM_HEAD_DIM
    for ci in range(SSD_CHUNKS_PER_STEP):
        rows = slice(ci * L, (ci + 1) * L)
        dt = _softplus(sm_ref[rows, LANES:2 * LANES] + dtb_ref[...])
        a = dt * (-LOG2_E * jnp.exp(alog_ref[...]))
        acum = _dot_left_exact(tri_ones, a)
        acum_t = acum.T
        dt_t = dt.T
        a_last = acum[L - 1:L, :]
        dec_out_p = _two_terms_on_lanes(jnp.exp2(acum))
        w_state_p = _two_terms_on_lanes(dt * jnp.exp2(a_last - acum))

        for g in range(SSM_GROUPS):
            c0 = g * SSM_GROUP_WIDTH
            ex = expand_ref[:, c0:c0 + SSM_GROUP_WIDTH]
            xs = u_ref[rows, c0:c0 + SSM_GROUP_WIDTH]
            b0 = SSM_INNER + g * SSM_STATE
            bm = u_ref[rows, b0:b0 + SSM_STATE].astype(BF16)
            cm = u_ref[rows, b0 + SSM_GROUPS * SSM_STATE:b0 + (SSM_GROUPS + 1) * SSM_STATE].astype(BF16)
            dec_out_x = _dot(dec_out_p, ex)
            w_state_x = _dot(w_state_p, ex)
            xs_b = xs.astype(BF16)

            scores = _dot_nt(cm, bm)
            state = st_ref[g]
            y_off = _dot(cm, state.astype(BF16)) * dec_out_x
            st_ref[g] = state * dec_out_x[L - 1:L, :] + _dot_tn(bm, (xs * w_state_x).astype(BF16))

            def mix_of(h):
                seg = acum[:, h:h + 1] - acum_t[h:h + 1, :]
                return (scores * jnp.exp2(jnp.where(tri, seg, -jnp.inf)) * dt_t[h:h + 1, :]).astype(BF16)

            for jp in range(SSM_GROUP_WIDTH // LANES):
                h = g * (SSM_GROUP_WIDTH // SSM_HEAD_DIM) + 2 * jp
                xp = xs_b[:, jp * LANES:(jp + 1) * LANES]
                x2 = jnp.concatenate([jnp.where(head_a, xp, 0.0), jnp.where(head_a, 0.0, xp)], axis=0)
                yd_ref[ci, :, jp * LANES:(jp + 1) * LANES] = _dot(
                    jnp.concatenate([mix_of(h), mix_of(h + 1)], axis=1), x2)

            y = yd_ref[ci] + y_off + dskip_ref[:, c0:c0 + SSM_GROUP_WIDTH] * xs
            z = z_ref[rows, c0:c0 + SSM_GROUP_WIDTH]
            y = y * _silu(z)
            ms = jnp.mean(y * y, axis=-1, keepdims=True)
            y_ref[rows, c0:c0 + SSM_GROUP_WIDTH] = (
                y * lax.rsqrt(ms + EPS) * gain_ref[:, c0:c0 + SSM_GROUP_WIDTH]).astype(BF16)


def _ssd_branch(u, zg, rw, batch, seq, dt_bias, a_log, dskip_x, norm_gain, expand):
    L = SSD_CHUNK * SSD_CHUNKS_PER_STEP
    nc = seq // L
    row = lambda b, c: b * nc + c
    const = lambda shape: pl.BlockSpec(shape, lambda b, c: (0, 0))
    return pl.pallas_call(
        _ssd_kernel,
        grid=(batch, nc),
        in_specs=[
            pl.BlockSpec((L, SSM_XBC), lambda b, c: (row(b, c), 0)),
            pl.BlockSpec((L, SSM_INNER), lambda b, c: (row(b, c), 0)),
            pl.BlockSpec((L, SMALL_WIDTH), lambda b, c: (row(b, c), 4 * RWKV_DIM // SMALL_WIDTH)),
            const((1, LANES)),
            const((1, LANES)),
            const((1, SSM_INNER)),
            const((1, SSM_INNER)),
            const((LANES, SSM_INNER)),
        ],
        out_specs=pl.BlockSpec((L, SSM_INNER), lambda b, c: (row(b, c), 0)),
        out_shape=jax.ShapeDtypeStruct((batch * seq, SSM_INNER), BF16),
        scratch_shapes=[
            pltpu.VMEM((SSM_GROUPS, SSM_STATE, SSM_GROUP_WIDTH), F32),
            pltpu.VMEM((SSD_CHUNKS_PER_STEP, SSD_CHUNK, SSM_GROUP_WIDTH), F32),
        ],
        compiler_params=pltpu.CompilerParams(
            dimension_semantics=("arbitrary", "arbitrary"), vmem_limit_bytes=VMEM_LIMIT),
        name="ssd_branch",
    )(u, zg, rw, dt_bias, a_log, dskip_x, norm_gain, expand)


def _rwkv_kernel(r_ref, k_ref, v_ref, g_ref, sm_ref, w0_ref, w2_ref, a0_ref, a2_ref, kks_ref, kas_ref,
                 rk_ref, gng_ref, gnb_ref, y_ref, s_ref):
    C = RWKV_CHUNK
    D = RWKV_HEAD_DIM
    NB = RWKV_BATCH_ROWS
    CP = RWKV_CHUNKS_PER_STEP
    R = NB * CP * C

    @pl.when(pl.program_id(1) == 0)
    def _():
        s_ref[...] = jnp.zeros_like(s_ref)

    rows = lambda x_ref: x_ref[...].reshape(R, x_ref.shape[-1])
    r, k, v, gate, sm = rows(r_ref), rows(k_ref), rows(v_ref), rows(g_ref), rows(sm_ref)
    wd = sm[:, 0:LORA]
    ad = sm[:, LORA:2 * LORA]

    lw = _dot(jnp.tanh(wd).astype(BF16), w2_ref[...])
    logw = (-0.5 * DECAY_SCALE * LOG2_E) * jnp.tanh(0.5 * (w0_ref[...] + lw)) - 0.5 * DECAY_SCALE * LOG2_E
    th_a = jnp.tanh(0.5 * (a0_ref[...] + _dot(ad.astype(BF16), a2_ref[...])))
    neg_a = -0.5 * th_a - 0.5
    kkp = k * kks_ref[...]
    half_ka = 0.5 * kas_ref[...]
    k2 = k * ((1.0 - half_ka) + half_ka * th_a)

    tri = _lower_tri_ones(C)
    cum = jnp.concatenate([_dot_left_exact(tri, logw[q * C:(q + 1) * C]) for q in range(NB * CP)], axis=0)
    e_pos = jnp.exp2(cum)
    e_neg = 1.0 / e_pos
    kc_p = kkp * jnp.exp2(cum - logw)
    rt = r * e_pos
    kt = k2 * e_neg
    nbt_p = kkp * neg_a * e_neg
    gamma_end = [jnp.exp2(cum[(q + 1) * C - 1:(q + 1) * C, :]) for q in range(NB * CP)]
    rkk = r * k2 * rk_ref[...]
    gsilu = _silu(gate)

    P = 2 * D
    lane_a = lax.broadcasted_iota(jnp.int32, (C, P), 1) < D
    lane_a2 = lax.broadcasted_iota(jnp.int32, (2 * C, P), 1) < D
    row2 = lax.broadcasted_iota(jnp.int32, (2 * C, 2 * C), 0)
    col2 = lax.broadcasted_iota(jnp.int32, (2 * C, 2 * C), 1)
    colm = jnp.where(col2 >= C, col2 - C, col2)
    gmask = colm < jnp.where(row2 >= C, row2 - C + 1, row2)
    diag_blk = jnp.where(row2 >= C, 1, 0) == jnp.where(col2 >= C, 1, 0)
    anti_eye = jnp.where(col2 == jnp.where(row2 >= C, row2 - C, row2 + C), 1.0, 0.0)

    def seg_sum(x):
        sa = jnp.sum(jnp.where(lane_a, x, 0.0), axis=-1, keepdims=True)
        sb = jnp.sum(jnp.where(lane_a, 0.0, x), axis=-1, keepdims=True)
        return jnp.where(lane_a, sa, sb)

    def cross(lo, hi):
        return jnp.concatenate([lo, hi], axis=0).astype(BF16)

    units = [(b, j) for b in range(NB) for j in range(RWKV_HEADS // 2)]
    n_units = range(len(units))

    def chunk_chain(ci):
        idx = [(slice((b * CP + ci) * C, (b * CP + ci + 1) * C), slice(j * P, (j + 1) * P)) for b, j in units]
        xb, wa, g_top, g_bot, v_p = [], [], [], [], []
        for rs, sl in idx:
            kkp_p = kkp[rs, sl]
            rinv = jnp.minimum(lax.rsqrt(seg_sum(kkp_p * kkp_p)), 1e12)
            nbt = nbt_p[rs, sl] * rinv
            xp = jnp.concatenate([kc_p[rs, sl] * rinv, rt[rs, sl]], axis=0)
            w_a = jnp.concatenate([nbt, kt[rs, sl]], axis=0).astype(BF16)
            w_b = jnp.concatenate([kt[rs, sl], nbt], axis=0).astype(BF16)
            g_a = jnp.where(gmask, _dot_nt(jnp.where(lane_a2, xp, 0.0).astype(BF16), w_a), 0.0)
            g_b = jnp.where(gmask, _dot_nt(jnp.where(lane_a2, 0.0, xp).astype(BF16), w_b), 0.0)
            xb.append(xp.astype(BF16))
            wa.append(w_a)
            g_top.append(jnp.concatenate([g_a[:C], g_b[:C]], axis=0))
            g_bot.append(jnp.concatenate([g_a[C:], g_b[C:]], axis=0))
            v_p.append(v[rs, sl])

        z = [_dot_nt(xb[i], s_ref[units[i][0], units[i][1]].astype(BF16)) for i in n_units]
        w1 = []
        for i in n_units:
            rw = _dot(g_top[i].astype(BF16),
                      cross(jnp.where(lane_a, 0.0, v_p[i]), jnp.where(lane_a, v_p[i], 0.0)))
            w1.append(jnp.where(lane_a, rw[:C], rw[C:]))

        pt = [jnp.where(diag_blk, g_top[i], anti_eye) for i in n_units]
        for _ in range(6):
            pt = [_dot(jnp.where(diag_blk, pt[i], 0.0).astype(BF16), pt[i].astype(BF16))
                  + jnp.where(diag_blk, 0.0, pt[i]) for i in n_units]

        u = []
        for i in n_units:
            rhs = z[i][:C] + w1[i]
            ru = _dot(pt[i].astype(BF16), cross(jnp.where(lane_a, 0.0, rhs), jnp.where(lane_a, rhs, 0.0)))
            u.append(ru[:C] + ru[C:])
        for i in n_units:
            (b, j), (rs, sl) = units[i], idx[i]
            ry = _dot(g_bot[i].astype(BF16),
                      cross(jnp.where(lane_a, u[i], v_p[i]), jnp.where(lane_a, v_p[i], u[i])))
            y = z[i][C:] + jnp.where(lane_a, ry[:C], ry[C:])
            upd = _dot_tn(jnp.concatenate([u[i], v_p[i]], axis=0).astype(BF16), wa[i])
            s_ref[b, j] = (s_ref[b, j] + jnp.where(diag_blk, upd, 0.0)) * gamma_end[b * CP + ci][:, sl]

            yc = y - seg_sum(y) * (1.0 / D)
            var = seg_sum(yc * yc) * (1.0 / D)
            yn = yc * lax.rsqrt(var + GN_EPS) * gng_ref[:, sl] + gnb_ref[:, sl]
            bonus = seg_sum(rkk[rs, sl]) * v_p[i]
            y_ref[b, ci * C:(ci + 1) * C, sl] = ((yn + bonus) * gsilu[rs, sl]).astype(BF16)

    for ci in range(CP):
        chunk_chain(ci)


def _rwkv_branch(rw, batch, seq, w0, w2, a0, a2, k_k, k_a, r_k, gn_gain, gn_bias):
    C = RWKV_CHUNK
    NB = RWKV_BATCH_ROWS
    CP = RWKV_CHUNKS_PER_STEP
    rw3 = rw.reshape(batch, seq, RW_COLS)
    const = lambda shape: pl.BlockSpec(shape, lambda b, c: (0, 0))
    col_block = lambda j: pl.BlockSpec((NB, CP * C, RWKV_DIM), lambda b, c: (b, c, j))
    return pl.pallas_call(
        _rwkv_kernel,
        grid=(batch // NB, seq // (CP * C)),
        in_specs=[
            col_block(0), col_block(1), col_block(2), col_block(3),
            pl.BlockSpec((NB, CP * C, SMALL_WIDTH), lambda b, c: (b, c, 4 * RWKV_DIM // SMALL_WIDTH)),
            const((1, RWKV_DIM)), const((LORA, RWKV_DIM)), const((1, RWKV_DIM)), const((LORA, RWKV_DIM)),
            const((1, RWKV_DIM)), const((1, RWKV_DIM)), const((1, RWKV_DIM)),
            const((1, RWKV_DIM)), const((1, RWKV_DIM)),
        ],
        out_specs=pl.BlockSpec((NB, CP * C, RWKV_DIM), lambda b, c: (b, c, 0)),
        out_shape=jax.ShapeDtypeStruct((batch, seq, RWKV_DIM), BF16),
        scratch_shapes=[
            pltpu.VMEM((NB, RWKV_HEADS // 2, 2 * RWKV_HEAD_DIM, 2 * RWKV_HEAD_DIM), F32),
        ],
        compiler_params=pltpu.CompilerParams(
            dimension_semantics=("arbitrary", "arbitrary"), vmem_limit_bytes=VMEM_LIMIT),
        name="rwkv_branch",
    )(rw3, rw3, rw3, rw3, rw3, w0, w2, a0, a2, k_k, k_a, r_k, gn_gain, gn_bias
      ).reshape(batch * seq, RWKV_DIM)


def _merge_kernel(x_ref, ys_ref, yr_ref, g0_ref, g1_ref, bg_ref, ws_ref, wr_ref, wo_ref, gain_ref, o_ref):
    g_ssm = _sigmoid(g0_ref[...] + bg_ref[:, 0:D_MODEL])
    g_rwkv = _sigmoid(g1_ref[...] + bg_ref[:, D_MODEL:2 * D_MODEL])
    merged = g_ssm * _dot(ys_ref[...], ws_ref[...]) + g_rwkv * _dot(yr_ref[...], wr_ref[...])
    out = _dot(merged.astype(BF16), wo_ref[...])
    ms = jnp.mean(out * out, axis=-1, keepdims=True)
    o_ref[...] = x_ref[...] + out * lax.rsqrt(ms + EPS) * gain_ref[...]


def _merge(x2d, y_ssm, y_rwkv, zg, b_gate, w_ssm, w_rwkv, w_out, post_gain):
    t = x2d.shape[0]
    tm = MERGE_TM
    const = lambda shape: pl.BlockSpec(shape, lambda i: (0, 0), pipeline_mode=pl.Buffered(1))
    return pl.pallas_call(
        _merge_kernel,
        grid=(t // tm,),
        in_specs=[
            pl.BlockSpec((tm, D_MODEL), lambda i: (i, 0)),
            pl.BlockSpec((tm, SSM_INNER), lambda i: (i, 0)),
            pl.BlockSpec((tm, RWKV_DIM), lambda i: (i, 0)),
            pl.BlockSpec((tm, D_MODEL), lambda i: (i, SSM_INNER // D_MODEL)),
            pl.BlockSpec((tm, D_MODEL), lambda i: (i, SSM_INNER // D_MODEL + 1)),
            const((1, 2 * D_MODEL)),
            const((SSM_INNER, D_MODEL)), const((RWKV_DIM, D_MODEL)), const((D_MODEL, D_MODEL)),
            const((1, D_MODEL)),
        ],
        out_specs=pl.BlockSpec((tm, D_MODEL), lambda i: (i, 0)),
        out_shape=jax.ShapeDtypeStruct((t, D_MODEL), F32),
        compiler_params=pltpu.CompilerParams(
            dimension_semantics=("arbitrary",), vmem_limit_bytes=VMEM_LIMIT),
        name="gated_merge",
    )(x2d, y_ssm, y_rwkv, zg, zg, b_gate, w_ssm, w_rwkv, w_out, post_gain)


def _pad_lanes(v, width):
    return jnp.pad(v, ((0, 0), (0, width - v.shape[-1])))


W_XBC = SSM_INNER
W_DT = W_XBC + SSM_XBC
W_RW = W_DT + SSM_HEADS
W_GATE = W_RW + 4 * RWKV_DIM + 2 * LORA
W_COLS = W_GATE + 2 * D_MODEL
WEIGHT_ROWS = 128


def _lane_aligned(col):
    return col // LANES * LANES


def _weights_kernel(w_ref, xbc_ref, rw_ref, zg_ref):
    xbc_ref[...] = w_ref[:, W_XBC:W_DT]
    zg_ref[:, 0:SSM_INNER] = w_ref[:, 0:W_XBC]
    g0 = _lane_aligned(W_GATE)
    zg_ref[:, SSM_INNER:ZG_COLS] = w_ref[:, g0:W_COLS][:, W_GATE - g0:W_COLS - g0]
    n_rw = W_GATE - W_RW
    r0 = _lane_aligned(W_RW)
    r1 = _lane_aligned(W_GATE + LANES - 1)
    rw_ref[:, 0:n_rw] = w_ref[:, r0:r1][:, W_RW - r0:W_GATE - r0]
    dt_tile = w_ref[:, W_DT:W_DT + LANES]
    lane = lax.broadcasted_iota(jnp.int32, dt_tile.shape, 1)
    rw_ref[:, n_rw:n_rw + LANES] = jnp.where(lane < SSM_HEADS, dt_tile, jnp.zeros_like(dt_tile))
    rw_ref[:, n_rw + LANES:RW_COLS] = jnp.zeros((w_ref.shape[0], RW_COLS - n_rw - LANES), BF16)


def _split_projection_weights(w_in):
    rows = lambda n: pl.BlockSpec((WEIGHT_ROWS, n), lambda i: (i, 0))
    return pl.pallas_call(
        _weights_kernel,
        grid=(D_MODEL // WEIGHT_ROWS,),
        in_specs=[rows(W_COLS)],
        out_specs=[rows(SSM_XBC), rows(RW_COLS), rows(ZG_COLS)],
        out_shape=[jax.ShapeDtypeStruct((D_MODEL, n), BF16) for n in (SSM_XBC, RW_COLS, ZG_COLS)],
        compiler_params=pltpu.CompilerParams(dimension_semantics=("arbitrary",), vmem_limit_bytes=VMEM_LIMIT),
        name="weight_split",
    )(w_in)


def _layer(x, pre_gain, w_in, b_gate, conv_w, conv_b, dt_bias, a_log, d_skip, ssm_norm_gain,
           rwkv_mu, decay_w0, decay_w2, iclr_a0, iclr_a2, k_k, k_a, r_k, gn_gain, gn_bias,
           w_branch_ssm, w_branch_rwkv, w_out, post_gain):
    batch, seq, _ = x.shape
    x2d = x.reshape(batch * seq, D_MODEL)
    row = lambda v: v.reshape(1, -1)

    w_xbc, w_rw, w_zg = _split_projection_weights(w_in.astype(BF16))
    mu = _pad_lanes(row(rwkv_mu), RW_COLS)

    gain = row(pre_gain)
    u, h = _project_normed(_proj_conv_kernel, seq, x2d, gain, w_xbc, [conv_w, row(conv_b)])
    rw = _project(_proj_shift_kernel, seq, h, w_rw, [mu], True)
    zg = _project(_proj_plain_kernel, seq, h, w_zg, [], False)

    head_of_col = jnp.arange(SSM_INNER) // SSM_HEAD_DIM
    expand = ((jnp.arange(LANES)[:, None] % SSM_HEADS == head_of_col[None, :])
              & (jnp.arange(LANES)[:, None] < 2 * SSM_HEADS)).astype(BF16)
    y_ssm = _ssd_branch(
        u, zg, rw, batch, seq, _pad_lanes(row(dt_bias), LANES), _pad_lanes(row(a_log), LANES),
        row(jnp.repeat(d_skip, SSM_HEAD_DIM)), row(ssm_norm_gain), expand)

    y_rwkv = _rwkv_branch(
        rw, batch, seq, row(decay_w0), decay_w2.astype(BF16), row(iclr_a0), iclr_a2.astype(BF16),
        row(k_k), row(k_a), row(r_k), row(gn_gain), row(gn_bias))

    out = _merge(x2d, y_ssm, y_rwkv, zg, row(b_gate), w_branch_ssm.astype(BF16),
                 w_branch_rwkv.astype(BF16), w_out.astype(BF16), row(post_gain))
    return out.reshape(batch, seq, D_MODEL)


def kernel(x, pre_gain, w_in, b_gate, conv_w, conv_b, dt_bias, a_log, d_skip, ssm_norm_gain, rwkv_mu,
           decay_w0, decay_w2, iclr_a0, iclr_a2, k_k, k_a, r_k, gn_gain, gn_bias, w_branch_ssm,
           w_branch_rwkv, w_out, post_gain):
    for layer in range(pre_gain.shape[0]):
        x = _layer(
            x, pre_gain[layer], w_in[layer], b_gate[layer], conv_w[layer], conv_b[layer], dt_bias[layer],
            a_log[layer], d_skip[layer], ssm_norm_gain[layer], rwkv_mu[layer], decay_w0[layer],
            decay_w2[layer], iclr_a0[layer], iclr_a2[layer], k_k[layer], k_a[layer], r_k[layer],
            gn_gain[layer], gn_bias[layer], w_branch_ssm[layer], w_branch_rwkv[layer], w_out[layer],
            post_gain[layer])
    return x
```
